```python
import jax, jax.numpy as jnp
from jax import lax
import numpy as np

D_MODEL = 1024
BATCH = 4
SEQ = 8192
DEPTH = 2

N_A = DEPTH // 2
N_B = DEPTH - N_A
MEM_LEN = 256
MEM_HEADS = 4
MEM_DH = D_MODEL // (2 * MEM_HEADS)
MEM_WIDTH = MEM_HEADS * MEM_DH
GLA_HEADS = 4
GLA_DV = D_MODEL // (2 * GLA_HEADS)
GLA_DK = GLA_DV // 2
GLA_QK = GLA_HEADS * GLA_DK
GLA_V = GLA_HEADS * GLA_DV
GLA_RANK = 16
GLA_TAU = 16.0
GLA_CHUNK = 64
NSA_HEADS = 8
NSA_GROUPS = 2
NSA_HPG = NSA_HEADS // NSA_GROUPS
NSA_DH = D_MODEL // (2 * NSA_HEADS)
NSA_WIDTH = NSA_HEADS * NSA_DH
KV_WIDTH = NSA_GROUPS * NSA_DH
CMP_BLOCK = 32
CMP_STRIDE = 16
CMP_HIDDEN = 256
SEL_BLOCK = 64
SEL_TOPN = 16
WINDOW = 512
Q_BLOCK = 128
FFN_DIM = ((8 * D_MODEL // 3 + 127) // 128) * 128
CONV_WIDTH = 3
EPS = 1e-6
A_PROJ = 2 * GLA_QK + 2 * GLA_V + GLA_RANK + MEM_WIDTH
B_PROJ = NSA_WIDTH + 3 * NSA_HEADS + MEM_WIDTH
SHARED_KV_PROJ = 6 * KV_WIDTH

kernel_name = 'yoco_gla_nsa_hybrid'

F32 = jnp.float32


def rmsnorm(x, g):
    xf = x.astype(F32)
    y = xf * lax.rsqrt(jnp.mean(xf * xf, axis=-1, keepdims=True) + EPS)
    return y.astype(x.dtype) * g


def split_cols(z, widths):
    idx = [int(i) for i in np.cumsum(widths)[:-1]]
    return jnp.split(z, idx, axis=-1)


def masked_softmax(s, mask):
    s = jnp.where(mask, s.astype(F32), -jnp.inf)
    m = jnp.max(s, axis=-1, keepdims=True)
    m = jnp.where(jnp.isfinite(m), m, 0.0)
    e = jnp.exp(s - m)
    return e / jnp.maximum(jnp.sum(e, axis=-1, keepdims=True), 1e-30)


def alibi_slopes(n):
    return jnp.exp2(-8.0 * jnp.arange(1, n + 1, dtype=F32) / n)


def mem_attend(mq, mem_k, mem_v):
    B, T, _ = mq.shape
    qh = mq.reshape(B, T, MEM_HEADS, MEM_DH) * (MEM_DH ** -0.5)
    s = jnp.einsum('bthd,bmhd->bhtm', qh, mem_k)
    p = jax.nn.softmax(s.astype(F32), axis=-1).astype(mem_v.dtype)
    return jnp.einsum('bhtm,bmhd->bthd', p, mem_v).reshape(B, T, MEM_WIDTH)


def gla_chunked(q, k, v, log_a):
    B, T, H, DK = q.shape
    DV = v.shape[-1]
    C = GLA_CHUNK
    NC = T // C

    def chunks(a):
        return a.astype(F32).reshape(B, NC, C, H, a.shape[-1]).transpose(1, 0, 3, 2, 4)

    causal = jnp.tril(jnp.ones((C, C), dtype=bool))[:, :, None]

    def step(S, inp):
        qc, kc, vc, gc = inp
        b = jnp.cumsum(gc, axis=2)
        b_last = b[:, :, -1:, :]
        o_inter = jnp.einsum('bhtd,bhde->bhte', qc * jnp.exp(b), S)
        decay = jnp.exp(jnp.where(causal, b[:, :, :, None, :] - b[:, :, None, :, :], -jnp.inf))
        attn = jnp.einsum('bhtd,bhsd,bhtsd->bhts', qc, kc, decay)
        o = o_inter + jnp.einsum('bhts,bhse->bhte', attn, vc)
        S = jnp.exp(b_last[:, :, 0, :])[..., None] * S + jnp.einsum('bhsd,bhse->bhde', kc * jnp.exp(b_last - b), vc)
        return S, o

    S0 = jnp.zeros((B, H, DK, DV), F32)
    _, o = lax.scan(step, S0, (chunks(q), chunks(k), chunks(v), chunks(log_a)))
    return o.transpose(1, 0, 3, 2, 4).reshape(B, T, H, DV)


def gla_layer_mix(h, mem_k, mem_v, w_in, w_alpha, b_alpha, g_head, w_out):
    B, T, _ = h.shape
    q, k, v, r, alr, mq = split_cols(h @ w_in, [GLA_QK, GLA_QK, GLA_V, GLA_V, GLA_RANK, MEM_WIDTH])
    q = q.reshape(B, T, GLA_HEADS, GLA_DK) * (GLA_DK ** -0.5)
    k = k.reshape(B, T, GLA_HEADS, GLA_DK)
    v = v.reshape(B, T, GLA_HEADS, GLA_DV)
    log_a = jax.nn.log_sigmoid((alr @ w_alpha + b_alpha).astype(F32)) / GLA_TAU
    log_a = log_a.reshape(B, T, GLA_HEADS, GLA_DK)
    o = gla_chunked(q, k, v, log_a).astype(h.dtype)
    o = rmsnorm(o, g_head).reshape(B, T, GLA_V) * jax.nn.silu(r)
    m = mem_attend(mq, mem_k, mem_v)
    return jnp.concatenate([o, m], axis=-1) @ w_out


def compress_blocks(a, pe, w1, w2):
    B, T, G, DH = a.shape
    n_sub = CMP_BLOCK // CMP_STRIDE
    sub = a.reshape(B, T // CMP_STRIDE, CMP_STRIDE, G, DH)
    ncmp = T // CMP_STRIDE - n_sub + 1
    blocks = jnp.concatenate([sub[:, i:i + ncmp] for i in range(n_sub)], axis=2)
    blocks = blocks + pe[:, None, :]
    flat = blocks.transpose(0, 1, 3, 2, 4).reshape(B, ncmp, G, CMP_BLOCK * DH)
    return jax.nn.gelu(flat @ w1) @ w2


def nsa_shared_kv(x, g_kv, w_kv, pe_k, pe_v, w_ck1, w_ck2, w_cv1, w_cv2):
    B, T, _ = x.shape
    kv = (rmsnorm(x, g_kv) @ w_kv).reshape(B, T, 6, NSA_GROUPS, NSA_DH)
    k_c, v_c, k_s, v_s, k_w, v_w = [kv[:, :, i] for i in range(6)]
    return (compress_blocks(k_c, pe_k, w_ck1, w_ck2), compress_blocks(v_c, pe_v, w_cv1, w_cv2),
            k_s, v_s, k_w, v_w)


def selection_overlap(ncmp, ns):
    cs = jnp.arange(ncmp) * CMP_STRIDE
    ss = jnp.arange(ns) * SEL_BLOCK
    ov = jnp.minimum(cs[:, None] + CMP_BLOCK, ss[None, :] + SEL_BLOCK) - jnp.maximum(cs[:, None], ss[None, :])
    return jnp.clip(ov, 0, None).astype(F32) / CMP_BLOCK


def nsa_attend(q, gates, k_cmp, v_cmp, k_slc, v_slc, k_win, v_win):
    B, T, H, DH = q.shape
    G, HPG = NSA_GROUPS, NSA_HPG
    nqb = T // Q_BLOCK
    ncmp = k_cmp.shape[1]
    ns = T // SEL_BLOCK
    n_sel = min(SEL_TOPN, ns)
    slope = alibi_slopes(H).reshape(G, HPG)[None, None, :, :, None]
    cmp_end = jnp.arange(ncmp) * CMP_STRIDE + CMP_BLOCK - 1
    overlap = selection_overlap(ncmp, ns)

    def to_blocks(a):
        return a.reshape(B, ns, SEL_BLOCK, G, DH).transpose(0, 3, 1, 2, 4)

    ks_b, vs_b = to_blocks(k_slc), to_blocks(v_slc)
    pad = ((0, 0), (WINDOW, 0), (0, 0), (0, 0))
    kw_pad, vw_pad = jnp.pad(k_win, pad), jnp.pad(v_win, pad)
    gather = jax.vmap(jax.vmap(lambda blocks, ix: blocks[ix]))
    sel_off = jnp.arange(SEL_BLOCK)
    win_off = jnp.arange(Q_BLOCK + WINDOW)
    blk = jnp.arange(ns)

    def one_block(inp):
        qi, gi, bi = inp
        t = bi * Q_BLOCK + jnp.arange(Q_BLOCK)
        dist_c = (t[:, None] - cmp_end[None, :])
        s = jnp.einsum('bqghd,bngd->bqghn', qi, k_cmp).astype(F32) - slope * dist_c.astype(F32)[None, :, None, None, :]
        p_c = masked_softmax(s, (dist_c >= 0)[None, :, None, None, :])
        o_c = jnp.einsum('bqghn,bngd->bqghd', p_c.astype(v_cmp.dtype), v_cmp)
        cur = t // SEL_BLOCK
        imp = jnp.einsum('bqghn,nj->bqgj', p_c, overlap)
        valid = (blk[None, :] <= cur[:, None])[None, :, None, :]
        forced = ((blk[None, :] == 0) | (blk[None, :] == cur[:, None]) | (blk[None, :] == cur[:, None] - 1))[None, :, None, :]
        imp = jnp.where(forced, jnp.inf, jnp.where(valid, imp, -jnp.inf))
        _, idx = lax.top_k(imp, n_sel)
        idx = idx.transpose(0, 2, 1, 3)
        kg = gather(ks_b, idx).reshape(B, G, Q_BLOCK, n_sel * SEL_BLOCK, DH)
        vg = gather(vs_b, idx).reshape(B, G, Q_BLOCK, n_sel * SEL_BLOCK, DH)
        pos = (idx[..., None] * SEL_BLOCK + sel_off).reshape(B, G, Q_BLOCK, n_sel * SEL_BLOCK)
        dist_s = (t[None, None, :, None] - pos).transpose(0, 2, 1, 3)[:, :, :, None, :]
        s = jnp.einsum('bqghd,bgqsd->bqghs', qi, kg).astype(F32) - slope * dist_s.astype(F32)
        p_s = masked_softmax(s, dist_s >= 0)
        o_s = jnp.einsum('bqghs,bgqsd->bqghd', p_s.astype(vg.dtype), vg)
        start = bi * Q_BLOCK
        kw = lax.dynamic_slice_in_dim(kw_pad, start, Q_BLOCK + WINDOW, axis=1)
        vw = lax.dynamic_slice_in_dim(vw_pad, start, Q_BLOCK + WINDOW, axis=1)
        kpos = start - WINDOW + win_off
        dist_w = t[:, None] - kpos[None, :]
        mask_w = ((dist_w >= 0) & (dist_w < WINDOW) & (kpos >= 0)[None, :])[None, :, None, None, :]
        s = jnp.einsum('bqghd,bkgd->bqghk', qi, kw).astype(F32) - slope * dist_w.astype(F32)[None, :, None, None, :]
        p_w = masked_softmax(s, mask_w)
        o_w = jnp.einsum('bqghk,bkgd->bqghd', p_w.astype(vw.dtype), vw)
        return gi[..., 0:1] * o_c + gi[..., 1:2] * o_s + gi[..., 2:3] * o_w

    qb = q.reshape(B, nqb, Q_BLOCK, G, HPG, DH).transpose(1, 0, 2, 3, 4, 5)
    gb = gates.reshape(B, nqb, Q_BLOCK, G, HPG, 3).transpose(1, 0, 2, 3, 4, 5).astype(q.dtype)
    out = lax.map(one_block, (qb, gb, jnp.arange(nqb)))
    return out.transpose(1, 0, 2, 3, 4, 5).reshape(B, T, H * DH)


def nsa_layer_mix(h, shared, mem_k, mem_v, w_in, w_out):
    B, T, _ = h.shape
    q, gl, mq = split_cols(h @ w_in, [NSA_WIDTH, 3 * NSA_HEADS, MEM_WIDTH])
    q = q.reshape(B, T, NSA_HEADS, NSA_DH) * (NSA_DH ** -0.5)
    gates = jax.nn.sigmoid(gl.reshape(B, T, NSA_HEADS, 3))
    o = nsa_attend(q, gates, *shared)
    m = mem_attend(mq, mem_k, mem_v)
    return jnp.concatenate([o, m], axis=-1) @ w_out


def conv_ffn(h, w_up, conv_w, conv_b, w_down):
    T = h.shape[1]
    a, b = jnp.split(h @ w_up, 2, axis=-1)
    a_pad = jnp.pad(a, ((0, 0), (CONV_WIDTH - 1, 0), (0, 0)))
    a = sum(a_pad[:, j:j + T] * conv_w[j] for j in range(CONV_WIDTH)) + conv_b
    return (jax.nn.silu(a) * b) @ w_down


def setup_inputs(seed: int = 0) -> dict:
    key = jax.random.key(seed)
    ks = jax.random.split(key, 26)

    def nrm(k, shape, scale):
        return jax.random.normal(k, shape, F32) * scale

    def gain(k, shape):
        return 1.0 + 0.02 * jax.random.normal(k, shape, F32)

    D = D_MODEL
    return {
        'x': nrm(ks[0], (BATCH, SEQ, D), 1.0),
        'mem': nrm(ks[1], (BATCH, MEM_LEN, D), 1.0),
        'g_mix': gain(ks[2], (DEPTH, D)),
        'g_ffn': gain(ks[3], (DEPTH, D)),
        'g_mem': gain(ks[4], (DEPTH, D)),
        'w_mem_kv': nrm(ks[5], (DEPTH, D, 2 * MEM_WIDTH), D ** -0.5),
        'w_up': nrm(ks[6], (DEPTH, D, 2 * FFN_DIM), D ** -0.5),
        'conv_w': nrm(ks[7], (DEPTH, CONV_WIDTH, FFN_DIM), CONV_WIDTH ** -0.5),
        'conv_b': nrm(ks[8], (DEPTH, FFN_DIM), 0.01),
        'w_down': nrm(ks[9], (DEPTH, FFN_DIM, D), FFN_DIM ** -0.5),
        'a_w_in': nrm(ks[10], (N_A, D, A_PROJ), D ** -0.5),
        'a_w_alpha': nrm(ks[11], (N_A, GLA_RANK, GLA_QK), GLA_RANK ** -0.5),
        'a_b_alpha': nrm(ks[12], (N_A, GLA_QK), 0.01),
        'a_g_head': gain(ks[13], (N_A, GLA_DV)),
        'a_w_out': nrm(ks[14], (N_A, GLA_V + MEM_WIDTH, D), (GLA_V + MEM_WIDTH) ** -0.5),
        'g_kv': gain(ks[15], (D,)),
        'w_kv': nrm(ks[16], (D, SHARED_KV_PROJ), D ** -0.5),
        'pe_k': nrm(ks[17], (CMP_BLOCK, NSA_DH), 0.1),
        'pe_v': nrm(ks[18], (CMP_BLOCK, NSA_DH), 0.1),
        'w_ck1': nrm(ks[19], (CMP_BLOCK * NSA_DH, CMP_HIDDEN), (CMP_BLOCK * NSA_DH) ** -0.5),
        'w_ck2': nrm(ks[20], (CMP_HIDDEN, NSA_DH), CMP_HIDDEN ** -0.5),
        'w_cv1': nrm(ks[21], (CMP_BLOCK * NSA_DH, CMP_HIDDEN), (CMP_BLOCK * NSA_DH) ** -0.5),
        'w_cv2': nrm(ks[22], (CMP_HIDDEN, NSA_DH), CMP_HIDDEN ** -0.5),
        'b_w_in': nrm(ks[23], (N_B, D, B_PROJ), D ** -0.5),
        'b_w_out': nrm(ks[24], (N_B, NSA_WIDTH + MEM_WIDTH, D), (NSA_WIDTH + MEM_WIDTH) ** -0.5),
        'g_final': gain(ks[25], (D,)),
    }


def reference(x, mem, g_mix, g_ffn, g_mem, w_mem_kv, w_up, conv_w, conv_b, w_down,
              a_w_in, a_w_alpha, a_b_alpha, a_g_head, a_w_out,
              g_kv, w_kv, pe_k, pe_v, w_ck1, w_ck2, w_cv1, w_cv2,
              b_w_in, b_w_out, g_final):
    B, M, _ = mem.shape
    shared = None
    for l in range(DEPTH):
        if l == N_A:
            shared = nsa_shared_kv(x, g_kv, w_kv, pe_k, pe_v, w_ck1, w_ck2, w_cv1, w_cv2)
        mkv = (rmsnorm(mem, g_mem[l]) @ w_mem_kv[l]).reshape(B, M, 2, MEM_HEADS, MEM_DH)
        mem_k, mem_v = mkv[:, :, 0], mkv[:, :, 1]
        h = rmsnorm(x, g_mix[l])
        if l < N_A:
            x = x + gla_layer_mix(h, mem_k, mem_v, a_w_in[l], a_w_alpha[l], a_b_alpha[l], a_g_head[l], a_w_out[l])
        else:
            j = l - N_A
            x = x + nsa_layer_mix(h, shared, mem_k, mem_v, b_w_in[j], b_w_out[j])
        x = x + conv_ffn(rmsnorm(x, g_ffn[l]), w_up[l], conv_w[l], conv_b[l], w_down[l])
    return rmsnorm(x, g_final)
```

```python
import functools

import numpy as np
import jax
import jax.numpy as jnp
from jax import lax
from jax.experimental import pallas as pl
from jax.experimental.pallas import tpu as pltpu

F32 = jnp.float32
BF16 = jnp.bfloat16
EPS = 1e-6
NEG_INF = float("-inf")

V7X_VMEM_BYTES = 64 * 1024 * 1024
VMEM_LIMIT = V7X_VMEM_BYTES - 8 * 1024 * 1024

D_MODEL = 1024
MEM_HEADS = 4
MEM_DH = 128
MEM_WIDTH = MEM_HEADS * MEM_DH
GLA_HEADS = 4
GLA_DK = 64
GLA_DV = 128
GLA_QK = GLA_HEADS * GLA_DK
GLA_V = GLA_HEADS * GLA_DV
GLA_RANK = 16
GLA_TAU = 16.0
GLA_CHUNK = 64
GLA_LEVELS = 6
NSA_HEADS = 8
NSA_GROUPS = 2
NSA_HPG = NSA_HEADS // NSA_GROUPS
NSA_DH = 64
NSA_WIDTH = NSA_HEADS * NSA_DH
KV_WIDTH = NSA_GROUPS * NSA_DH
CMP_BLOCK = 32
CMP_STRIDE = 16
CMP_HIDDEN = 256
SEL_BLOCK = 64
SEL_TOPN = 16
WINDOW = 512
Q_BLOCK = 128
WIN_KEYS = WINDOW + Q_BLOCK
FFN_DIM = 2816
FFN_CHUNK = 256
LANE = 128


def _dot(a, b):
    return jnp.dot(a, b, preferred_element_type=F32)


def _dot_nt(a, b):
    return lax.dot_general(a, b, (((1,), (1,)), ((), ())), preferred_element_type=F32)


def _dot_tn(a, b):
    return lax.dot_general(a, b, (((0,), (0,)), ((), ())), preferred_element_type=F32)


def _params(sem):
    return pltpu.CompilerParams(dimension_semantics=sem, vmem_limit_bytes=VMEM_LIMIT)


def _const_spec(shape):
    n = len(shape)
    return pl.BlockSpec(shape, lambda *_: (0,) * n)


def _normalize(x):
    return x * lax.rsqrt(jnp.mean(x * x, axis=-1, keepdims=True) + EPS)


def _sigmoid(x):
    return 1.0 / (1.0 + jnp.exp(-x))


def _rms_proj_kernel(x_ref, g_ref, w_ref, o_ref):
    h = (_normalize(x_ref[...]) * g_ref[...]).astype(BF16)
    o_ref[...] = _dot(h, w_ref[...]).astype(o_ref.dtype)


def _rms_proj(x, g, w, tm, out_dtype):
    n, d = x.shape
    p = w.shape[1]
    return pl.pallas_call(
        _rms_proj_kernel,
        out_shape=jax.ShapeDtypeStruct((n, p), out_dtype),
        grid=(n // tm,),
        in_specs=[pl.BlockSpec((tm, d), lambda i: (i, 0)), _const_spec((1, d)), _const_spec((d, p))],
        out_specs=pl.BlockSpec((tm, p), lambda i: (i, 0)),
        compiler_params=_params(("parallel",)),
        name="rms_proj",
    )(x, g, w)


def _inproj_a_kernel(x_ref, g_ref, w_ref, wa_ref, ba_ref, q_ref, k_ref, gl_ref, v_ref, r_ref, mq_ref):
    h = (_normalize(x_ref[...]) * g_ref[...]).astype(BF16)
    c = 0
    q_ref[...] = _dot(h, w_ref[:, c:c + GLA_QK]) * (GLA_DK ** -0.5)
    c += GLA_QK
    k_ref[...] = _dot(h, w_ref[:, c:c + GLA_QK])
    c += GLA_QK
    v_ref[...] = _dot(h, w_ref[:, c:c + GLA_V]).astype(BF16)
    c += GLA_V
    r_ref[...] = _dot(h, w_ref[:, c:c + GLA_V])
    c += GLA_V
    mq_ref[...] = _dot(h, w_ref[:, c:c + MEM_WIDTH]).astype(BF16)
    c += MEM_WIDTH
    alr = _dot(h, w_ref[:, c:c + LANE]).astype(BF16)
    z = _dot(alr, wa_ref[...]) + ba_ref[...]
    log_sig = jnp.minimum(z, 0.0) - jnp.log1p(jnp.exp(-jnp.abs(z)))
    gl_ref[...] = log_sig * (1.0 / GLA_TAU)


def _inproj_a(x, g, w, wa, ba, tm):
    n, d = x.shape
    row = lambda i: (i, 0)
    outs = [
        jax.ShapeDtypeStruct((n, GLA_QK), F32), jax.ShapeDtypeStruct((n, GLA_QK), F32),
        jax.ShapeDtypeStruct((n, GLA_QK), F32), jax.ShapeDtypeStruct((n, GLA_V), BF16),
        jax.ShapeDtypeStruct((n, GLA_V), F32), jax.ShapeDtypeStruct((n, MEM_WIDTH), BF16),
    ]
    return pl.pallas_call(
        _inproj_a_kernel,
        out_shape=outs,
        grid=(n // tm,),
        in_specs=[pl.BlockSpec((tm, d), row), _const_spec((1, d)), _const_spec(w.shape),
                  _const_spec(wa.shape), _const_spec(ba.shape)],
        out_specs=[pl.BlockSpec((tm, s.shape[1]), row) for s in outs],
        compiler_params=_params(("parallel",)),
        name="inproj_a",
    )(x, g, w, wa, ba)


def _gla_constants():
    c = GLA_CHUNK
    w = np.zeros((8 * c, c), np.float32)
    masks = np.zeros((GLA_LEVELS + 1, c, c), np.float32)
    masks[0] = np.eye(c)
    for l in range(1, GLA_LEVELS + 1):
        blk, half = 2 ** l, 2 ** (l - 1)
        for i in range(c):
            mid = (i // blk) * blk + half - 1
            if i % blk >= half:
                w[(l - 1) * c + i, mid + 1:i + 1] = 1.0
            else:
                w[(l - 1) * c + i, i + 1:mid + 1] = 1.0
        for t in range(c):
            for s in range(c):
                if t // blk == s // blk and t % blk >= half and s % blk < half:
                    masks[l, t, s] = 1.0
    for i in range(c):
        w[6 * c + i, :i + 1] = 1.0
        w[7 * c + i, i + 1:] = 1.0
    return w, np.tile(masks, (1, 1, GLA_HEADS))


def _gla_kernel(q_ref, k_ref, gl_ref, v_ref, r_ref, gh_ref, wcat_ref, msk_ref, o_ref, st_ref, *, n_chunks):
    c = GLA_CHUNK

    @pl.when(pl.program_id(1) == 0)
    def _():
        st_ref[...] = jnp.zeros_like(st_ref)

    lane_qk = lax.broadcasted_iota(jnp.int32, (1, GLA_QK), 1) // GLA_DK
    lane_v = lax.broadcasted_iota(jnp.int32, (1, GLA_V), 1) // GLA_DV
    ones = jnp.ones((c, LANE), BF16)
    wcat = wcat_ref[...]

    def chunk(ci, carry):
        rows = pl.ds(pl.multiple_of(ci * c, c), c)
        g = gl_ref[rows, :]
        g_hi = g.astype(BF16)
        g_lo = (g - g_hi.astype(F32)).astype(BF16)
        e = jnp.exp(_dot(wcat, g_hi) + _dot(wcat, g_lo))
        dec = jnp.exp(_dot_tn(g_hi, ones) + _dot_tn(g_lo, ones))
        q = q_ref[rows, :]
        k = k_ref[rows, :]
        v = v_ref[rows, :]

        attn = jnp.zeros((c, GLA_HEADS * c), F32)
        for l in range(GLA_LEVELS + 1):
            if l == 0:
                ql, kl = q, k
            else:
                el = e[(l - 1) * c:l * c]
                ql, kl = q * el, k * el
            kstack = jnp.concatenate(
                [jnp.where(lane_qk == h, kl, 0.0).astype(BF16) for h in range(GLA_HEADS)], axis=0)
            attn = attn + _dot_nt(ql.astype(BF16), kstack) * msk_ref[l]
        vstack = jnp.concatenate(
            [jnp.where(lane_v == h, v, jnp.zeros_like(v)) for h in range(GLA_HEADS)], axis=0)
        o_intra = _dot(attn.astype(BF16), vstack)

        st = st_ref[...]
        qi = q * e[6 * c:7 * c]
        qstack = jnp.concatenate(
            [jnp.where(lane_qk == h, qi, 0.0).astype(BF16) for h in range(GLA_HEADS)], axis=0)
        o_inter = _dot(qstack, st.astype(BF16))
        kk = (k * e[7 * c:8 * c]).astype(BF16)
        upd = _dot_tn(kk, v)
        new_rows = []
        for h in range(GLA_HEADS):
            rs = slice(h * GLA_DK, (h + 1) * GLA_DK)
            new_rows.append(dec[rs] * st[rs] + upd[rs, h * GLA_DV:(h + 1) * GLA_DV])
        st_ref[...] = jnp.concatenate(new_rows, axis=0)

        for h in range(GLA_HEADS):
            cs = slice(h * GLA_DV, (h + 1) * GLA_DV)
            o = o_intra[:, cs] + o_inter[h * c:(h + 1) * c]
            on = _normalize(o) * gh_ref[...]
            r = r_ref[rows, cs]
            o_ref[rows, cs] = (on * (r * _sigmoid(r))).astype(BF16)
        return carry

    lax.fori_loop(0, n_chunks, chunk, 0)


def _gla(q, k, gl, v, r, g_head, batch, ct):
    n = q.shape[0]
    t = n // batch
    nt = t // ct
    wcat, masks = _gla_constants()
    row = lambda b, i: (b * nt + i, 0)
    return pl.pallas_call(
        functools.partial(_gla_kernel, n_chunks=ct // GLA_CHUNK),
        out_shape=jax.ShapeDtypeStruct((n, GLA_V), BF16),
        grid=(batch, nt),
        in_specs=[pl.BlockSpec((ct, GLA_QK), row), pl.BlockSpec((ct, GLA_QK), row),
                  pl.BlockSpec((ct, GLA_QK), row), pl.BlockSpec((ct, GLA_V), row),
                  pl.BlockSpec((ct, GLA_V), row), _const_spec((1, GLA_DV)),
                  _const_spec(wcat.shape), _const_spec(masks.shape)],
        out_specs=pl.BlockSpec((ct, GLA_V), row),
        scratch_shapes=[pltpu.VMEM((GLA_QK, GLA_DV), F32)],
        compiler_params=_params(("parallel", "arbitrary")),
        name="gla",
    )(q, k, gl, v, r, g_head, jnp.asarray(wcat, BF16), jnp.asarray(masks, F32))


def _attn_out_kernel(o_ref, mq_ref, mk_ref, mv_ref, w_ref, x_ref, out_ref):
    parts = [o_ref[...]]
    for h in range(MEM_HEADS):
        cs = slice(h * MEM_DH, (h + 1) * MEM_DH)
        s = _dot_nt(mq_ref[:, cs], mk_ref[0, :, cs]) * (MEM_DH ** -0.5)
        e = jnp.exp(s - jnp.max(s, axis=-1, keepdims=True))
        p = e / jnp.sum(e, axis=-1, keepdims=True)
        parts.append(_dot(p.astype(BF16), mv_ref[0, :, cs]).astype(BF16))
    cat = jnp.concatenate(parts, axis=-1)
    out_ref[...] = x_ref[...] + _dot(cat, w_ref[...])


def _attn_out(o, mq, mkv, w_out, x, batch, tm):
    n, d = x.shape
    t = n // batch
    nt = t // tm
    m = mkv.shape[1]
    row = lambda b, i: (b * nt + i, 0)
    return pl.pallas_call(
        _attn_out_kernel,
        out_shape=jax.ShapeDtypeStruct((n, d), F32),
        grid=(batch, nt),
        in_specs=[pl.BlockSpec((tm, o.shape[1]), row), pl.BlockSpec((tm, MEM_WIDTH), row),
                  pl.BlockSpec((1, m, MEM_WIDTH), lambda b, i: (b, 0, 0)),
                  pl.BlockSpec((1, m, MEM_WIDTH), lambda b, i: (b, 0, 1)),
                  _const_spec(w_out.shape), pl.BlockSpec((tm, d), row)],
        out_specs=pl.BlockSpec((tm, d), row),
        compiler_params=_params(("parallel", "parallel")),
        name="attn_out",
    )(o, mq, mkv, mkv, w_out, x)


def _ffn_kernel(x_ref, g_ref, wup_ref, cw_ref, cb_ref, wd_ref, gf_ref, out_ref, act_ref, tail_ref, *,
                final_norm):
    @pl.when(pl.program_id(1) == 0)
    def _():
        tail_ref[...] = jnp.zeros_like(tail_ref)

    x = x_ref[...]
    tm = x.shape[0]
    h = (_normalize(x) * g_ref[...]).astype(BF16)
    rid = lax.broadcasted_iota(jnp.int32, (8, FFN_CHUNK), 0)
    for j in range(FFN_DIM // FFN_CHUNK):
        cs = slice(j * FFN_CHUNK, (j + 1) * FFN_CHUNK)
        a = _dot(h, wup_ref[:, cs])
        b = _dot(h, wup_ref[:, FFN_DIM + j * FFN_CHUNK:FFN_DIM + (j + 1) * FFN_CHUNK])
        tail = tail_ref[j]
        r1 = pltpu.roll(a, 1, 0)
        r2 = pltpu.roll(a, 2, 0)
        top1 = jnp.where(rid == 0, tail[7:8], r1[:8])
        top2 = jnp.where(rid == 0, tail[6:7], jnp.where(rid == 1, tail[7:8], r2[:8]))
        a1 = jnp.concatenate([top1, r1[8:]], axis=0)
        a2 = jnp.concatenate([top2, r2[8:]], axis=0)
        tail_ref[j] = a[tm - 8:]
        ac = a2 * cw_ref[0:1, cs] + a1 * cw_ref[1:2, cs] + a * cw_ref[2:3, cs] + cb_ref[:, cs]
        act_ref[:, cs] = (ac * _sigmoid(ac) * b).astype(BF16)
    y = x + _dot(act_ref[...], wd_ref[...])
    if final_norm:
        y = _normalize(y) * gf_ref[...]
    out_ref[...] = y


def _ffn(x, g, w_up, conv_w, conv_b, w_down, g_final, batch, tm, final_norm):
    n, d = x.shape
    nt = n // batch // tm
    row = lambda b, i: (b * nt + i, 0)
    once = pl.Buffered(1)
    return pl.pallas_call(
        functools.partial(_ffn_kernel, final_norm=final_norm),
        out_shape=jax.ShapeDtypeStruct((n, d), F32),
        grid=(batch, nt),
        in_specs=[pl.BlockSpec((tm, d), row), _const_spec((1, d)),
                  pl.BlockSpec(w_up.shape, lambda b, i: (0, 0), pipeline_mode=once),
                  _const_spec(conv_w.shape), _const_spec(conv_b.shape),
                  pl.BlockSpec(w_down.shape, lambda b, i: (0, 0), pipeline_mode=once),
                  _const_spec((1, d))],
        out_specs=pl.BlockSpec((tm, d), row),
        scratch_shapes=[pltpu.VMEM((tm, FFN_DIM), BF16),
                        pltpu.VMEM((FFN_DIM // FFN_CHUNK, 8, FFN_CHUNK), F32)],
        compiler_params=_params(("parallel", "arbitrary")),
        name="conv_ffn",
    )(x, g, w_up, conv_w, conv_b, w_down, g_final)


def _proj_b_kernel(x_ref, gkv_ref, gmix_ref, wkv_ref, wqt_ref, wgt_ref, wmq_ref,
                   cmp_ref, kvs_ref, qt_ref, gt_ref, mq_ref):
    xn = _normalize(x_ref[...])
    hkv = (xn * gkv_ref[...]).astype(BF16)
    h1 = (xn * gmix_ref[...]).astype(BF16)
    cmp_ref[0] = _dot(hkv, wkv_ref[:, 0:KV_WIDTH]).astype(BF16)
    cmp_ref[1] = _dot(hkv, wkv_ref[:, KV_WIDTH:2 * KV_WIDTH]).astype(BF16)
    kvs_ref[...] = _dot(hkv, wkv_ref[:, 2 * KV_WIDTH:]).astype(BF16)
    qt_ref[0] = (_dot_nt(wqt_ref[...], h1) * (NSA_DH ** -0.5)).astype(BF16)
    gt_ref[0] = _sigmoid(_dot_nt(wgt_ref[...], h1))
    mq_ref[...] = _dot(h1, wmq_ref[...]).astype(BF16)


def _proj_b(x, g_kv, g_mix, w_kv, wqt, wgt, wmq, batch, tm):
    n, d = x.shape
    t = n // batch
    nt = t // tm
    row = lambda b, i: (b * nt + i, 0)
    outs = [
        jax.ShapeDtypeStruct((2, n, KV_WIDTH), BF16),
        jax.ShapeDtypeStruct((n, 4 * KV_WIDTH), BF16),
        jax.ShapeDtypeStruct((batch, NSA_WIDTH, t), BF16),
        jax.ShapeDtypeStruct((batch, wgt.shape[0], t), F32),
        jax.ShapeDtypeStruct((n, MEM_WIDTH), BF16),
    ]
    return pl.pallas_call(
        _proj_b_kernel,
        out_shape=outs,
        grid=(batch, nt),
        in_specs=[pl.BlockSpec((tm, d), row), _const_spec((1, d)), _const_spec((1, d)),
                  _const_spec(w_kv.shape), _const_spec(wqt.shape), _const_spec(wgt.shape),
                  _const_spec(wmq.shape)],
        out_specs=[pl.BlockSpec((2, tm, KV_WIDTH), lambda b, i: (0, b * nt + i, 0)),
                   pl.BlockSpec((tm, 4 * KV_WIDTH), row),
                   pl.BlockSpec((1, NSA_WIDTH, tm), lambda b, i: (b, 0, i)),
                   pl.BlockSpec((1, wgt.shape[0], tm), lambda b, i: (b, 0, i)),
                   pl.BlockSpec((tm, MEM_WIDTH), row)],
        compiler_params=_params(("parallel", "parallel")),
        name="proj_b",
    )(x, g_kv, g_mix, w_kv, wqt, wgt, wmq)


def _compress_kernel(a_ref, w1_ref, pe_ref, w2_ref, o_ref):
    a = a_ref[0, 0]
    w1 = w1_ref[0]
    u = _dot(a, w1)
    cpe = _dot(pe_ref[0], w1)
    nrow = a.shape[0]
    hid = []
    for g in range(NSA_GROUPS):
        c0 = slice((2 * g) * CMP_HIDDEN, (2 * g + 1) * CMP_HIDDEN)
        c1 = slice((2 * g + 1) * CMP_HIDDEN, (2 * g + 2) * CMP_HIDDEN)
        nxt = pltpu.roll(u[:, c1], nrow - 1, 0)
        hid.append(u[:, c0] + nxt + cpe[0:1, c0] + cpe[8:9, c1])
    hcat = jax.nn.gelu(jnp.concatenate(hid, axis=-1), approximate=True).astype(BF16)
    o_ref[0, 0] = _dot(hcat, w2_ref[0]).astype(BF16)


def _compress(a, w1, pe, w2):
    _, batch, nsub, width = a.shape
    return pl.pallas_call(
        _compress_kernel,
        out_shape=jax.ShapeDtypeStruct((2, batch, nsub, KV_WIDTH), BF16),
        grid=(2, batch),
        in_specs=[pl.BlockSpec((1, 1, nsub, width), lambda s, b: (s, b, 0, 0)),
                  pl.BlockSpec((1,) + w1.shape[1:], lambda s, b: (s, 0, 0)),
                  pl.BlockSpec((1,) + pe.shape[1:], lambda s, b: (s, 0, 0)),
                  pl.BlockSpec((1,) + w2.shape[1:], lambda s, b: (s, 0, 0))],
        out_specs=pl.BlockSpec((1, 1, nsub, KV_WIDTH), lambda s, b: (s, b, 0, 0)),
        compiler_params=_params(("parallel", "parallel")),
        name="compress",
    )(a, w1, pe, w2)


def _alibi_tiles(ncmp_pad):
    slopes = 2.0 ** (-8.0 * np.arange(1, NSA_HEADS + 1) / NSA_HEADS)
    tl = np.tile(np.arange(Q_BLOCK), NSA_HPG)[None, :]
    cmp_end = (np.arange(ncmp_pad) * CMP_STRIDE + CMP_BLOCK - 1)[:, None]
    key = np.arange(WIN_KEYS)[:, None]
    c_cmp = np.zeros((NSA_GROUPS, ncmp_pad, NSA_HPG * Q_BLOCK), np.float32)
    c_key = np.zeros((NSA_GROUPS, WIN_KEYS, NSA_HPG * Q_BLOCK), np.float32)
    srow = np.zeros((NSA_GROUPS, 1, NSA_HPG * Q_BLOCK), np.float32)
    for g in range(NSA_GROUPS):
        s = np.repeat(slopes[g * NSA_HPG:(g + 1) * NSA_HPG], Q_BLOCK)[None, :]
        c_cmp[g] = -s * (tl - cmp_end)
        c_key[g] = -s * (tl - key)
        srow[g] = s
    return c_cmp, c_key, srow


def _group_queries(qt_ref, g):
    zeros = jnp.zeros((NSA_DH, Q_BLOCK), BF16)
    cols = []
    for hh in range(NSA_HPG):
        h = g * NSA_HPG + hh
        piece = qt_ref[0, h * NSA_DH:(h + 1) * NSA_DH, :]
        cols.append(jnp.concatenate([piece, zeros] if g == 0 else [zeros, piece], axis=0))
    return jnp.concatenate(cols, axis=1)


def _nsa_a_kernel(qt_ref, ckv_ref, ccmp_ref, srow_ref, ovt_ref, oc_ref, sel_ref, cnt_ref, vals_ref):
    qb = pl.program_id(1)
    base = (qb * Q_BLOCK).astype(F32)
    kc = ckv_ref[0, 0]
    vc = ckv_ref[1, 0]
    nblk = sel_ref.shape[3]
    jidx = lax.broadcasted_iota(jnp.int32, (nblk, Q_BLOCK), 0)
    tl = lax.broadcasted_iota(jnp.int32, (nblk, Q_BLOCK), 1)
    cur = 2 * qb + jnp.where(tl >= SEL_BLOCK, 1, 0)
    valid_blk = jidx <= cur
    forced = (jidx == 0) | (jidx == cur) | (jidx == cur - 1)
    for g in range(NSA_GROUPS):
        qtg = _group_queries(qt_ref, g)
        bias = ccmp_ref[g] - srow_ref[g] * base
        valid = bias <= 0.0
        s = jnp.where(valid, _dot(kc, qtg) + bias, NEG_INF)
        m = jnp.max(s, axis=0, keepdims=True)
        m = jnp.where(m == NEG_INF, 0.0, m)
        e = jnp.exp(s - m)
        p = e * (1.0 / jnp.maximum(jnp.sum(e, axis=0, keepdims=True), 1e-30))
        oc = _dot_tn(vc, p.astype(BF16))
        for hh in range(NSA_HPG):
            h = g * NSA_HPG + hh
            oc_ref[0, h * NSA_DH:(h + 1) * NSA_DH, :] = \
                oc[g * NSA_DH:(g + 1) * NSA_DH, hh * Q_BLOCK:(hh + 1) * Q_BLOCK]
        ps = p[:, 0:Q_BLOCK]
        for hh in range(1, NSA_HPG):
            ps = ps + p[:, hh * Q_BLOCK:(hh + 1) * Q_BLOCK]
        p1 = ps.astype(BF16)
        r1 = ps - p1.astype(F32)
        p2 = r1.astype(BF16)
        p3 = (r1 - p2.astype(F32)).astype(BF16)
        ovt = ovt_ref[...]
        imp = _dot(ovt, p1) + _dot(ovt, p2) + _dot(ovt, p3)
        vals = jnp.where(forced, jnp.inf, jnp.where(valid_blk, imp, NEG_INF))
        vals_ref[...] = vals

        def body(i, cnt):
            row = vals_ref[pl.ds(i, 1), :]
            ahead = (row > vals) | ((row == vals) & (jidx > i))
            return cnt + jnp.where(ahead, 1.0, 0.0)

        rank = lax.fori_loop(0, 2 * qb + 2, body, jnp.zeros((nblk, Q_BLOCK), F32))
        sel = jnp.where((rank < float(SEL_TOPN)) & valid_blk, 1.0, 0.0)
        sel_ref[0, g, 0] = sel
        cnt_ref[0, g, 0] = _dot_nt(jnp.ones((8, Q_BLOCK), BF16), sel.astype(BF16))


def _nsa_a(qt, ckv, c_cmp, srow, ovt):
    batch, _, t = qt.shape
    nqb = t // Q_BLOCK
    ns = t // SEL_BLOCK
    ncmp = ckv.shape[2]
    outs = [jax.ShapeDtypeStruct((batch, NSA_WIDTH, t), F32),
            jax.ShapeDtypeStruct((batch, NSA_GROUPS, nqb, ns, Q_BLOCK), F32),
            jax.ShapeDtypeStruct((batch, NSA_GROUPS, nqb, 8, ns), F32)]
    return pl.pallas_call(
        _nsa_a_kernel,
        out_shape=outs,
        grid=(batch, nqb),
        in_specs=[pl.BlockSpec((1, NSA_WIDTH, Q_BLOCK), lambda b, i: (b, 0, i)),
                  pl.BlockSpec((2, 1, ncmp, KV_WIDTH), lambda b, i: (0, b, 0, 0)),
                  _const_spec(c_cmp.shape), _const_spec(srow.shape), _const_spec(ovt.shape)],
        out_specs=[pl.BlockSpec((1, NSA_WIDTH, Q_BLOCK), lambda b, i: (b, 0, i)),
                   pl.BlockSpec((1, NSA_GROUPS, 1, ns, Q_BLOCK), lambda b, i: (b, 0, i, 0, 0)),
                   pl.BlockSpec((1, NSA_GROUPS, 1, 8, ns), lambda b, i: (b, 0, i, 0, 0))],
        scratch_shapes=[pltpu.VMEM((ns, Q_BLOCK), F32)],
        compiler_params=_params(("parallel", "parallel")),
        name="nsa_select",
    )(qt, ckv, c_cmp, srow, ovt)


def _nsa_b_kernel(idx_ref, qt_ref, ks_ref, vs_ref, kw_ref, vw_ref, sel_ref, oc_ref, gt_ref, ckey_ref, srow_ref,
                  o_ref, *, idx_words):
    b = pl.program_id(0)
    qb = pl.program_id(1)
    base = qb * Q_BLOCK
    width = NSA_HPG * Q_BLOCK
    heads_out = [None] * NSA_HEADS
    for g in range(NSA_GROUPS):
        qtg = _group_queries(qt_ref, g)
        srow = srow_ref[g]
        rows_g = slice(g * NSA_DH, (g + 1) * NSA_DH)

        c_blk = ckey_ref[g, 0:SEL_BLOCK, :]

        entry = ((b * NSA_GROUPS + g) * pl.num_programs(1) + qb) * idx_words

        def body(i, carry):
            m, l, acc = carry
            word = idx_ref[entry + lax.shift_right_logical(i, 2)]
            j = lax.shift_right_logical(word, (i & 3) * 8) & 0xFF
            k0 = pl.multiple_of(j * SEL_BLOCK, SEL_BLOCK)
            kt = ks_ref[pl.ds(k0, SEL_BLOCK), :]
            vt = vs_ref[pl.ds(k0, SEL_BLOCK), :]
            mrow = sel_ref[0, g, 0, pl.ds(j, 1), :]
            mrow = jnp.concatenate([mrow] * NSA_HPG, axis=1)
            bias = c_blk - srow * (base - k0).astype(F32)
            valid = (mrow > 0.0) & (bias <= 0.0)
            s = jnp.where(valid, _dot(kt, qtg) + bias, NEG_INF)
            m_new = jnp.maximum(m, jnp.max(s, axis=0, keepdims=True))
            m_safe = jnp.where(m_new == NEG_INF, 0.0, m_new)
            alpha = jnp.where(m == NEG_INF, 0.0, jnp.exp(m - m_safe))
            p = jnp.exp(s - m_safe)
            l = alpha * l + jnp.sum(p, axis=0, keepdims=True)
            acc = alpha * acc + _dot_tn(vt, p.astype(BF16))[rows_g]
            return m_new, l, acc

        n_act = idx_ref[entry + idx_words - 1]
        init = (jnp.full((1, width), NEG_INF, F32), jnp.zeros((1, width), F32),
                jnp.zeros((NSA_DH, width), F32))
        _, l_s, acc_s = lax.fori_loop(0, n_act, body, init)
        o_s = acc_s * (1.0 / jnp.maximum(l_s, 1e-30))

        start = jnp.maximum(base - WINDOW, 0)
        start = pl.multiple_of(start, Q_BLOCK)
        kw = kw_ref[pl.ds(start, WIN_KEYS), :]
        vw = vw_ref[pl.ds(start, WIN_KEYS), :]
        bias = ckey_ref[g] - srow * (base - start).astype(F32)
        valid = (bias <= 0.0) & (bias > srow * (-float(WINDOW)))
        s = jnp.where(valid, _dot(kw, qtg) + bias, NEG_INF)
        m = jnp.max(s, axis=0, keepdims=True)
        e = jnp.exp(s - m)
        p = e * (1.0 / jnp.sum(e, axis=0, keepdims=True))
        o_w = _dot_tn(vw, p.astype(BF16))[rows_g]

        for hh in range(NSA_HPG):
            h = g * NSA_HPG + hh
            cs = slice(hh * Q_BLOCK, (hh + 1) * Q_BLOCK)
            gates = gt_ref[0, 3 * h:3 * h + 3, :]
            heads_out[h] = (gates[0:1] * oc_ref[0, h * NSA_DH:(h + 1) * NSA_DH, :]
                            + gates[1:2] * o_s[:, cs] + gates[2:3] * o_w[:, cs])
    o_ref[...] = jnp.concatenate(heads_out, axis=0).T.astype(BF16)


def _nsa_b(idx, qt, kvs, sel, oc, gt, c_key, srow):
    batch, _, t = qt.shape
    nqb = t // Q_BLOCK
    ns = t // SEL_BLOCK
    kv_spec = lambda c: pl.BlockSpec((t, KV_WIDTH), lambda b, i, idx_ref: (b, c))
    grid_spec = pltpu.PrefetchScalarGridSpec(
        num_scalar_prefetch=1,
        grid=(batch, nqb),
        in_specs=[pl.BlockSpec((1, NSA_WIDTH, Q_BLOCK), lambda b, i, r: (b, 0, i)),
                  kv_spec(0), kv_spec(1), kv_spec(2), kv_spec(3),
                  pl.BlockSpec((1, NSA_GROUPS, 1, ns, Q_BLOCK), lambda b, i, r: (b, 0, i, 0, 0)),
                  pl.BlockSpec((1, NSA_WIDTH, Q_BLOCK), lambda b, i, r: (b, 0, i)),
                  pl.BlockSpec((1, gt.shape[1], Q_BLOCK), lambda b, i, r: (b, 0, i)),
                  pl.BlockSpec(c_key.shape, lambda b, i, r: (0, 0, 0)),
                  pl.BlockSpec(srow.shape, lambda b, i, r: (0, 0, 0))],
        out_specs=pl.BlockSpec((Q_BLOCK, NSA_WIDTH), lambda b, i, r: (b * nqb + i, 0)),
    )
    return pl.pallas_call(
        functools.partial(_nsa_b_kernel, idx_words=ns // 4 + 1),
        out_shape=jax.ShapeDtypeStruct((batch * t, NSA_WIDTH), BF16),
        grid_spec=grid_spec,
        compiler_params=_params(("parallel", "parallel")),
        name="nsa_attend",
    )(idx, qt, kvs, kvs, kvs, kvs, sel, oc, gt, c_key, srow)


def _pad_cols(w, width):
    return jnp.pad(w, ((0, 0), (0, width - w.shape[1])))


def _compress_weights(w1, pe, w2):
    nsub = CMP_STRIDE
    w1r = w1.reshape(2, nsub, NSA_DH, CMP_HIDDEN)
    big = jnp.zeros((nsub, NSA_GROUPS, NSA_DH, NSA_GROUPS, 2, CMP_HIDDEN), F32)
    for g in range(NSA_GROUPS):
        for half in range(2):
            big = big.at[:, g, :, g, half, :].set(w1r[half])
    big = big.reshape(nsub * KV_WIDTH, NSA_GROUPS * 2 * CMP_HIDDEN)
    per = pe.reshape(2, nsub, 1, NSA_DH)
    pe2 = jnp.broadcast_to(per, (2, nsub, NSA_GROUPS, NSA_DH)).reshape(2, 1, nsub * KV_WIDTH)
    pe2 = jnp.broadcast_to(pe2, (2, 8, nsub * KV_WIDTH)).reshape(16, nsub * KV_WIDTH)
    w2bd = jnp.zeros((NSA_GROUPS, CMP_HIDDEN, NSA_GROUPS, NSA_DH), F32)
    for g in range(NSA_GROUPS):
        w2bd = w2bd.at[g, :, g, :].set(w2)
    return big.astype(BF16), pe2.astype(BF16), w2bd.reshape(NSA_GROUPS * CMP_HIDDEN, KV_WIDTH).astype(BF16)


def _selection_overlap_t(ncmp_pad, ns):
    cs = np.arange(ncmp_pad) * CMP_STRIDE
    ss = np.arange(ns) * SEL_BLOCK
    ov = np.minimum(cs[:, None] + CMP_BLOCK, ss[None, :] + SEL_BLOCK) - np.maximum(cs[:, None], ss[None, :])
    return (np.clip(ov, 0, None).astype(np.float32) / CMP_BLOCK).T


def _active_blocks(cnt):
    flags = cnt[:, :, :, 0, :] > 0.0
    ns = flags.shape[-1]
    order = jnp.argsort(jnp.where(flags, 0, 1).astype(jnp.int32), axis=-1, stable=True).astype(jnp.int32)
    packed = jnp.sum(order.reshape(order.shape[:-1] + (ns // 4, 4)) << (8 * jnp.arange(4, dtype=jnp.int32)), axis=-1)
    n_act = jnp.sum(flags, axis=-1, dtype=jnp.int32)
    return jnp.concatenate([packed, n_act[..., None]], axis=-1).reshape(-1)


def kernel(x, mem, g_mix, g_ffn, g_mem, w_mem_kv, w_up, conv_w, conv_b, w_down,
           a_w_in, a_w_alpha, a_b_alpha, a_g_head, a_w_out,
           g_kv, w_kv, pe_k, pe_v, w_ck1, w_ck2, w_cv1, w_cv2,
           b_w_in, b_w_out, g_final):
    batch, t, d = x.shape
    n = batch * t
    m = mem.shape[1]
    tm = min(512, t)
    xf = x.reshape(n, d)
    memf = mem.reshape(batch * m, d)
    row = lambda v: v.reshape(1, -1)

    mkv0 = _rms_proj(memf, row(g_mem[0]), w_mem_kv[0].astype(BF16), m, BF16).reshape(batch, m, 2 * MEM_WIDTH)
    wa = a_w_in[0]
    c_alr = 2 * GLA_QK + 2 * GLA_V
    w_a = jnp.concatenate([wa[:, :c_alr], wa[:, c_alr + GLA_RANK:], _pad_cols(wa[:, c_alr:c_alr + GLA_RANK], LANE)],
                          axis=1).astype(BF16)
    w_alpha = jnp.pad(a_w_alpha[0], ((0, LANE - GLA_RANK), (0, 0))).astype(BF16)
    q, k, gl, v, r, mq = _inproj_a(xf, row(g_mix[0]), w_a, w_alpha, row(a_b_alpha[0]), tm)
    o = _gla(q, k, gl, v, r, row(a_g_head[0]), batch, tm)
    xf = _attn_out(o, mq, mkv0, a_w_out[0].astype(BF16), xf, batch, tm)
    xf = _ffn(xf, row(g_ffn[0]), w_up[0].astype(BF16), conv_w[0], row(conv_b[0]), w_down[0].astype(BF16),
              row(g_final), batch, tm, final_norm=False)

    wb = b_w_in[0]
    wqt = wb[:, :NSA_WIDTH].T.astype(BF16)
    n_gate = 3 * NSA_HEADS
    wgt = jnp.pad(wb[:, NSA_WIDTH:NSA_WIDTH + n_gate].T, ((0, 32 - n_gate), (0, 0))).astype(BF16)
    wmq = wb[:, NSA_WIDTH + n_gate:].astype(BF16)
    ckv_in, kvs, qt, gt, mq1 = _proj_b(xf, row(g_kv), row(g_mix[1]), w_kv.astype(BF16), wqt, wgt, wmq, batch, tm)
    nsub = t // CMP_STRIDE
    w1k, pek, w2k = _compress_weights(w_ck1, pe_k, w_ck2)
    w1v, pev, w2v = _compress_weights(w_cv1, pe_v, w_cv2)
    ckv = _compress(ckv_in.reshape(2, batch, nsub, CMP_STRIDE * KV_WIDTH),
                    jnp.stack([w1k, w1v]), jnp.stack([pek, pev]), jnp.stack([w2k, w2v]))

    ns = t // SEL_BLOCK
    c_cmp, c_key, srow = _alibi_tiles(nsub)
    ovt = jnp.asarray(_selection_overlap_t(nsub, ns), BF16)
    oc, sel, cnt = _nsa_a(qt, ckv, jnp.asarray(c_cmp), jnp.asarray(srow), ovt)
    idx = _active_blocks(cnt)
    o1 = _nsa_b(idx, qt, kvs, sel, oc, gt, jnp.asarray(c_key), jnp.asarray(srow))
    mkv1 = _rms_proj(memf, row(g_mem[1]), w_mem_kv[1].astype(BF16), m, BF16).reshape(batch, m, 2 * MEM_WIDTH)
    xf = _attn_out(o1, mq1, mkv1, b_w_out[0].astype(BF16), xf, batch, tm)
    xf = _ffn(xf, row(g_ffn[1]), w_up[1].astype(BF16), conv_w[1], row(conv_b[1]), w_down[1].astype(BF16),
              row(g_final), batch, tm, final_norm=True)
    return xf.reshape(batch, t, d)
```

```python
import functools

import numpy as np
import jax
import jax.numpy as jnp
from jax import lax
from jax.experimental import pallas as pl
from jax.experimental.pallas import tpu as pltpu

F32 = jnp.float32
BF16 = jnp.bfloat16
EPS = 1e-6
NEG_INF = float("-inf")

V7X_VMEM_BYTES = 64 * 1024 * 1024
VMEM_LIMIT = V7X_VMEM_BYTES - 8 * 1024 * 1024

D_MODEL = 1024
MEM_HEADS = 4
MEM_DH = 128
MEM_WIDTH = MEM_HEADS * MEM_DH
GLA_HEADS = 4
GLA_DK = 64
GLA_DV = 128
GLA_QK = GLA_HEADS * GLA_DK
GLA_V = GLA_HEADS * GLA_DV
GLA_RANK = 16
GLA_TAU = 16.0
GLA_CHUNK = 64
GLA_LEVELS = 6
NSA_HEADS = 8
NSA_GROUPS = 2
NSA_HPG = NSA_HEADS // NSA_GROUPS
NSA_DH = 64
NSA_WIDTH = NSA_HEADS * NSA_DH
KV_WIDTH = NSA_GROUPS * NSA_DH
CMP_BLOCK = 32
CMP_STRIDE = 16
CMP_HIDDEN = 256
SEL_BLOCK = 64
SEL_TOPN = 16
WINDOW = 512
Q_BLOCK = 128
WIN_KEYS = WINDOW + Q_BLOCK
WIN_CHUNK = WIN_KEYS // 2
CMP_CHUNK = 128
SLC_CHUNK = 4
RANK_FIRST_TILE = SEL_TOPN * SEL_BLOCK // Q_BLOCK
FFN_DIM = 2816
FFN_CHUNK = 256
LANE = 128


def _dot(a, b):
    return jnp.dot(a, b, preferred_element_type=F32)


def _dot_nt(a, b):
    return lax.dot_general(a, b, (((1,), (1,)), ((), ())), preferred_element_type=F32)


def _dot_tn(a, b):
    return lax.dot_general(a, b, (((0,), (0,)), ((), ())), preferred_element_type=F32)


def _params(sem):
    return pltpu.CompilerParams(dimension_semantics=sem, vmem_limit_bytes=VMEM_LIMIT)


def _const_spec(shape):
    n = len(shape)
    return pl.BlockSpec(shape, lambda *_: (0,) * n)


def _normalize(x):
    return x * lax.rsqrt(jnp.mean(x * x, axis=-1, keepdims=True) + EPS)


def _sigmoid(x):
    return 1.0 / (1.0 + jnp.exp(-x))


def _rms_proj_kernel(x_ref, g_ref, w_ref, o_ref):
    h = (_normalize(x_ref[...]) * g_ref[...]).astype(BF16)
    o_ref[...] = _dot(h, w_ref[...]).astype(o_ref.dtype)


def _rms_proj(x, g, w, tm, out_dtype):
    n, d = x.shape
    p = w.shape[1]
    return pl.pallas_call(
        _rms_proj_kernel,
        out_shape=jax.ShapeDtypeStruct((n, p), out_dtype),
        grid=(n // tm,),
        in_specs=[pl.BlockSpec((tm, d), lambda i: (i, 0)), _const_spec((1, d)), _const_spec((d, p))],
        out_specs=pl.BlockSpec((tm, p), lambda i: (i, 0)),
        compiler_params=_params(("parallel",)),
        name="rms_proj",
    )(x, g, w)


def _inproj_a_kernel(x_ref, g_ref, w_ref, wa_ref, ba_ref, q_ref, k_ref, gl_ref, v_ref, r_ref, mq_ref):
    h = (_normalize(x_ref[...]) * g_ref[...]).astype(BF16)
    c = 0
    q_ref[...] = _dot(h, w_ref[:, c:c + GLA_QK]) * (GLA_DK ** -0.5)
    c += GLA_QK
    k_ref[...] = _dot(h, w_ref[:, c:c + GLA_QK])
    c += GLA_QK
    v_ref[...] = _dot(h, w_ref[:, c:c + GLA_V]).astype(BF16)
    c += GLA_V
    r_ref[...] = _dot(h, w_ref[:, c:c + GLA_V])
    c += GLA_V
    mq_ref[...] = _dot(h, w_ref[:, c:c + MEM_WIDTH]).astype(BF16)
    c += MEM_WIDTH
    alr = _dot(h, w_ref[:, c:c + LANE]).astype(BF16)
    z = _dot(alr, wa_ref[...]) + ba_ref[...]
    log_sig = jnp.minimum(z, 0.0) - jnp.log1p(jnp.exp(-jnp.abs(z)))
    gl_ref[...] = log_sig * (1.0 / GLA_TAU)


def _inproj_a(x, g, w, wa, ba, tm):
    n, d = x.shape
    row = lambda i: (i, 0)
    outs = [
        jax.ShapeDtypeStruct((n, GLA_QK), F32), jax.ShapeDtypeStruct((n, GLA_QK), F32),
        jax.ShapeDtypeStruct((n, GLA_QK), F32), jax.ShapeDtypeStruct((n, GLA_V), BF16),
        jax.ShapeDtypeStruct((n, GLA_V), F32), jax.ShapeDtypeStruct((n, MEM_WIDTH), BF16),
    ]
    return pl.pallas_call(
        _inproj_a_kernel,
        out_shape=outs,
        grid=(n // tm,),
        in_specs=[pl.BlockSpec((tm, d), row), _const_spec((1, d)), _const_spec(w.shape),
                  _const_spec(wa.shape), _const_spec(ba.shape)],
        out_specs=[pl.BlockSpec((tm, s.shape[1]), row) for s in outs],
        compiler_params=_params(("parallel",)),
        name="inproj_a",
    )(x, g, w, wa, ba)


def _gla_constants():
    c = GLA_CHUNK
    w = np.zeros((8 * c, c), np.float32)
    masks = np.zeros((GLA_LEVELS + 1, c, c), np.float32)
    masks[0] = np.eye(c)
    for l in range(1, GLA_LEVELS + 1):
        blk, half = 2 ** l, 2 ** (l - 1)
        for i in range(c):
            mid = (i // blk) * blk + half - 1
            if i % blk >= half:
                w[(l - 1) * c + i, mid + 1:i + 1] = 1.0
            else:
                w[(l - 1) * c + i, i + 1:mid + 1] = 1.0
        for t in range(c):
            for s in range(c):
                if t // blk == s // blk and t % blk >= half and s % blk < half:
                    masks[l, t, s] = 1.0
    for i in range(c):
        w[6 * c + i, :i + 1] = 1.0
        w[7 * c + i, i + 1:] = 1.0
    return w, np.tile(masks, (1, 1, GLA_HEADS))


def _gla_kernel(q_ref, k_ref, gl_ref, v_ref, r_ref, gh_ref, wcat_ref, msk_ref, o_ref, st_ref, *, n_chunks):
    c = GLA_CHUNK

    @pl.when(pl.program_id(1) == 0)
    def _():
        st_ref[...] = jnp.zeros_like(st_ref)

    lane_qk = lax.broadcasted_iota(jnp.int32, (1, GLA_QK), 1) // GLA_DK
    lane_v = lax.broadcasted_iota(jnp.int32, (1, GLA_V), 1) // GLA_DV
    ones = jnp.ones((c, LANE), BF16)
    wcat = wcat_ref[...]

    def chunk(ci, carry):
        rows = pl.ds(pl.multiple_of(ci * c, c), c)
        g = gl_ref[rows, :]
        g_hi = g.astype(BF16)
        g_lo = (g - g_hi.astype(F32)).astype(BF16)
        e = jnp.exp(_dot(wcat, g_hi) + _dot(wcat, g_lo))
        dec = jnp.exp(_dot_tn(g_hi, ones) + _dot_tn(g_lo, ones))
        q = q_ref[rows, :]
        k = k_ref[rows, :]
        v = v_ref[rows, :]

        attn = jnp.zeros((c, GLA_HEADS * c), F32)
        for l in range(GLA_LEVELS + 1):
            if l == 0:
                ql, kl = q, k
            else:
                el = e[(l - 1) * c:l * c]
                ql, kl = q * el, k * el
            kstack = jnp.concatenate(
                [jnp.where(lane_qk == h, kl, 0.0).astype(BF16) for h in range(GLA_HEADS)], axis=0)
            attn = attn + _dot_nt(ql.astype(BF16), kstack) * msk_ref[l]
        vstack = jnp.concatenate(
            [jnp.where(lane_v == h, v, jnp.zeros_like(v)) for h in range(GLA_HEADS)], axis=0)
        o_intra = _dot(attn.astype(BF16), vstack)

        st = st_ref[...]
        qi = q * e[6 * c:7 * c]
        qstack = jnp.concatenate(
            [jnp.where(lane_qk == h, qi, 0.0).astype(BF16) for h in range(GLA_HEADS)], axis=0)
        o_inter = _dot(qstack, st.astype(BF16))
        kk = (k * e[7 * c:8 * c]).astype(BF16)
        upd = _dot_tn(kk, v)
        new_rows = []
        for h in range(GLA_HEADS):
            rs = slice(h * GLA_DK, (h + 1) * GLA_DK)
            new_rows.append(dec[rs] * st[rs] + upd[rs, h * GLA_DV:(h + 1) * GLA_DV])
        st_ref[...] = jnp.concatenate(new_rows, axis=0)

        for h in range(GLA_HEADS):
            cs = slice(h * GLA_DV, (h + 1) * GLA_DV)
            o = o_intra[:, cs] + o_inter[h * c:(h + 1) * c]
            on = _normalize(o) * gh_ref[...]
            r = r_ref[rows, cs]
            o_ref[rows, cs] = (on * (r * _sigmoid(r))).astype(BF16)
        return carry

    lax.fori_loop(0, n_chunks, chunk, 0)


def _gla(q, k, gl, v, r, g_head, batch, ct):
    n = q.shape[0]
    t = n // batch
    nt = t // ct
    wcat, masks = _gla_constants()
    row = lambda b, i: (b * nt + i, 0)
    return pl.pallas_call(
        functools.partial(_gla_kernel, n_chunks=ct // GLA_CHUNK),
        out_shape=jax.ShapeDtypeStruct((n, GLA_V), BF16),
        grid=(batch, nt),
        in_specs=[pl.BlockSpec((ct, GLA_QK), row), pl.BlockSpec((ct, GLA_QK), row),
                  pl.BlockSpec((ct, GLA_QK), row), pl.BlockSpec((ct, GLA_V), row),
                  pl.BlockSpec((ct, GLA_V), row), _const_spec((1, GLA_DV)),
                  _const_spec(wcat.shape), _const_spec(masks.shape)],
        out_specs=pl.BlockSpec((ct, GLA_V), row),
        scratch_shapes=[pltpu.VMEM((GLA_QK, GLA_DV), F32)],
        compiler_params=_params(("parallel", "arbitrary")),
        name="gla",
    )(q, k, gl, v, r, g_head, jnp.asarray(wcat, BF16), jnp.asarray(masks, F32))


def _attn_out_kernel(o_ref, mq_ref, mk_ref, mv_ref, w_ref, x_ref, out_ref):
    parts = [o_ref[...]]
    for h in range(MEM_HEADS):
        cs = slice(h * MEM_DH, (h + 1) * MEM_DH)
        s = _dot_nt(mq_ref[:, cs], mk_ref[0, :, cs]) * (MEM_DH ** -0.5)
        e = jnp.exp(s - jnp.max(s, axis=-1, keepdims=True))
        p = e / jnp.sum(e, axis=-1, keepdims=True)
        parts.append(_dot(p.astype(BF16), mv_ref[0, :, cs]).astype(BF16))
    cat = jnp.concatenate(parts, axis=-1)
    out_ref[...] = x_ref[...] + _dot(cat, w_ref[...])


def _attn_out(o, mq, mkv, w_out, x, batch, tm):
    n, d = x.shape
    t = n // batch
    nt = t // tm
    m = mkv.shape[1]
    row = lambda b, i: (b * nt + i, 0)
    return pl.pallas_call(
        _attn_out_kernel,
        out_shape=jax.ShapeDtypeStruct((n, d), F32),
        grid=(batch, nt),
        in_specs=[pl.BlockSpec((tm, o.shape[1]), row), pl.BlockSpec((tm, MEM_WIDTH), row),
                  pl.BlockSpec((1, m, MEM_WIDTH), lambda b, i: (b, 0, 0)),
                  pl.BlockSpec((1, m, MEM_WIDTH), lambda b, i: (b, 0, 1)),
                  _const_spec(w_out.shape), pl.BlockSpec((tm, d), row)],
        out_specs=pl.BlockSpec((tm, d), row),
        compiler_params=_params(("parallel", "parallel")),
        name="attn_out",
    )(o, mq, mkv, mkv, w_out, x)


def _ffn_kernel(x_ref, g_ref, wup_ref, cw_ref, cb_ref, wd_ref, gf_ref, out_ref, act_ref, tail_ref, *,
                final_norm):
    @pl.when(pl.program_id(1) == 0)
    def _():
        tail_ref[...] = jnp.zeros_like(tail_ref)

    x = x_ref[...]
    tm = x.shape[0]
    h = (_normalize(x) * g_ref[...]).astype(BF16)
    rid = lax.broadcasted_iota(jnp.int32, (8, FFN_CHUNK), 0)
    for j in range(FFN_DIM // FFN_CHUNK):
        cs = slice(j * FFN_CHUNK, (j + 1) * FFN_CHUNK)
        a = _dot(h, wup_ref[:, cs])
        b = _dot(h, wup_ref[:, FFN_DIM + j * FFN_CHUNK:FFN_DIM + (j + 1) * FFN_CHUNK])
        tail = tail_ref[j]
        r1 = pltpu.roll(a, 1, 0)
        r2 = pltpu.roll(a, 2, 0)
        top1 = jnp.where(rid == 0, tail[7:8], r1[:8])
        top2 = jnp.where(rid == 0, tail[6:7], jnp.where(rid == 1, tail[7:8], r2[:8]))
        a1 = jnp.concatenate([top1, r1[8:]], axis=0)
        a2 = jnp.concatenate([top2, r2[8:]], axis=0)
        tail_ref[j] = a[tm - 8:]
        ac = a2 * cw_ref[0:1, cs] + a1 * cw_ref[1:2, cs] + a * cw_ref[2:3, cs] + cb_ref[:, cs]
        act_ref[:, cs] = (ac * _sigmoid(ac) * b).astype(BF16)
    y = x + _dot(act_ref[...], wd_ref[...])
    if final_norm:
        y = _normalize(y) * gf_ref[...]
    out_ref[...] = y


def _ffn(x, g, w_up, conv_w, conv_b, w_down, g_final, batch, tm, final_norm):
    n, d = x.shape
    nt = n // batch // tm
    row = lambda b, i: (b * nt + i, 0)
    once = pl.Buffered(1)
    return pl.pallas_call(
        functools.partial(_ffn_kernel, final_norm=final_norm),
        out_shape=jax.ShapeDtypeStruct((n, d), F32),
        grid=(batch, nt),
        in_specs=[pl.BlockSpec((tm, d), row), _const_spec((1, d)),
                  pl.BlockSpec(w_up.shape, lambda b, i: (0, 0), pipeline_mode=once),
                  _const_spec(conv_w.shape), _const_spec(conv_b.shape),
                  pl.BlockSpec(w_down.shape, lambda b, i: (0, 0), pipeline_mode=once),
                  _const_spec((1, d))],
        out_specs=pl.BlockSpec((tm, d), row),
        scratch_shapes=[pltpu.VMEM((tm, FFN_DIM), BF16),
                        pltpu.VMEM((FFN_DIM // FFN_CHUNK, 8, FFN_CHUNK), F32)],
        compiler_params=_params(("parallel", "arbitrary")),
        name="conv_ffn",
    )(x, g, w_up, conv_w, conv_b, w_down, g_final)


def _proj_b_kernel(x_ref, gkv_ref, gmix_ref, wkv_ref, wqt_ref, wgt_ref, wmq_ref,
                   cmp_ref, kvs_ref, qt_ref, gt_ref, mq_ref):
    xn = _normalize(x_ref[...])
    hkv = (xn * gkv_ref[...]).astype(BF16)
    h1 = (xn * gmix_ref[...]).astype(BF16)
    cmp_ref[0] = _dot(hkv, wkv_ref[:, 0:KV_WIDTH]).astype(BF16)
    cmp_ref[1] = _dot(hkv, wkv_ref[:, KV_WIDTH:2 * KV_WIDTH]).astype(BF16)
    kvs_ref[...] = _dot(hkv, wkv_ref[:, 2 * KV_WIDTH:]).astype(BF16)
    qt_ref[0] = (_dot_nt(wqt_ref[...], h1) * (NSA_DH ** -0.5)).astype(BF16)
    gt_ref[0] = _sigmoid(_dot_nt(wgt_ref[...], h1))
    mq_ref[...] = _dot(h1, wmq_ref[...]).astype(BF16)


def _proj_b(x, g_kv, g_mix, w_kv, wqt, wgt, wmq, batch, tm):
    n, d = x.shape
    t = n // batch
    nt = t // tm
    row = lambda b, i: (b * nt + i, 0)
    outs = [
        jax.ShapeDtypeStruct((2, n, KV_WIDTH), BF16),
        jax.ShapeDtypeStruct((n, 4 * KV_WIDTH), BF16),
        jax.ShapeDtypeStruct((batch, NSA_WIDTH, t), BF16),
        jax.ShapeDtypeStruct((batch, wgt.shape[0], t), F32),
        jax.ShapeDtypeStruct((n, MEM_WIDTH), BF16),
    ]
    return pl.pallas_call(
        _proj_b_kernel,
        out_shape=outs,
        grid=(batch, nt),
        in_specs=[pl.BlockSpec((tm, d), row), _const_spec((1, d)), _const_spec((1, d)),
                  _const_spec(w_kv.shape), _const_spec(wqt.shape), _const_spec(wgt.shape),
                  _const_spec(wmq.shape)],
        out_specs=[pl.BlockSpec((2, tm, KV_WIDTH), lambda b, i: (0, b * nt + i, 0)),
                   pl.BlockSpec((tm, 4 * KV_WIDTH), row),
                   pl.BlockSpec((1, NSA_WIDTH, tm), lambda b, i: (b, 0, i)),
                   pl.BlockSpec((1, wgt.shape[0], tm), lambda b, i: (b, 0, i)),
                   pl.BlockSpec((tm, MEM_WIDTH), row)],
        compiler_params=_params(("parallel", "parallel")),
        name="proj_b",
    )(x, g_kv, g_mix, w_kv, wqt, wgt, wmq)


def _compress_kernel(a_ref, w1_ref, pe_ref, w2_ref, o_ref):
    a = a_ref[0, 0]
    w1 = w1_ref[0]
    u = _dot(a, w1)
    cpe = _dot(pe_ref[0], w1)
    nrow = a.shape[0]
    hid = []
    for g in range(NSA_GROUPS):
        c0 = slice((2 * g) * CMP_HIDDEN, (2 * g + 1) * CMP_HIDDEN)
        c1 = slice((2 * g + 1) * CMP_HIDDEN, (2 * g + 2) * CMP_HIDDEN)
        nxt = pltpu.roll(u[:, c1], nrow - 1, 0)
        hid.append(u[:, c0] + nxt + cpe[0:1, c0] + cpe[8:9, c1])
    hcat = jax.nn.gelu(jnp.concatenate(hid, axis=-1), approximate=True).astype(BF16)
    o_ref[0, 0] = _dot(hcat, w2_ref[0]).astype(BF16)


def _compress(a, w1, pe, w2):
    _, batch, nsub, width = a.shape
    return pl.pallas_call(
        _compress_kernel,
        out_shape=jax.ShapeDtypeStruct((2, batch, nsub, KV_WIDTH), BF16),
        grid=(2, batch),
        in_specs=[pl.BlockSpec((1, 1, nsub, width), lambda s, b: (s, b, 0, 0)),
                  pl.BlockSpec((1,) + w1.shape[1:], lambda s, b: (s, 0, 0)),
                  pl.BlockSpec((1,) + pe.shape[1:], lambda s, b: (s, 0, 0)),
                  pl.BlockSpec((1,) + w2.shape[1:], lambda s, b: (s, 0, 0))],
        out_specs=pl.BlockSpec((1, 1, nsub, KV_WIDTH), lambda s, b: (s, b, 0, 0)),
        compiler_params=_params(("parallel", "parallel")),
        name="compress",
    )(a, w1, pe, w2)


def _alibi_tiles(ncmp_pad):
    slopes = 2.0 ** (-8.0 * np.arange(1, NSA_HEADS + 1) / NSA_HEADS)
    tl = np.tile(np.arange(Q_BLOCK), NSA_HPG)[None, :]
    cmp_end = (np.arange(ncmp_pad) * CMP_STRIDE + CMP_BLOCK - 1)[:, None]
    key = np.arange(SEL_BLOCK)[:, None]
    dist_win = tl + WINDOW - np.arange(WIN_KEYS)[:, None]
    c_cmp = np.zeros((NSA_GROUPS, ncmp_pad, NSA_HPG * Q_BLOCK), np.float32)
    c_blk = np.zeros((NSA_GROUPS, SEL_BLOCK, NSA_HPG * Q_BLOCK), np.float32)
    c_win = np.zeros((NSA_GROUPS, WIN_KEYS, NSA_HPG * Q_BLOCK), np.float32)
    srow = np.zeros((NSA_GROUPS, 1, NSA_HPG * Q_BLOCK), np.float32)
    for g in range(NSA_GROUPS):
        s = np.repeat(slopes[g * NSA_HPG:(g + 1) * NSA_HPG], Q_BLOCK)[None, :]
        c_cmp[g] = -s * (tl - cmp_end)
        c_blk[g] = -s * (tl - key)
        c_win[g] = np.where((dist_win >= 0) & (dist_win < WINDOW), -s * dist_win, -np.inf)
        srow[g] = s
    return c_cmp, c_blk, c_win, srow


def _group_queries(qt_ref, g):
    zeros = jnp.zeros((NSA_DH, Q_BLOCK), BF16)
    cols = []
    for hh in range(NSA_HPG):
        h = g * NSA_HPG + hh
        piece = qt_ref[0, h * NSA_DH:(h + 1) * NSA_DH, :]
        cols.append(jnp.concatenate([piece, zeros] if g == 0 else [zeros, piece], axis=0))
    return jnp.concatenate(cols, axis=1)


def _softmax_step(carry, s, v_rows, rows_g):
    m, l, acc = carry
    m_new = jnp.maximum(m, jnp.max(s, axis=0, keepdims=True))
    m_safe = jnp.where(m_new == NEG_INF, 0.0, m_new)
    alpha = jnp.where(m == NEG_INF, 0.0, jnp.exp(m - m_safe))
    p = jnp.exp(s - m_safe)
    l = alpha * l + jnp.sum(p, axis=0, keepdims=True)
    acc = alpha * acc + _dot_tn(v_rows, p.astype(BF16))[rows_g]
    return (m_new, l, acc), alpha, p


def _nsa_a_kernel(qt_ref, ckv_ref, ccmp_ref, srow_ref, ovt_ref, oc_ref, sel_ref, cnt_ref):
    qb = pl.program_id(1)
    base = (qb * Q_BLOCK).astype(F32)
    width = NSA_HPG * Q_BLOCK
    nblk = sel_ref.shape[3]
    jidx = lax.broadcasted_iota(jnp.int32, (nblk, Q_BLOCK), 0)
    tl = lax.broadcasted_iota(jnp.int32, (nblk, Q_BLOCK), 1)
    cur = 2 * qb + jnp.where(tl >= SEL_BLOCK, 1, 0)
    valid_blk = jidx <= cur
    n_chunks = (qb * (Q_BLOCK // CMP_STRIDE) + (Q_BLOCK - CMP_BLOCK) // CMP_STRIDE) // CMP_CHUNK + 1
    groups = range(NSA_GROUPS)
    qts = [_group_queries(qt_ref, g) for g in groups]
    shifts = [srow_ref[g] * base for g in groups]

    def body(c, carry):
        r0 = pl.multiple_of(c * CMP_CHUNK, CMP_CHUNK)
        kc = ckv_ref[0, 0, pl.ds(r0, CMP_CHUNK), :]
        vc = ckv_ref[1, 0, pl.ds(r0, CMP_CHUNK), :]
        ov = ovt_ref[c]
        ov2 = jnp.concatenate([ov, ov], axis=1)
        out = []
        raws = [_dot(kc, qts[g]) for g in groups]
        for g in groups:
            state, imp = carry[g]
            bias = ccmp_ref[g, pl.ds(r0, CMP_CHUNK), :] - shifts[g]
            s = raws[g] + jnp.where(bias <= 0.0, bias, NEG_INF)
            state, alpha, p = _softmax_step(state, s, vc, slice(g * NSA_DH, (g + 1) * NSA_DH))
            p1 = p.astype(BF16)
            p2 = (p - p1.astype(F32)).astype(BF16)
            imp = alpha * imp + _dot(ov2, jnp.concatenate([p1, p2], axis=0))
            out.append((state, imp))
        return tuple(out)

    init = ((jnp.full((1, width), NEG_INF, F32), jnp.zeros((1, width), F32), jnp.zeros((NSA_DH, width), F32)),
            jnp.zeros((nblk, width), F32))
    final = lax.fori_loop(0, n_chunks, body, (init,) * NSA_GROUPS)
    imps = []
    for g in groups:
        (_, l, acc), imp = final[g]
        inv = 1.0 / jnp.maximum(l, 1e-30)
        oc = acc * inv
        imp = imp * inv
        for hh in range(NSA_HPG):
            h = g * NSA_HPG + hh
            oc_ref[0, h * NSA_DH:(h + 1) * NSA_DH, :] = oc[:, hh * Q_BLOCK:(hh + 1) * Q_BLOCK]
        imp_g = imp[:, 0:Q_BLOCK]
        for hh in range(1, NSA_HPG):
            imp_g = imp_g + imp[:, hh * Q_BLOCK:(hh + 1) * Q_BLOCK]
        imps.append(imp_g)

    def emit(g, sel):
        sel_ref[0, g, 0] = sel
        cnt_ref[0, g, 0] = _dot_nt(jnp.ones((8, Q_BLOCK), BF16), sel.astype(BF16))

    @pl.when(qb < RANK_FIRST_TILE)
    def _():
        for g in groups:
            emit(g, jnp.where(valid_blk, 1.0, 0.0))

    @pl.when(qb >= RANK_FIRST_TILE)
    def _():
        jf = jidx.astype(F32)
        cand = (jidx >= 1) & (jidx <= cur - 2)
        v0 = [jnp.where(cand, imps[g], NEG_INF) for g in groups]
        v = list(v0)
        for _ in range(SEL_TOPN - 3):
            for g in groups:
                mx = jnp.max(v[g], axis=0, keepdims=True)
                first = jnp.min(jnp.where(v[g] == mx, jf, float(nblk)), axis=0, keepdims=True)
                v[g] = jnp.where(jf == first, NEG_INF, v[g])
        always = (jidx == 0) | (jidx == cur) | (jidx == cur - 1)
        for g in groups:
            emit(g, jnp.where((v[g] != v0[g]) | always, 1.0, 0.0))


def _nsa_a(qt, ckv, c_cmp, srow, ovt):
    batch, _, t = qt.shape
    nqb = t // Q_BLOCK
    ns = t // SEL_BLOCK
    ncmp = ckv.shape[2]
    ovt = ovt.reshape(ns, ncmp // CMP_CHUNK, CMP_CHUNK).transpose(1, 0, 2)
    outs = [jax.ShapeDtypeStruct((batch, NSA_WIDTH, t), F32),
            jax.ShapeDtypeStruct((batch, NSA_GROUPS, nqb, ns, Q_BLOCK), F32),
            jax.ShapeDtypeStruct((batch, NSA_GROUPS, nqb, 8, ns), F32)]
    return pl.pallas_call(
        _nsa_a_kernel,
        out_shape=outs,
        grid=(batch, nqb),
        in_specs=[pl.BlockSpec((1, NSA_WIDTH, Q_BLOCK), lambda b, i: (b, 0, i)),
                  pl.BlockSpec((2, 1, ncmp, KV_WIDTH), lambda b, i: (0, b, 0, 0)),
                  _const_spec(c_cmp.shape), _const_spec(srow.shape), _const_spec(ovt.shape)],
        out_specs=[pl.BlockSpec((1, NSA_WIDTH, Q_BLOCK), lambda b, i: (b, 0, i)),
                   pl.BlockSpec((1, NSA_GROUPS, 1, ns, Q_BLOCK), lambda b, i: (b, 0, i, 0, 0)),
                   pl.BlockSpec((1, NSA_GROUPS, 1, 8, ns), lambda b, i: (b, 0, i, 0, 0))],
        compiler_params=_params(("parallel", "parallel")),
        name="nsa_select",
    )(qt, ckv, c_cmp, srow, ovt)


def _block_softmax_step(carry, raw, consts, shifts, v_rows, rows_g):
    m, l, acc = carry
    us = [raw[i * SEL_BLOCK:(i + 1) * SEL_BLOCK] + c for i, c in enumerate(consts)]
    m_new = m
    for u, sh in zip(us, shifts):
        m_new = jnp.maximum(m_new, jnp.max(u, axis=0, keepdims=True) + sh)
    m_safe = jnp.where(m_new == NEG_INF, 0.0, m_new)
    alpha = jnp.where(m == NEG_INF, 0.0, jnp.exp(m - m_safe))
    ps = [jnp.exp(u - (m_safe - sh)) for u, sh in zip(us, shifts)]
    l = alpha * l
    for p in ps:
        l = l + jnp.sum(p, axis=0, keepdims=True)
    pv = _dot_tn(v_rows, jnp.concatenate(ps, axis=0).astype(BF16))
    return m_new, l, alpha * acc + pv[rows_g]


def _nsa_b_kernel(idx_ref, qt_ref, ks_ref, vs_ref, kw_ref, vw_ref, sel_ref, oc_ref, gt_ref, cblk_ref, cwin_ref,
                  srow_ref, o_ref, *, idx_words):
    b = pl.program_id(0)
    qb = pl.program_id(1)
    base = qb * Q_BLOCK
    width = NSA_HPG * Q_BLOCK
    groups = range(NSA_GROUPS)
    qts = [_group_queries(qt_ref, g) for g in groups]
    srows = [srow_ref[g] for g in groups]
    rows = [slice(g * NSA_DH, (g + 1) * NSA_DH) for g in groups]
    entries = [((b * NSA_GROUPS + g) * pl.num_programs(1) + qb) * idx_words for g in groups]
    n_act = [idx_ref[entries[g] + idx_words - 1] for g in groups]
    init = (jnp.full((1, width), NEG_INF, F32), jnp.zeros((1, width), F32), jnp.zeros((NSA_DH, width), F32))
    zero_row = jnp.zeros((1, width), F32)

    def gate_row(g, j, live):
        picked = sel_ref[0, g, 0, pl.ds(j, 1), :]
        gate = jnp.where((picked > 0.0) & live, 0.0, NEG_INF)
        return jnp.concatenate([gate] * NSA_HPG, axis=1)

    def stage(it):
        out = []
        for g in groups:
            word = idx_ref[entries[g] + it]
            kts, vts, shifts = [], [], []
            for i in range(SLC_CHUNK):
                live = it * SLC_CHUNK + i < n_act[g]
                j = jnp.where(live, lax.shift_right_logical(word, 8 * i) & 0xFF, 0)
                k0 = pl.multiple_of(j * SEL_BLOCK, SEL_BLOCK)
                kts.append(ks_ref[pl.ds(k0, SEL_BLOCK), :])
                vts.append(vs_ref[pl.ds(k0, SEL_BLOCK), :])
                shifts.append(gate_row(g, j, live) - srows[g] * (base - k0).astype(F32))
            out.append((_dot(jnp.concatenate(kts, axis=0), qts[g]), tuple(shifts), jnp.concatenate(vts, axis=0)))
        return tuple(out)

    def body(it, states):
        staged = stage(it)
        return tuple(_block_softmax_step(states[g], staged[g][0], [cblk_ref[g]] * SLC_CHUNK, staged[g][1],
                                         staged[g][2], rows[g]) for g in groups)

    n_it = (jnp.maximum(n_act[0], n_act[1]) + SLC_CHUNK - 1) // SLC_CHUNK
    slc = list(lax.fori_loop(0, n_it, body, (init,) * NSA_GROUPS))

    diag = range(WINDOW // SEL_BLOCK, WIN_KEYS // SEL_BLOCK)
    r0 = pl.multiple_of(base, Q_BLOCK)
    ks_d = ks_ref[pl.ds(r0, Q_BLOCK), :]
    vs_d = vs_ref[pl.ds(r0, Q_BLOCK), :]
    raw_d = [_dot(ks_d, qts[g]) for g in groups]
    n_win = WIN_KEYS // SEL_BLOCK
    win_staged = []
    for c in range(WIN_KEYS // WIN_CHUNK):
        blocks = range(c * n_win // 2, (c + 1) * n_win // 2)
        kws, vws, offs = [], [], []
        for r in blocks:
            j = 2 * qb - WINDOW // SEL_BLOCK + r
            k0 = pl.multiple_of(jnp.maximum(j, 0) * SEL_BLOCK, SEL_BLOCK)
            kws.append(kw_ref[pl.ds(k0, SEL_BLOCK), :])
            vws.append(vw_ref[pl.ds(k0, SEL_BLOCK), :])
            offs.append(zero_row + jnp.where(j >= 0, 0.0, NEG_INF))
        kw = jnp.concatenate(kws, axis=0)
        win_staged.append((blocks, [_dot(kw, qts[g]) for g in groups], offs, jnp.concatenate(vws, axis=0)))

    for g in groups:
        consts = [cwin_ref[g, r * SEL_BLOCK:(r + 1) * SEL_BLOCK, :] for r in diag]
        shifts = [gate_row(g, 2 * qb + i, True) for i in range(len(diag))]
        slc[g] = _block_softmax_step(slc[g], raw_d[g], consts, shifts, vs_d, rows[g])
    win = [init] * NSA_GROUPS
    for blocks, raws, offs, vw in win_staged:
        for g in groups:
            consts = [cwin_ref[g, r * SEL_BLOCK:(r + 1) * SEL_BLOCK, :] for r in blocks]
            win[g] = _block_softmax_step(win[g], raws[g], consts, offs, vw, rows[g])

    heads_out = []
    for g in groups:
        o_s = slc[g][2] * (1.0 / jnp.maximum(slc[g][1], 1e-30))
        o_w = win[g][2] * (1.0 / jnp.maximum(win[g][1], 1e-30))
        for hh in range(NSA_HPG):
            h = g * NSA_HPG + hh
            cs = slice(hh * Q_BLOCK, (hh + 1) * Q_BLOCK)
            gates = gt_ref[0, 3 * h:3 * h + 3, :]
            heads_out.append(gates[0:1] * oc_ref[0, h * NSA_DH:(h + 1) * NSA_DH, :]
                             + gates[1:2] * o_s[:, cs] + gates[2:3] * o_w[:, cs])
    o_ref[...] = jnp.concatenate(heads_out, axis=0).T.astype(BF16)


def _nsa_b(idx, qt, kvs, sel, oc, gt, c_blk, c_win, srow):
    batch, _, t = qt.shape
    nqb = t // Q_BLOCK
    ns = t // SEL_BLOCK
    kv_spec = lambda c: pl.BlockSpec((t, KV_WIDTH), lambda b, i, idx_ref: (b, c))
    grid_spec = pltpu.PrefetchScalarGridSpec(
        num_scalar_prefetch=1,
        grid=(batch, nqb),
        in_specs=[pl.BlockSpec((1, NSA_WIDTH, Q_BLOCK), lambda b, i, r: (b, 0, i)),
                  kv_spec(0), kv_spec(1), kv_spec(2), kv_spec(3),
                  pl.BlockSpec((1, NSA_GROUPS, 1, ns, Q_BLOCK), lambda b, i, r: (b, 0, i, 0, 0)),
                  pl.BlockSpec((1, NSA_WIDTH, Q_BLOCK), lambda b, i, r: (b, 0, i)),
                  pl.BlockSpec((1, gt.shape[1], Q_BLOCK), lambda b, i, r: (b, 0, i)),
                  pl.BlockSpec(c_blk.shape, lambda b, i, r: (0, 0, 0)),
                  pl.BlockSpec(c_win.shape, lambda b, i, r: (0, 0, 0)),
                  pl.BlockSpec(srow.shape, lambda b, i, r: (0, 0, 0))],
        out_specs=pl.BlockSpec((Q_BLOCK, NSA_WIDTH), lambda b, i, r: (b * nqb + i, 0)),
    )
    return pl.pallas_call(
        functools.partial(_nsa_b_kernel, idx_words=ns // 4 + 1),
        out_shape=jax.ShapeDtypeStruct((batch * t, NSA_WIDTH), BF16),
        grid_spec=grid_spec,
        compiler_params=_params(("parallel", "parallel")),
        name="nsa_attend",
    )(idx, qt, kvs, kvs, kvs, kvs, sel, oc, gt, c_blk, c_win, srow)


def _pad_cols(w, width):
    return jnp.pad(w, ((0, 0), (0, width - w.shape[1])))


def _compress_weights(w1, pe, w2):
    nsub = CMP_STRIDE
    w1r = w1.reshape(2, nsub, NSA_DH, CMP_HIDDEN)
    same_group = jnp.eye(NSA_GROUPS, dtype=F32)
    big = same_group[None, :, None, :, None, None] * w1r.transpose(1, 2, 0, 3)[:, None, :, None, :, :]
    big = big.reshape(nsub * KV_WIDTH, NSA_GROUPS * 2 * CMP_HIDDEN)
    per = pe.reshape(2, nsub, 1, NSA_DH)
    pe2 = jnp.broadcast_to(per, (2, nsub, NSA_GROUPS, NSA_DH)).reshape(2, 1, nsub * KV_WIDTH)
    pe2 = jnp.broadcast_to(pe2, (2, 8, nsub * KV_WIDTH)).reshape(16, nsub * KV_WIDTH)
    w2bd = same_group[:, None, :, None] * w2[None, :, None, :]
    return big.astype(BF16), pe2.astype(BF16), w2bd.reshape(NSA_GROUPS * CMP_HIDDEN, KV_WIDTH).astype(BF16)


def _selection_overlap_t(ncmp_pad, ns):
    cs = np.arange(ncmp_pad) * CMP_STRIDE
    ss = np.arange(ns) * SEL_BLOCK
    ov = np.minimum(cs[:, None] + CMP_BLOCK, ss[None, :] + SEL_BLOCK) - np.maximum(cs[:, None], ss[None, :])
    return (np.clip(ov, 0, None).astype(np.float32) / CMP_BLOCK).T


def _active_blocks(cnt):
    ns = cnt.shape[-1]
    before_tile = jnp.arange(ns)[None, :] < (Q_BLOCK // SEL_BLOCK) * jnp.arange(cnt.shape[2])[:, None]
    flags = (cnt[:, :, :, 0, :] > 0.0) & before_tile
    order = jnp.argsort(jnp.where(flags, 0, 1).astype(jnp.int32), axis=-1, stable=True).astype(jnp.int32)
    packed = jnp.sum(order.reshape(order.shape[:-1] + (ns // 4, 4)) << (8 * jnp.arange(4, dtype=jnp.int32)), axis=-1)
    n_act = jnp.sum(flags, axis=-1, dtype=jnp.int32)
    return jnp.concatenate([packed, n_act[..., None]], axis=-1).reshape(-1)


def kernel(x, mem, g_mix, g_ffn, g_mem, w_mem_kv, w_up, conv_w, conv_b, w_down,
           a_w_in, a_w_alpha, a_b_alpha, a_g_head, a_w_out,
           g_kv, w_kv, pe_k, pe_v, w_ck1, w_ck2, w_cv1, w_cv2,
           b_w_in, b_w_out, g_final):
    batch, t, d = x.shape
    n = batch * t
    m = mem.shape[1]
    tm = min(512, t)
    xf = x.reshape(n, d)
    memf = mem.reshape(batch * m, d)
    row = lambda v: v.reshape(1, -1)

    mkv0 = _rms_proj(memf, row(g_mem[0]), w_mem_kv[0].astype(BF16), m, BF16).reshape(batch, m, 2 * MEM_WIDTH)
    wa = a_w_in[0]
    c_alr = 2 * GLA_QK + 2 * GLA_V
    w_a = jnp.concatenate([wa[:, :c_alr], wa[:, c_alr + GLA_RANK:], _pad_cols(wa[:, c_alr:c_alr + GLA_RANK], LANE)],
                          axis=1).astype(BF16)
    w_alpha = jnp.pad(a_w_alpha[0], ((0, LANE - GLA_RANK), (0, 0))).astype(BF16)
    q, k, gl, v, r, mq = _inproj_a(xf, row(g_mix[0]), w_a, w_alpha, row(a_b_alpha[0]), tm)
    o = _gla(q, k, gl, v, r, row(a_g_head[0]), batch, tm)
    xf = _attn_out(o, mq, mkv0, a_w_out[0].astype(BF16), xf, batch, tm)
    xf = _ffn(xf, row(g_ffn[0]), w_up[0].astype(BF16), conv_w[0], row(conv_b[0]), w_down[0].astype(BF16),
              row(g_final), batch, tm, final_norm=False)

    wb = b_w_in[0]
    wqt = wb[:, :NSA_WIDTH].T.astype(BF16)
    n_gate = 3 * NSA_HEADS
    wgt = jnp.pad(wb[:, NSA_WIDTH:NSA_WIDTH + n_gate].T, ((0, 32 - n_gate), (0, 0))).astype(BF16)
    wmq = wb[:, NSA_WIDTH + n_gate:].astype(BF16)
    ckv_in, kvs, qt, gt, mq1 = _proj_b(xf, row(g_kv), row(g_mix[1]), w_kv.astype(BF16), wqt, wgt, wmq, batch, tm)
    nsub = t // CMP_STRIDE
    w1k, pek, w2k = _compress_weights(w_ck1, pe_k, w_ck2)
    w1v, pev, w2v = _compress_weights(w_cv1, pe_v, w_cv2)
    ckv = _compress(ckv_in.reshape(2, batch, nsub, CMP_STRIDE * KV_WIDTH),
                    jnp.stack([w1k, w1v]), jnp.stack([pek, pev]), jnp.stack([w2k, w2v]))

    ns = t // SEL_BLOCK
    c_cmp, c_blk, c_win, srow = _alibi_tiles(nsub)
    ovt = jnp.asarray(_selection_overlap_t(nsub, ns), BF16)
    oc, sel, cnt = _nsa_a(qt, ckv, jnp.asarray(c_cmp), jnp.asarray(srow), ovt)
    idx = _active_blocks(cnt)
    o1 = _nsa_b(idx, qt, kvs, sel, oc, gt, jnp.asarray(c_blk), jnp.asarray(c_win), jnp.asarray(srow))
    mkv1 = _rms_proj(memf, row(g_mem[1]), w_mem_kv[1].astype(BF16), m, BF16).reshape(batch, m, 2 * MEM_WIDTH)
    xf = _attn_out(o1, mq1, mkv1, b_w_out[0].astype(BF16), xf, batch, tm)
    xf = _ffn(xf, row(g_ffn[1]), w_up[1].astype(BF16), conv_w[1], row(conv_b[1]), w_down[1].astype(BF16),
              row(g_final), batch, tm, final_norm=True)
    return xf.reshape(batch, t, d)
```

```python
import functools

import numpy as np
import jax
import jax.numpy as jnp
from jax import lax
from jax.experimental import pallas as pl
from jax.experimental.pallas import tpu as pltpu

F32 = jnp.float32
BF16 = jnp.bfloat16
EPS = 1e-6
NEG_INF = float("-inf")

V7X_VMEM_BYTES = 64 * 1024 * 1024
VMEM_LIMIT = V7X_VMEM_BYTES - 8 * 1024 * 1024

D_MODEL = 1024
MEM_HEADS = 4
MEM_DH = 128
MEM_WIDTH = MEM_HEADS * MEM_DH
GLA_HEADS = 4
GLA_DK = 64
GLA_DV = 128
GLA_QK = GLA_HEADS * GLA_DK
GLA_V = GLA_HEADS * GLA_DV
GLA_RANK = 16
GLA_TAU = 16.0
GLA_CHUNK = 64
GLA_LEVELS = 6
NSA_HEADS = 8
NSA_GROUPS = 2
NSA_HPG = NSA_HEADS // NSA_GROUPS
NSA_DH = 64
NSA_WIDTH = NSA_HEADS * NSA_DH
KV_WIDTH = NSA_GROUPS * NSA_DH
CMP_BLOCK = 32
CMP_STRIDE = 16
CMP_HIDDEN = 256
SEL_BLOCK = 64
SEL_TOPN = 16
WINDOW = 512
Q_BLOCK = 128
WIN_KEYS = WINDOW + Q_BLOCK
WIN_CHUNK = WIN_KEYS // 2
CMP_CHUNK = 128
SLC_CHUNK = 4
RANK_FIRST_TILE = SEL_TOPN * SEL_BLOCK // Q_BLOCK
FFN_DIM = 2816
FFN_CHUNK = 256
LANE = 128


def _dot(a, b):
    return jnp.dot(a, b, preferred_element_type=F32)


def _dot_nt(a, b):
    return lax.dot_general(a, b, (((1,), (1,)), ((), ())), preferred_element_type=F32)


def _dot_tn(a, b):
    return lax.dot_general(a, b, (((0,), (0,)), ((), ())), preferred_element_type=F32)


def _params(sem):
    return pltpu.CompilerParams(dimension_semantics=sem, vmem_limit_bytes=VMEM_LIMIT)


def _const_spec(shape):
    n = len(shape)
    return pl.BlockSpec(shape, lambda *_: (0,) * n)


def _normalize(x):
    return x * lax.rsqrt(jnp.mean(x * x, axis=-1, keepdims=True) + EPS)


def _sigmoid(x):
    return 1.0 / (1.0 + jnp.exp(-x))


def _rms_proj_kernel(x_ref, g_ref, w_ref, o_ref):
    h = (_normalize(x_ref[...]) * g_ref[...]).astype(BF16)
    o_ref[...] = _dot(h, w_ref[...]).astype(o_ref.dtype)


def _rms_proj(x, g, w, tm, out_dtype):
    n, d = x.shape
    p = w.shape[1]
    return pl.pallas_call(
        _rms_proj_kernel,
        out_shape=jax.ShapeDtypeStruct((n, p), out_dtype),
        grid=(n // tm,),
        in_specs=[pl.BlockSpec((tm, d), lambda i: (i, 0)), _const_spec((1, d)), _const_spec((d, p))],
        out_specs=pl.BlockSpec((tm, p), lambda i: (i, 0)),
        compiler_params=_params(("parallel",)),
        name="rms_proj",
    )(x, g, w)


def _inproj_a_kernel(x_ref, g_ref, w_ref, wa_ref, ba_ref, q_ref, k_ref, gl_ref, v_ref, r_ref, mq_ref):
    h = (_normalize(x_ref[...]) * g_ref[...]).astype(BF16)
    c = 0
    q_ref[...] = _dot(h, w_ref[:, c:c + GLA_QK]) * (GLA_DK ** -0.5)
    c += GLA_QK
    k_ref[...] = _dot(h, w_ref[:, c:c + GLA_QK])
    c += GLA_QK
    v_ref[...] = _dot(h, w_ref[:, c:c + GLA_V]).astype(BF16)
    c += GLA_V
    r_ref[...] = _dot(h, w_ref[:, c:c + GLA_V])
    c += GLA_V
    mq_ref[...] = _dot(h, w_ref[:, c:c + MEM_WIDTH]).astype(BF16)
    c += MEM_WIDTH
    alr = _dot(h, w_ref[:, c:c + LANE]).astype(BF16)
    z = _dot(alr, wa_ref[...]) + ba_ref[...]
    log_sig = jnp.minimum(z, 0.0) - jnp.log1p(jnp.exp(-jnp.abs(z)))
    gl_ref[...] = log_sig * (1.0 / GLA_TAU)


def _inproj_a(x, g, w, wa, ba, tm):
    n, d = x.shape
    row = lambda i: (i, 0)
    outs = [
        jax.ShapeDtypeStruct((n, GLA_QK), F32), jax.ShapeDtypeStruct((n, GLA_QK), F32),
        jax.ShapeDtypeStruct((n, GLA_QK), F32), jax.ShapeDtypeStruct((n, GLA_V), BF16),
        jax.ShapeDtypeStruct((n, GLA_V), F32), jax.ShapeDtypeStruct((n, MEM_WIDTH), BF16),
    ]
    return pl.pallas_call(
        _inproj_a_kernel,
        out_shape=outs,
        grid=(n // tm,),
        in_specs=[pl.BlockSpec((tm, d), row), _const_spec((1, d)), _const_spec(w.shape),
                  _const_spec(wa.shape), _const_spec(ba.shape)],
        out_specs=[pl.BlockSpec((tm, s.shape[1]), row) for s in outs],
        compiler_params=_params(("parallel",)),
        name="inproj_a",
    )(x, g, w, wa, ba)


def _gla_constants():
    c = GLA_CHUNK
    w = np.zeros((8 * c, c), np.float32)
    masks = np.zeros((GLA_LEVELS + 1, c, c), np.float32)
    masks[0] = np.eye(c)
    for l in range(1, GLA_LEVELS + 1):
        blk, half = 2 ** l, 2 ** (l - 1)
        for i in range(c):
            mid = (i // blk) * blk + half - 1
            if i % blk >= half:
                w[(l - 1) * c + i, mid + 1:i + 1] = 1.0
            else:
                w[(l - 1) * c + i, i + 1:mid + 1] = 1.0
        for t in range(c):
            for s in range(c):
                if t // blk == s // blk and t % blk >= half and s % blk < half:
                    masks[l, t, s] = 1.0
    for i in range(c):
        w[6 * c + i, :i + 1] = 1.0
        w[7 * c + i, i + 1:] = 1.0
    return np.concatenate([w, w], axis=1), np.tile(masks, (1, 1, GLA_HEADS))


def _gla_kernel(q_ref, k_ref, gl_ref, v_ref, r_ref, gh_ref, wcat_ref, msk_ref, o_ref, st_ref, *, n_chunks):
    c = GLA_CHUNK

    @pl.when(pl.program_id(1) == 0)
    def _():
        st_ref[...] = jnp.zeros_like(st_ref)

    lane_qk = lax.broadcasted_iota(jnp.int32, (1, GLA_QK), 1) // GLA_DK
    lane_v = lax.broadcasted_iota(jnp.int32, (1, GLA_V), 1) // GLA_DV
    row_qk = lax.broadcasted_iota(jnp.int32, (GLA_QK, 1), 0) // GLA_DK
    ones = jnp.ones((2 * c, LANE), BF16)
    wcat = wcat_ref[...]

    def stack_heads(x, lane_head):
        return jnp.concatenate([jnp.where(lane_head == h, x, jnp.zeros_like(x)) for h in range(GLA_HEADS)], axis=0)

    chunks = [slice(ci * c, (ci + 1) * c) for ci in range(n_chunks)]
    es, decs = [], []
    for rows in chunks:
        g = gl_ref[rows, :]
        g_hi = g.astype(BF16)
        g_split = jnp.concatenate([g_hi, (g - g_hi.astype(F32)).astype(BF16)], axis=0)
        es.append(jnp.exp(_dot(wcat, g_split)))
        decs.append(jnp.exp(_dot_tn(g_split, ones)))

    o_intras, q_ins, upds = [], [], []
    for rows, e in zip(chunks, es):
        q = q_ref[rows, :]
        k = k_ref[rows, :]
        v = v_ref[rows, :]
        attn = jnp.zeros((c, GLA_HEADS * c), F32)
        for l in range(GLA_LEVELS + 1):
            if l == 0:
                ql, kl = q, k
            else:
                el = e[(l - 1) * c:l * c]
                ql, kl = q * el, k * el
            attn = attn + _dot_nt(ql.astype(BF16), stack_heads(kl, lane_qk).astype(BF16)) * msk_ref[l]
        o_intras.append(_dot(attn.astype(BF16), stack_heads(v, lane_v)))
        q_ins.append((q * e[6 * c:7 * c]).astype(BF16))
        kk = stack_heads(k * e[7 * c:8 * c], lane_qk).astype(BF16)
        v_rows = jnp.concatenate([v[:, h * GLA_DV:(h + 1) * GLA_DV] for h in range(GLA_HEADS)], axis=0)
        upds.append(_dot_tn(kk, v_rows))

    st = st_ref[...]
    outs = []
    for o_intra, q_in, dec, upd in zip(o_intras, q_ins, decs, upds):
        st_bd = jnp.concatenate([jnp.where(row_qk == h, st, 0.0).astype(BF16) for h in range(GLA_HEADS)], axis=1)
        outs.append(o_intra + _dot(q_in, st_bd))
        st = dec * st + upd
    st_ref[...] = st

    for rows, o in zip(chunks, outs):
        for h in range(GLA_HEADS):
            cs = slice(h * GLA_DV, (h + 1) * GLA_DV)
            on = _normalize(o[:, cs]) * gh_ref[...]
            r = r_ref[rows, cs]
            o_ref[rows, cs] = (on * (r * _sigmoid(r))).astype(BF16)


def _gla(q, k, gl, v, r, g_head, batch, ct):
    n = q.shape[0]
    t = n // batch
    nt = t // ct
    wcat, masks = _gla_constants()
    row = lambda b, i: (b * nt + i, 0)
    return pl.pallas_call(
        functools.partial(_gla_kernel, n_chunks=ct // GLA_CHUNK),
        out_shape=jax.ShapeDtypeStruct((n, GLA_V), BF16),
        grid=(batch, nt),
        in_specs=[pl.BlockSpec((ct, GLA_QK), row), pl.BlockSpec((ct, GLA_QK), row),
                  pl.BlockSpec((ct, GLA_QK), row), pl.BlockSpec((ct, GLA_V), row),
                  pl.BlockSpec((ct, GLA_V), row), _const_spec((1, GLA_DV)),
                  _const_spec(wcat.shape), _const_spec(masks.shape)],
        out_specs=pl.BlockSpec((ct, GLA_V), row),
        scratch_shapes=[pltpu.VMEM((GLA_QK, GLA_DV), F32)],
        compiler_params=_params(("parallel", "arbitrary")),
        name="gla",
    )(q, k, gl, v, r, g_head, jnp.asarray(wcat, BF16), jnp.asarray(masks, F32))


def _attn_out_kernel(o_ref, mq_ref, mk_ref, mv_ref, w_ref, x_ref, out_ref):
    heads = [slice(h * MEM_DH, (h + 1) * MEM_DH) for h in range(MEM_HEADS)]
    scores = [_dot_nt(mq_ref[:, cs], mk_ref[0, :, cs]) * (MEM_DH ** -0.5) for cs in heads]
    parts = [o_ref[...]]
    for s, cs in zip(scores, heads):
        e = jnp.exp(s - jnp.max(s, axis=-1, keepdims=True))
        p = e / jnp.sum(e, axis=-1, keepdims=True)
        parts.append(_dot(p.astype(BF16), mv_ref[0, :, cs]).astype(BF16))
    cat = jnp.concatenate(parts, axis=-1)
    out_ref[...] = x_ref[...] + _dot(cat, w_ref[...])


def _attn_out(o, mq, mkv, w_out, x, batch, tm):
    n, d = x.shape
    t = n // batch
    nt = t // tm
    m = mkv.shape[1]
    row = lambda b, i: (b * nt + i, 0)
    return pl.pallas_call(
        _attn_out_kernel,
        out_shape=jax.ShapeDtypeStruct((n, d), F32),
        grid=(batch, nt),
        in_specs=[pl.BlockSpec((tm, o.shape[1]), row), pl.BlockSpec((tm, MEM_WIDTH), row),
                  pl.BlockSpec((1, m, MEM_WIDTH), lambda b, i: (b, 0, 0)),
                  pl.BlockSpec((1, m, MEM_WIDTH), lambda b, i: (b, 0, 1)),
                  _const_spec(w_out.shape), pl.BlockSpec((tm, d), row)],
        out_specs=pl.BlockSpec((tm, d), row),
        compiler_params=_params(("parallel", "parallel")),
        name="attn_out",
    )(o, mq, mkv, mkv, w_out, x)


def _ffn_kernel(x_ref, g_ref, wup_ref, cw_ref, cb_ref, wd_ref, gf_ref, out_ref, act_ref, tail_ref, *,
                final_norm):
    @pl.when(pl.program_id(1) == 0)
    def _():
        tail_ref[...] = jnp.zeros_like(tail_ref)

    x = x_ref[...]
    tm = x.shape[0]
    h = (_normalize(x) * g_ref[...]).astype(BF16)
    rid = lax.broadcasted_iota(jnp.int32, (8, FFN_CHUNK), 0)
    for j in range(FFN_DIM // FFN_CHUNK):
        cs = slice(j * FFN_CHUNK, (j + 1) * FFN_CHUNK)
        a = _dot(h, wup_ref[:, cs])
        b = _dot(h, wup_ref[:, FFN_DIM + j * FFN_CHUNK:FFN_DIM + (j + 1) * FFN_CHUNK])
        tail = tail_ref[j]
        r1 = pltpu.roll(a, 1, 0)
        r2 = pltpu.roll(a, 2, 0)
        top1 = jnp.where(rid == 0, tail[7:8], r1[:8])
        top2 = jnp.where(rid == 0, tail[6:7], jnp.where(rid == 1, tail[7:8], r2[:8]))
        a1 = jnp.concatenate([top1, r1[8:]], axis=0)
        a2 = jnp.concatenate([top2, r2[8:]], axis=0)
        tail_ref[j] = a[tm - 8:]
        ac = a2 * cw_ref[0:1, cs] + a1 * cw_ref[1:2, cs] + a * cw_ref[2:3, cs] + cb_ref[:, cs]
        act_ref[:, cs] = (ac * _sigmoid(ac) * b).astype(BF16)
    y = x + _dot(act_ref[...], wd_ref[...])
    if final_norm:
        y = _normalize(y) * gf_ref[...]
    out_ref[...] = y


def _ffn(x, g, w_up, conv_w, conv_b, w_down, g_final, batch, tm, final_norm):
    n, d = x.shape
    nt = n // batch // tm
    row = lambda b, i: (b * nt + i, 0)
    once = pl.Buffered(1)
    return pl.pallas_call(
        functools.partial(_ffn_kernel, final_norm=final_norm),
        out_shape=jax.ShapeDtypeStruct((n, d), F32),
        grid=(batch, nt),
        in_specs=[pl.BlockSpec((tm, d), row), _const_spec((1, d)),
                  pl.BlockSpec(w_up.shape, lambda b, i: (0, 0), pipeline_mode=once),
                  _const_spec(conv_w.shape), _const_spec(conv_b.shape),
                  pl.BlockSpec(w_down.shape, lambda b, i: (0, 0), pipeline_mode=once),
                  _const_spec((1, d))],
        out_specs=pl.BlockSpec((tm, d), row),
        scratch_shapes=[pltpu.VMEM((tm, FFN_DIM), BF16),
                        pltpu.VMEM((FFN_DIM // FFN_CHUNK, 8, FFN_CHUNK), F32)],
        compiler_params=_params(("parallel", "arbitrary")),
        name="conv_ffn",
    )(x, g, w_up, conv_w, conv_b, w_down, g_final)


def _proj_b_kernel(x_ref, gkv_ref, gmix_ref, wkv_ref, wqt_ref, wgt_ref, wmq_ref,
                   cmp_ref, kvs_ref, qt_ref, gt_ref, mq_ref):
    xn = _normalize(x_ref[...])
    hkv = (xn * gkv_ref[...]).astype(BF16)
    h1 = (xn * gmix_ref[...]).astype(BF16)
    cmp_ref[0] = _dot(hkv, wkv_ref[:, 0:KV_WIDTH]).astype(BF16)
    cmp_ref[1] = _dot(hkv, wkv_ref[:, KV_WIDTH:2 * KV_WIDTH]).astype(BF16)
    kvs_ref[...] = _dot(hkv, wkv_ref[:, 2 * KV_WIDTH:]).astype(BF16)
    qt_ref[0] = (_dot_nt(wqt_ref[...], h1) * (NSA_DH ** -0.5)).astype(BF16)
    gt_ref[0] = _sigmoid(_dot_nt(wgt_ref[...], h1))
    mq_ref[...] = _dot(h1, wmq_ref[...]).astype(BF16)


def _proj_b(x, g_kv, g_mix, w_kv, wqt, wgt, wmq, batch, tm):
    n, d = x.shape
    t = n // batch
    nt = t // tm
    row = lambda b, i: (b * nt + i, 0)
    outs = [
        jax.ShapeDtypeStruct((2, n, KV_WIDTH), BF16),
        jax.ShapeDtypeStruct((n, 4 * KV_WIDTH), BF16),
        jax.ShapeDtypeStruct((batch, NSA_WIDTH, t), BF16),
        jax.ShapeDtypeStruct((batch, wgt.shape[0], t), F32),
        jax.ShapeDtypeStruct((n, MEM_WIDTH), BF16),
    ]
    return pl.pallas_call(
        _proj_b_kernel,
        out_shape=outs,
        grid=(batch, nt),
        in_specs=[pl.BlockSpec((tm, d), row), _const_spec((1, d)), _const_spec((1, d)),
                  _const_spec(w_kv.shape), _const_spec(wqt.shape), _const_spec(wgt.shape),
                  _const_spec(wmq.shape)],
        out_specs=[pl.BlockSpec((2, tm, KV_WIDTH), lambda b, i: (0, b * nt + i, 0)),
                   pl.BlockSpec((tm, 4 * KV_WIDTH), row),
                   pl.BlockSpec((1, NSA_WIDTH, tm), lambda b, i: (b, 0, i)),
                   pl.BlockSpec((1, wgt.shape[0], tm), lambda b, i: (b, 0, i)),
                   pl.BlockSpec((tm, MEM_WIDTH), row)],
        compiler_params=_params(("parallel", "parallel")),
        name="proj_b",
    )(x, g_kv, g_mix, w_kv, wqt, wgt, wmq)


def _compress_kernel(a_ref, w1_ref, pe_ref, w2_ref, o_ref):
    a = a_ref[0, 0]
    w1 = w1_ref[0]
    u = _dot(a, w1)
    cpe = _dot(pe_ref[0], w1)
    nrow = a.shape[0]
    hid = []
    for g in range(NSA_GROUPS):
        c0 = slice((2 * g) * CMP_HIDDEN, (2 * g + 1) * CMP_HIDDEN)
        c1 = slice((2 * g + 1) * CMP_HIDDEN, (2 * g + 2) * CMP_HIDDEN)
        nxt = pltpu.roll(u[:, c1], nrow - 1, 0)
        hid.append(u[:, c0] + nxt + cpe[0:1, c0] + cpe[8:9, c1])
    hcat = jax.nn.gelu(jnp.concatenate(hid, axis=-1), approximate=True).astype(BF16)
    o_ref[0, 0] = _dot(hcat, w2_ref[0]).astype(BF16)


def _compress(a, w1, pe, w2):
    _, batch, nsub, width = a.shape
    return pl.pallas_call(
        _compress_kernel,
        out_shape=jax.ShapeDtypeStruct((2, batch, nsub, KV_WIDTH), BF16),
        grid=(2, batch),
        in_specs=[pl.BlockSpec((1, 1, nsub, width), lambda s, b: (s, b, 0, 0)),
                  pl.BlockSpec((1,) + w1.shape[1:], lambda s, b: (s, 0, 0)),
                  pl.BlockSpec((1,) + pe.shape[1:], lambda s, b: (s, 0, 0)),
                  pl.BlockSpec((1,) + w2.shape[1:], lambda s, b: (s, 0, 0))],
        out_specs=pl.BlockSpec((1, 1, nsub, KV_WIDTH), lambda s, b: (s, b, 0, 0)),
        compiler_params=_params(("parallel", "parallel")),
        name="compress",
    )(a, w1, pe, w2)


def _alibi_tiles(ncmp_pad):
    slopes = 2.0 ** (-8.0 * np.arange(1, NSA_HEADS + 1) / NSA_HEADS)
    tl = np.tile(np.arange(Q_BLOCK), NSA_HPG)[None, :]
    cmp_end = (np.arange(ncmp_pad) * CMP_STRIDE + CMP_BLOCK - 1)[:, None]
    key = np.arange(SEL_BLOCK)[:, None]
    dist_win = tl + WINDOW - np.arange(WIN_KEYS)[:, None]
    c_cmp = np.zeros((NSA_GROUPS, ncmp_pad, NSA_HPG * Q_BLOCK), np.float32)
    c_blk = np.zeros((NSA_GROUPS, SEL_BLOCK, NSA_HPG * Q_BLOCK), np.float32)
    c_win = np.zeros((NSA_GROUPS, WIN_KEYS, NSA_HPG * Q_BLOCK), np.float32)
    srow = np.zeros((NSA_GROUPS, 1, NSA_HPG * Q_BLOCK), np.float32)
    for g in range(NSA_GROUPS):
        s = np.repeat(slopes[g * NSA_HPG:(g + 1) * NSA_HPG], Q_BLOCK)[None, :]
        c_cmp[g] = -s * (tl - cmp_end)
        c_blk[g] = -s * (tl - key)
        c_win[g] = np.where((dist_win >= 0) & (dist_win < WINDOW), -s * dist_win, -np.inf)
        srow[g] = s
    return c_cmp, c_blk, c_win, srow


def _group_queries(qt_ref, g):
    zeros = jnp.zeros((NSA_DH, Q_BLOCK), BF16)
    cols = []
    for hh in range(NSA_HPG):
        h = g * NSA_HPG + hh
        piece = qt_ref[0, h * NSA_DH:(h + 1) * NSA_DH, :]
        cols.append(jnp.concatenate([piece, zeros] if g == 0 else [zeros, piece], axis=0))
    return jnp.concatenate(cols, axis=1)


def _nsa_a_kernel(qt_ref, ckv_ref, ccmp_ref, srow_ref, ovt_ref, oc_ref, sel_ref, cnt_ref, imp_ref):
    qb = pl.program_id(1)
    base = (qb * Q_BLOCK).astype(F32)
    nblk = sel_ref.shape[3]
    jidx = lax.broadcasted_iota(jnp.int32, (nblk, Q_BLOCK), 0)
    tl = lax.broadcasted_iota(jnp.int32, (nblk, Q_BLOCK), 1)
    cur = 2 * qb + jnp.where(tl >= SEL_BLOCK, 1, 0)
    valid_blk = jidx <= cur
    n_chunks = (qb * (Q_BLOCK // CMP_STRIDE) + (Q_BLOCK - CMP_BLOCK) // CMP_STRIDE) // CMP_CHUNK + 1
    groups = range(NSA_GROUPS)
    qts = [_group_queries(qt_ref, g) for g in groups]
    shifts = [srow_ref[g] * base for g in groups]

    def cmp_branch(n_rows):
        kc = ckv_ref[0, 0, 0:n_rows, :]
        vc = ckv_ref[1, 0, 0:n_rows, :]
        ov = ovt_ref[:, 0:n_rows]
        ov2 = jnp.concatenate([ov, ov], axis=1)
        raws = [_dot(kc, qts[g]) for g in groups]
        for g in groups:
            bias = ccmp_ref[g, 0:n_rows, :] - shifts[g]
            s = raws[g] + jnp.where(bias <= 0.0, bias, NEG_INF)
            m = jnp.max(s, axis=0, keepdims=True)
            e = jnp.exp(s - jnp.where(m == NEG_INF, 0.0, m))
            inv = 1.0 / jnp.maximum(jnp.sum(e, axis=0, keepdims=True), 1e-30)
            e1 = e.astype(BF16)
            e2 = (e - e1.astype(F32)).astype(BF16)
            oc = _dot_tn(vc, e1)[g * NSA_DH:(g + 1) * NSA_DH] * inv
            imp = _dot(ov2, jnp.concatenate([e1, e2], axis=0)) * inv
            for hh in range(NSA_HPG):
                h = g * NSA_HPG + hh
                oc_ref[0, h * NSA_DH:(h + 1) * NSA_DH, :] = oc[:, hh * Q_BLOCK:(hh + 1) * Q_BLOCK]
            imp_g = imp[:, 0:Q_BLOCK]
            for hh in range(1, NSA_HPG):
                imp_g = imp_g + imp[:, hh * Q_BLOCK:(hh + 1) * Q_BLOCK]
            imp_ref[g] = imp_g

    for nc in range(1, ckv_ref.shape[2] // CMP_CHUNK + 1):
        pl.when(n_chunks == nc)(functools.partial(cmp_branch, nc * CMP_CHUNK))
    imps = [imp_ref[g] for g in groups]

    def emit(g, sel):
        sel_ref[0, g, 0] = sel
        cnt_ref[0, g, 0] = _dot_nt(jnp.ones((8, Q_BLOCK), BF16), sel.astype(BF16))

    @pl.when(qb < RANK_FIRST_TILE)
    def _():
        for g in groups:
            emit(g, jnp.where(valid_blk, 1.0, 0.0))

    @pl.when(qb >= RANK_FIRST_TILE)
    def _():
        jf = jidx.astype(F32)
        cand = (jidx >= 1) & (jidx <= cur - 2)
        v0 = [jnp.where(cand, imps[g], NEG_INF) for g in groups]
        v = list(v0)
        for _ in range(SEL_TOPN - 3):
            for g in groups:
                mx = jnp.max(v[g], axis=0, keepdims=True)
                first = jnp.min(jnp.where(v[g] == mx, jf, float(nblk)), axis=0, keepdims=True)
                v[g] = jnp.where(jf == first, NEG_INF, v[g])
        always = (jidx == 0) | (jidx == cur) | (jidx == cur - 1)
        for g in groups:
            emit(g, jnp.where((v[g] != v0[g]) | always, 1.0, 0.0))


def _nsa_a(qt, ckv, c_cmp, srow, ovt):
    batch, _, t = qt.shape
    nqb = t // Q_BLOCK
    ns = t // SEL_BLOCK
    ncmp = ckv.shape[2]
    outs = [jax.ShapeDtypeStruct((batch, NSA_WIDTH, t), F32),
            jax.ShapeDtypeStruct((batch, NSA_GROUPS, nqb, ns, Q_BLOCK), F32),
            jax.ShapeDtypeStruct((batch, NSA_GROUPS, nqb, 8, ns), F32)]
    return pl.pallas_call(
        _nsa_a_kernel,
        out_shape=outs,
        grid=(batch, nqb),
        in_specs=[pl.BlockSpec((1, NSA_WIDTH, Q_BLOCK), lambda b, i: (b, 0, i)),
                  pl.BlockSpec((2, 1, ncmp, KV_WIDTH), lambda b, i: (0, b, 0, 0)),
                  _const_spec(c_cmp.shape), _const_spec(srow.shape), _const_spec(ovt.shape)],
        out_specs=[pl.BlockSpec((1, NSA_WIDTH, Q_BLOCK), lambda b, i: (b, 0, i)),
                   pl.BlockSpec((1, NSA_GROUPS, 1, ns, Q_BLOCK), lambda b, i: (b, 0, i, 0, 0)),
                   pl.BlockSpec((1, NSA_GROUPS, 1, 8, ns), lambda b, i: (b, 0, i, 0, 0))],
        scratch_shapes=[pltpu.VMEM((NSA_GROUPS, ns, Q_BLOCK), F32)],
        compiler_params=_params(("parallel", "parallel")),
        name="nsa_select",
    )(qt, ckv, c_cmp, srow, ovt)


def _block_softmax_step(carry, raw, consts, shifts, v_rows, rows_g):
    m, l, acc = carry
    us = [raw[i * SEL_BLOCK:(i + 1) * SEL_BLOCK] + c for i, c in enumerate(consts)]
    m_new = m
    for u, sh in zip(us, shifts):
        m_new = jnp.maximum(m_new, jnp.max(u, axis=0, keepdims=True) + sh)
    m_safe = jnp.where(m_new == NEG_INF, 0.0, m_new)
    alpha = jnp.where(m == NEG_INF, 0.0, jnp.exp(m - m_safe))
    ps = [jnp.exp(u - (m_safe - sh)) for u, sh in zip(us, shifts)]
    l = alpha * l
    for p in ps:
        l = l + jnp.sum(p, axis=0, keepdims=True)
    pv = _dot_tn(v_rows, jnp.concatenate(ps, axis=0).astype(BF16))
    return m_new, l, alpha * acc + pv[rows_g]


def _nsa_b_kernel(idx_ref, qt_ref, ks_ref, vs_ref, kw_ref, vw_ref, sel_ref, oc_ref, gt_ref, cblk_ref, cwin_ref,
                  srow_ref, o_ref, *, idx_words):
    b = pl.program_id(0)
    qb = pl.program_id(1)
    base = qb * Q_BLOCK
    width = NSA_HPG * Q_BLOCK
    groups = range(NSA_GROUPS)
    qts = [_group_queries(qt_ref, g) for g in groups]
    srows = [srow_ref[g] for g in groups]
    rows = [slice(g * NSA_DH, (g + 1) * NSA_DH) for g in groups]
    entries = [((b * NSA_GROUPS + g) * pl.num_programs(1) + qb) * idx_words for g in groups]
    n_act = [idx_ref[entries[g] + idx_words - 1] for g in groups]
    init = (jnp.full((1, width), NEG_INF, F32), jnp.zeros((1, width), F32), jnp.zeros((NSA_DH, width), F32))
    zero_row = jnp.zeros((1, width), F32)

    def gate_row(g, j, live):
        picked = sel_ref[0, g, 0, pl.ds(j, 1), :]
        gate = jnp.where((picked > 0.0) & live, 0.0, NEG_INF)
        return jnp.concatenate([gate] * NSA_HPG, axis=1)

    def stage(it):
        out = []
        for g in groups:
            word = idx_ref[entries[g] + it]
            kts, vts, shifts = [], [], []
            for i in range(SLC_CHUNK):
                live = it * SLC_CHUNK + i < n_act[g]
                j = jnp.where(live, lax.shift_right_logical(word, 8 * i) & 0xFF, 0)
                k0 = pl.multiple_of(j * SEL_BLOCK, SEL_BLOCK)
                kts.append(ks_ref[pl.ds(k0, SEL_BLOCK), :])
                vts.append(vs_ref[pl.ds(k0, SEL_BLOCK), :])
                shifts.append(gate_row(g, j, live) - srows[g] * (base - k0).astype(F32))
            out.append((_dot(jnp.concatenate(kts, axis=0), qts[g]), tuple(shifts), jnp.concatenate(vts, axis=0)))
        return tuple(out)

    def body(it, states):
        staged = stage(it)
        return tuple(_block_softmax_step(states[g], staged[g][0], [cblk_ref[g]] * SLC_CHUNK, staged[g][1],
                                         staged[g][2], rows[g]) for g in groups)

    n_it = (jnp.maximum(n_act[0], n_act[1]) + SLC_CHUNK - 1) // SLC_CHUNK
    slc = list(lax.fori_loop(0, n_it, body, (init,) * NSA_GROUPS))

    diag = range(WINDOW // SEL_BLOCK, WIN_KEYS // SEL_BLOCK)
    r0 = pl.multiple_of(base, Q_BLOCK)
    ks_d = ks_ref[pl.ds(r0, Q_BLOCK), :]
    vs_d = vs_ref[pl.ds(r0, Q_BLOCK), :]
    raw_d = [_dot(ks_d, qts[g]) for g in groups]
    n_win = WIN_KEYS // SEL_BLOCK
    win_staged = []
    for c in range(WIN_KEYS // WIN_CHUNK):
        blocks = range(c * n_win // 2, (c + 1) * n_win // 2)
        kws, vws, offs = [], [], []
        for r in blocks:
            j = 2 * qb - WINDOW // SEL_BLOCK + r
            k0 = pl.multiple_of(jnp.maximum(j, 0) * SEL_BLOCK, SEL_BLOCK)
            kws.append(kw_ref[pl.ds(k0, SEL_BLOCK), :])
            vws.append(vw_ref[pl.ds(k0, SEL_BLOCK), :])
            offs.append(zero_row + jnp.where(j >= 0, 0.0, NEG_INF))
        kw = jnp.concatenate(kws, axis=0)
        win_staged.append((blocks, [_dot(kw, qts[g]) for g in groups], offs, jnp.concatenate(vws, axis=0)))

    for g in groups:
        consts = [cwin_ref[g, r * SEL_BLOCK:(r + 1) * SEL_BLOCK, :] for r in diag]
        shifts = [gate_row(g, 2 * qb + i, True) for i in range(len(diag))]
        slc[g] = _block_softmax_step(slc[g], raw_d[g], consts, shifts, vs_d, rows[g])
    win = [init] * NSA_GROUPS
    for blocks, raws, offs, vw in win_staged:
        for g in groups:
            consts = [cwin_ref[g, r * SEL_BLOCK:(r + 1) * SEL_BLOCK, :] for r in blocks]
            win[g] = _block_softmax_step(win[g], raws[g], consts, offs, vw, rows[g])

    heads_out = []
    for g in groups:
        o_s = slc[g][2] * (1.0 / jnp.maximum(slc[g][1], 1e-30))
        o_w = win[g][2] * (1.0 / jnp.maximum(win[g][1], 1e-30))
        for hh in range(NSA_HPG):
            h = g * NSA_HPG + hh
            cs = slice(hh * Q_BLOCK, (hh + 1) * Q_BLOCK)
            gates = gt_ref[0, 3 * h:3 * h + 3, :]
            heads_out.append(gates[0:1] * oc_ref[0, h * NSA_DH:(h + 1) * NSA_DH, :]
                             + gates[1:2] * o_s[:, cs] + gates[2:3] * o_w[:, cs])
    o_ref[...] = jnp.concatenate(heads_out, axis=0).T.astype(BF16)


def _nsa_b(idx, qt, kvs, sel, oc, gt, c_blk, c_win, srow):
    batch, _, t = qt.shape
    nqb = t // Q_BLOCK
    ns = t // SEL_BLOCK
    kv_spec = lambda c: pl.BlockSpec((t, KV_WIDTH), lambda b, i, idx_ref: (b, c))
    grid_spec = pltpu.PrefetchScalarGridSpec(
        num_scalar_prefetch=1,
        grid=(batch, nqb),
        in_specs=[pl.BlockSpec((1, NSA_WIDTH, Q_BLOCK), lambda b, i, r: (b, 0, i)),
                  kv_spec(0), kv_spec(1), kv_spec(2), kv_spec(3),
                  pl.BlockSpec((1, NSA_GROUPS, 1, ns, Q_BLOCK), lambda b, i, r: (b, 0, i, 0, 0)),
                  pl.BlockSpec((1, NSA_WIDTH, Q_BLOCK), lambda b, i, r: (b, 0, i)),
                  pl.BlockSpec((1, gt.shape[1], Q_BLOCK), lambda b, i, r: (b, 0, i)),
                  pl.BlockSpec(c_blk.shape, lambda b, i, r: (0, 0, 0)),
                  pl.BlockSpec(c_win.shape, lambda b, i, r: (0, 0, 0)),
                  pl.BlockSpec(srow.shape, lambda b, i, r: (0, 0, 0))],
        out_specs=pl.BlockSpec((Q_BLOCK, NSA_WIDTH), lambda b, i, r: (b * nqb + i, 0)),
    )
    return pl.pallas_call(
        functools.partial(_nsa_b_kernel, idx_words=ns // 4 + 1),
        out_shape=jax.ShapeDtypeStruct((batch * t, NSA_WIDTH), BF16),
        grid_spec=grid_spec,
        compiler_params=_params(("parallel", "parallel")),
        name="nsa_attend",
    )(idx, qt, kvs, kvs, kvs, kvs, sel, oc, gt, c_blk, c_win, srow)


def _pad_cols(w, width):
    return jnp.pad(w, ((0, 0), (0, width - w.shape[1])))


def _compress_weights(w1, pe, w2):
    nsub = CMP_STRIDE
    w1r = w1.reshape(2, nsub, NSA_DH, CMP_HIDDEN)
    same_group = jnp.eye(NSA_GROUPS, dtype=F32)
    big = same_group[None, :, None, :, None, None] * w1r.transpose(1, 2, 0, 3)[:, None, :, None, :, :]
    big = big.reshape(nsub * KV_WIDTH, NSA_GROUPS * 2 * CMP_HIDDEN)
    per = pe.reshape(2, nsub, 1, NSA_DH)
    pe2 = jnp.broadcast_to(per, (2, nsub, NSA_GROUPS, NSA_DH)).reshape(2, 1, nsub * KV_WIDTH)
    pe2 = jnp.broadcast_to(pe2, (2, 8, nsub * KV_WIDTH)).reshape(16, nsub * KV_WIDTH)
    w2bd = same_group[:, None, :, None] * w2[None, :, None, :]
    return big.astype(BF16), pe2.astype(BF16), w2bd.reshape(NSA_GROUPS * CMP_HIDDEN, KV_WIDTH).astype(BF16)


def _selection_overlap_t(ncmp_pad, ns):
    cs = np.arange(ncmp_pad) * CMP_STRIDE
    ss = np.arange(ns) * SEL_BLOCK
    ov = np.minimum(cs[:, None] + CMP_BLOCK, ss[None, :] + SEL_BLOCK) - np.maximum(cs[:, None], ss[None, :])
    return (np.clip(ov, 0, None).astype(np.float32) / CMP_BLOCK).T


def _active_blocks(cnt):
    ns = cnt.shape[-1]
    before_tile = jnp.arange(ns)[None, :] < (Q_BLOCK // SEL_BLOCK) * jnp.arange(cnt.shape[2])[:, None]
    flags = (cnt[:, :, :, 0, :] > 0.0) & before_tile
    order = jnp.argsort(jnp.where(flags, 0, 1).astype(jnp.int32), axis=-1, stable=True).astype(jnp.int32)
    packed = jnp.sum(order.reshape(order.shape[:-1] + (ns // 4, 4)) << (8 * jnp.arange(4, dtype=jnp.int32)), axis=-1)
    n_act = jnp.sum(flags, axis=-1, dtype=jnp.int32)
    return jnp.concatenate([packed, n_act[..., None]], axis=-1).reshape(-1)


def kernel(x, mem, g_mix, g_ffn, g_mem, w_mem_kv, w_up, conv_w, conv_b, w_down,
           a_w_in, a_w_alpha, a_b_alpha, a_g_head, a_w_out,
           g_kv, w_kv, pe_k, pe_v, w_ck1, w_ck2, w_cv1, w_cv2,
           b_w_in, b_w_out, g_final):
    batch, t, d = x.shape
    n = batch * t
    m = mem.shape[1]
    tm = min(512, t)
    xf = x.reshape(n, d)
    memf = mem.reshape(batch * m, d)
    row = lambda v: v.reshape(1, -1)

    mkv0 = _rms_proj(memf, row(g_mem[0]), w_mem_kv[0].astype(BF16), m, BF16).reshape(batch, m, 2 * MEM_WIDTH)
    wa = a_w_in[0]
    c_alr = 2 * GLA_QK + 2 * GLA_V
    w_a = jnp.concatenate([wa[:, :c_alr], wa[:, c_alr + GLA_RANK:], _pad_cols(wa[:, c_alr:c_alr + GLA_RANK], LANE)],
                          axis=1).astype(BF16)
    w_alpha = jnp.pad(a_w_alpha[0], ((0, LANE - GLA_RANK), (0, 0))).astype(BF16)
    q, k, gl, v, r, mq = _inproj_a(xf, row(g_mix[0]), w_a, w_alpha, row(a_b_alpha[0]), tm)
    o = _gla(q, k, gl, v, r, row(a_g_head[0]), batch, tm)
    xf = _attn_out(o, mq, mkv0, a_w_out[0].astype(BF16), xf, batch, tm)
    xf = _ffn(xf, row(g_ffn[0]), w_up[0].astype(BF16), conv_w[0], row(conv_b[0]), w_down[0].astype(BF16),
              row(g_final), batch, tm, final_norm=False)

    wb = b_w_in[0]
    wqt = wb[:, :NSA_WIDTH].T.astype(BF16)
    n_gate = 3 * NSA_HEADS
    wgt = jnp.pad(wb[:, NSA_WIDTH:NSA_WIDTH + n_gate].T, ((0, 32 - n_gate), (0, 0))).astype(BF16)
    wmq = wb[:, NSA_WIDTH + n_gate:].astype(BF16)
    ckv_in, kvs, qt, gt, mq1 = _proj_b(xf, row(g_kv), row(g_mix[1]), w_kv.astype(BF16), wqt, wgt, wmq, batch, tm)
    nsub = t // CMP_STRIDE
    w1k, pek, w2k = _compress_weights(w_ck1, pe_k, w_ck2)
    w1v, pev, w2v = _compress_weights(w_cv1, pe_v, w_cv2)
    ckv = _compress(ckv_in.reshape(2, batch, nsub, CMP_STRIDE * KV_WIDTH),
                    jnp.stack([w1k, w1v]), jnp.stack([pek, pev]), jnp.stack([w2k, w2v]))

    ns = t // SEL_BLOCK
    c_cmp, c_blk, c_win, srow = _alibi_tiles(nsub)
    ovt = jnp.asarray(_selection_overlap_t(nsub, ns), BF16)
    oc, sel, cnt = _nsa_a(qt, ckv, jnp.asarray(c_cmp), jnp.asarray(srow), ovt)
    idx = _active_blocks(cnt)
    o1 = _nsa_b(idx, qt, kvs, sel, oc, gt, jnp.asarray(c_blk), jnp.asarray(c_win), jnp.asarray(srow))
    mkv1 = _rms_proj(memf, row(g_mem[1]), w_mem_kv[1].astype(BF16), m, BF16).reshape(batch, m, 2 * MEM_WIDTH)
    xf = _attn_out(o1, mq1, mkv1, b_w_out[0].astype(BF16), xf, batch, tm)
    xf = _ffn(xf, row(g_ffn[1]), w_up[1].astype(BF16), conv_w[1], row(conv_b[1]), w_down[1].astype(BF16),
              row(g_final), batch, tm, final_norm=True)
    return xf.reshape(batch, t, d)
```

```python
import functools

import numpy as np
import jax
import jax.numpy as jnp
from jax import lax
from jax.experimental import pallas as pl
from jax.experimental.pallas import tpu as pltpu

F32 = jnp.float32
BF16 = jnp.bfloat16
EPS = 1e-6
NEG_INF = float("-inf")
LOG2E = 1.4426950408889634

V7X_VMEM_BYTES = 64 * 1024 * 1024
VMEM_LIMIT = V7X_VMEM_BYTES - 8 * 1024 * 1024

D_MODEL = 1024
MEM_HEADS = 4
MEM_DH = 128
MEM_WIDTH = MEM_HEADS * MEM_DH
GLA_HEADS = 4
GLA_DK = 64
GLA_DV = 128
GLA_QK = GLA_HEADS * GLA_DK
GLA_V = GLA_HEADS * GLA_DV
GLA_RANK = 16
GLA_TAU = 16.0
GLA_CHUNK = 64
GLA_LEVELS = 6
NSA_HEADS = 8
NSA_GROUPS = 2
NSA_HPG = NSA_HEADS // NSA_GROUPS
NSA_DH = 64
NSA_WIDTH = NSA_HEADS * NSA_DH
KV_WIDTH = NSA_GROUPS * NSA_DH
CMP_BLOCK = 32
CMP_STRIDE = 16
CMP_HIDDEN = 256
SEL_BLOCK = 64
SEL_TOPN = 16
WINDOW = 512
Q_BLOCK = 128
WIN_KEYS = WINDOW + Q_BLOCK
WIN_CHUNK = WIN_KEYS // 2
CMP_CHUNK = 128
SLC_CHUNK = 8
RANK_FIRST_TILE = SEL_TOPN * SEL_BLOCK // Q_BLOCK
FFN_DIM = 2816
FFN_CHUNK = 256
LANE = 128


def _dot(a, b):
    return jnp.dot(a, b, preferred_element_type=F32)


def _dot_nt(a, b):
    return lax.dot_general(a, b, (((1,), (1,)), ((), ())), preferred_element_type=F32)


def _dot_tn(a, b):
    return lax.dot_general(a, b, (((0,), (0,)), ((), ())), preferred_element_type=F32)


def _params(sem):
    return pltpu.CompilerParams(dimension_semantics=sem, vmem_limit_bytes=VMEM_LIMIT)


def _const_spec(shape):
    n = len(shape)
    return pl.BlockSpec(shape, lambda *_: (0,) * n)


def _normalize(x):
    return x * lax.rsqrt(jnp.mean(x * x, axis=-1, keepdims=True) + EPS)


def _sigmoid(x):
    return 1.0 / (1.0 + jnp.exp(-x))


def _rms_proj_kernel(x_ref, g_ref, w_ref, o_ref):
    h = (_normalize(x_ref[...]) * g_ref[...]).astype(BF16)
    o_ref[...] = _dot(h, w_ref[...]).astype(o_ref.dtype)


def _rms_proj(x, g, w, tm, out_dtype):
    n, d = x.shape
    p = w.shape[1]
    return pl.pallas_call(
        _rms_proj_kernel,
        out_shape=jax.ShapeDtypeStruct((n, p), out_dtype),
        grid=(n // tm,),
        in_specs=[pl.BlockSpec((tm, d), lambda i: (i, 0)), _const_spec((1, d)), _const_spec((d, p))],
        out_specs=pl.BlockSpec((tm, p), lambda i: (i, 0)),
        compiler_params=_params(("parallel",)),
        name="rms_proj",
    )(x, g, w)


def _inproj_a_kernel(x_ref, g_ref, w_ref, wa_ref, ba_ref, q_ref, k_ref, gl_ref, v_ref, r_ref, mq_ref):
    h = (_normalize(x_ref[...]) * g_ref[...]).astype(BF16)
    c = 0
    q_ref[...] = _dot(h, w_ref[:, c:c + GLA_QK]) * (GLA_DK ** -0.5)
    c += GLA_QK
    k_ref[...] = _dot(h, w_ref[:, c:c + GLA_QK])
    c += GLA_QK
    v_ref[...] = _dot(h, w_ref[:, c:c + GLA_V]).astype(BF16)
    c += GLA_V
    r_ref[...] = _dot(h, w_ref[:, c:c + GLA_V])
    c += GLA_V
    mq_ref[...] = _dot(h, w_ref[:, c:c + MEM_WIDTH]).astype(BF16)
    c += MEM_WIDTH
    alr = _dot(h, w_ref[:, c:c + LANE]).astype(BF16)
    z = _dot(alr, wa_ref[...]) + ba_ref[...]
    log_sig = jnp.minimum(z, 0.0) - jnp.log1p(jnp.exp(-jnp.abs(z)))
    gl_ref[...] = log_sig * (1.0 / GLA_TAU)


def _inproj_a(x, g, w, wa, ba, tm):
    n, d = x.shape
    row = lambda i: (i, 0)
    outs = [
        jax.ShapeDtypeStruct((n, GLA_QK), F32), jax.ShapeDtypeStruct((n, GLA_QK), F32),
        jax.ShapeDtypeStruct((n, GLA_QK), F32), jax.ShapeDtypeStruct((n, GLA_V), BF16),
        jax.ShapeDtypeStruct((n, GLA_V), F32), jax.ShapeDtypeStruct((n, MEM_WIDTH), BF16),
    ]
    return pl.pallas_call(
        _inproj_a_kernel,
        out_shape=outs,
        grid=(n // tm,),
        in_specs=[pl.BlockSpec((tm, d), row), _const_spec((1, d)), _const_spec(w.shape),
                  _const_spec(wa.shape), _const_spec(ba.shape)],
        out_specs=[pl.BlockSpec((tm, s.shape[1]), row) for s in outs],
        compiler_params=_params(("parallel",)),
        name="inproj_a",
    )(x, g, w, wa, ba)


def _gla_constants():
    c = GLA_CHUNK
    w = np.zeros((8 * c, c), np.float32)
    masks = np.zeros((GLA_LEVELS + 1, c, c), np.float32)
    masks[0] = np.eye(c)
    for l in range(1, GLA_LEVELS + 1):
        blk, half = 2 ** l, 2 ** (l - 1)
        for i in range(c):
            mid = (i // blk) * blk + half - 1
            if i % blk >= half:
                w[(l - 1) * c + i, mid + 1:i + 1] = 1.0
            else:
                w[(l - 1) * c + i, i + 1:mid + 1] = 1.0
        for t in range(c):
            for s in range(c):
                if t // blk == s // blk and t % blk >= half and s % blk < half:
                    masks[l, t, s] = 1.0
    for i in range(c):
        w[6 * c + i, :i + 1] = 1.0
        w[7 * c + i, i + 1:] = 1.0
    return np.concatenate([w, w], axis=1), np.tile(masks, (1, 1, GLA_HEADS))


def _gla_kernel(q_ref, k_ref, gl_ref, v_ref, r_ref, gh_ref, wcat_ref, msk_ref, o_ref, st_ref, *, n_chunks):
    c = GLA_CHUNK

    @pl.when(pl.program_id(1) == 0)
    def _():
        st_ref[...] = jnp.zeros_like(st_ref)

    lane_qk = lax.broadcasted_iota(jnp.int32, (1, GLA_QK), 1) // GLA_DK
    lane_v = lax.broadcasted_iota(jnp.int32, (1, GLA_V), 1) // GLA_DV
    row_qk = lax.broadcasted_iota(jnp.int32, (GLA_QK, 1), 0) // GLA_DK
    ones = jnp.ones((2 * c, LANE), BF16)
    wcat = wcat_ref[...]

    def stack_heads(x, lane_head):
        return jnp.concatenate([jnp.where(lane_head == h, x, jnp.zeros_like(x)) for h in range(GLA_HEADS)], axis=0)

    chunks = [slice(ci * c, (ci + 1) * c) for ci in range(n_chunks)]
    es, decs = [], []
    for rows in chunks:
        g = gl_ref[rows, :] * LOG2E
        g_hi = g.astype(BF16)
        g_split = jnp.concatenate([g_hi, (g - g_hi.astype(F32)).astype(BF16)], axis=0)
        es.append(jnp.exp2(_dot(wcat, g_split)))
        decs.append(jnp.exp2(_dot_tn(g_split, ones)))

    o_intras, q_ins, upds = [], [], []
    for rows, e in zip(chunks, es):
        q = q_ref[rows, :]
        k = k_ref[rows, :]
        v = v_ref[rows, :]
        attn = jnp.zeros((c, GLA_HEADS * c), F32)
        for l in range(GLA_LEVELS + 1):
            if l == 0:
                ql, kl = q, k
            else:
                el = e[(l - 1) * c:l * c]
                ql, kl = q * el, k * el
            attn = attn + _dot_nt(ql.astype(BF16), stack_heads(kl, lane_qk).astype(BF16)) * msk_ref[l]
        o_intras.append(_dot(attn.astype(BF16), stack_heads(v, lane_v)))
        q_ins.append((q * e[6 * c:7 * c]).astype(BF16))
        kk = stack_heads(k * e[7 * c:8 * c], lane_qk).astype(BF16)
        v_rows = jnp.concatenate([v[:, h * GLA_DV:(h + 1) * GLA_DV] for h in range(GLA_HEADS)], axis=0)
        upds.append(_dot_tn(kk, v_rows))

    st = st_ref[...]
    outs = []
    for o_intra, q_in, dec, upd in zip(o_intras, q_ins, decs, upds):
        st_bd = jnp.concatenate([jnp.where(row_qk == h, st, 0.0).astype(BF16) for h in range(GLA_HEADS)], axis=1)
        outs.append(o_intra + _dot(q_in, st_bd))
        st = dec * st + upd
    st_ref[...] = st

    for rows, o in zip(chunks, outs):
        for h in range(GLA_HEADS):
            cs = slice(h * GLA_DV, (h + 1) * GLA_DV)
            on = _normalize(o[:, cs]) * gh_ref[...]
            r = r_ref[rows, cs]
            o_ref[rows, cs] = (on * (r * _sigmoid(r))).astype(BF16)


def _gla(q, k, gl, v, r, g_head, batch, ct):
    n = q.shape[0]
    t = n // batch
    nt = t // ct
    wcat, masks = _gla_constants()
    row = lambda b, i: (b * nt + i, 0)
    return pl.pallas_call(
        functools.partial(_gla_kernel, n_chunks=ct // GLA_CHUNK),
        out_shape=jax.ShapeDtypeStruct((n, GLA_V), BF16),
        grid=(batch, nt),
        in_specs=[pl.BlockSpec((ct, GLA_QK), row), pl.BlockSpec((ct, GLA_QK), row),
                  pl.BlockSpec((ct, GLA_QK), row), pl.BlockSpec((ct, GLA_V), row),
                  pl.BlockSpec((ct, GLA_V), row), _const_spec((1, GLA_DV)),
                  _const_spec(wcat.shape), _const_spec(masks.shape)],
        out_specs=pl.BlockSpec((ct, GLA_V), row),
        scratch_shapes=[pltpu.VMEM((GLA_QK, GLA_DV), F32)],
        compiler_params=_params(("parallel", "arbitrary")),
        name="gla",
    )(q, k, gl, v, r, g_head, jnp.asarray(wcat, BF16), jnp.asarray(masks, F32))


def _attn_out_kernel(o_ref, mq_ref, mk_ref, mv_ref, w_ref, x_ref, out_ref):
    heads = [slice(h * MEM_DH, (h + 1) * MEM_DH) for h in range(MEM_HEADS)]
    scores = [_dot_nt(mq_ref[:, cs], mk_ref[0, :, cs]) * (MEM_DH ** -0.5 * LOG2E) for cs in heads]
    parts = [o_ref[...]]
    for s, cs in zip(scores, heads):
        e = jnp.exp2(s - jnp.max(s, axis=-1, keepdims=True))
        p = e / jnp.sum(e, axis=-1, keepdims=True)
        parts.append(_dot(p.astype(BF16), mv_ref[0, :, cs]).astype(BF16))
    cat = jnp.concatenate(parts, axis=-1)
    out_ref[...] = x_ref[...] + _dot(cat, w_ref[...])


def _attn_out(o, mq, mkv, w_out, x, batch, tm):
    n, d = x.shape
    t = n // batch
    nt = t // tm
    m = mkv.shape[1]
    row = lambda b, i: (b * nt + i, 0)
    return pl.pallas_call(
        _attn_out_kernel,
        out_shape=jax.ShapeDtypeStruct((n, d), F32),
        grid=(batch, nt),
        in_specs=[pl.BlockSpec((tm, o.shape[1]), row), pl.BlockSpec((tm, MEM_WIDTH), row),
                  pl.BlockSpec((1, m, MEM_WIDTH), lambda b, i: (b, 0, 0)),
                  pl.BlockSpec((1, m, MEM_WIDTH), lambda b, i: (b, 0, 1)),
                  _const_spec(w_out.shape), pl.BlockSpec((tm, d), row)],
        out_specs=pl.BlockSpec((tm, d), row),
        compiler_params=_params(("parallel", "parallel")),
        name="attn_out",
    )(o, mq, mkv, mkv, w_out, x)


def _ffn_kernel(x_ref, g_ref, wup_ref, cw_ref, cb_ref, wd_ref, gf_ref, out_ref, act_ref, tail_ref, *,
                final_norm):
    @pl.when(pl.program_id(1) == 0)
    def _():
        tail_ref[...] = jnp.zeros_like(tail_ref)

    x = x_ref[...]
    tm = x.shape[0]
    h = (_normalize(x) * g_ref[...]).astype(BF16)
    rid = lax.broadcasted_iota(jnp.int32, (8, FFN_CHUNK), 0)
    for j in range(FFN_DIM // FFN_CHUNK):
        cs = slice(j * FFN_CHUNK, (j + 1) * FFN_CHUNK)
        a = _dot(h, wup_ref[:, cs])
        b = _dot(h, wup_ref[:, FFN_DIM + j * FFN_CHUNK:FFN_DIM + (j + 1) * FFN_CHUNK])
        tail = tail_ref[j]
        r1 = pltpu.roll(a, 1, 0)
        r2 = pltpu.roll(a, 2, 0)
        top1 = jnp.where(rid == 0, tail[7:8], r1[:8])
        top2 = jnp.where(rid == 0, tail[6:7], jnp.where(rid == 1, tail[7:8], r2[:8]))
        a1 = jnp.concatenate([top1, r1[8:]], axis=0)
        a2 = jnp.concatenate([top2, r2[8:]], axis=0)
        tail_ref[j] = a[tm - 8:]
        ac = a2 * cw_ref[0:1, cs] + a1 * cw_ref[1:2, cs] + a * cw_ref[2:3, cs] + cb_ref[:, cs]
        act_ref[:, cs] = (ac * _sigmoid(ac) * b).astype(BF16)
    y = x + _dot(act_ref[...], wd_ref[...])
    if final_norm:
        y = _normalize(y) * gf_ref[...]
    out_ref[...] = y


def _ffn(x, g, w_up, conv_w, conv_b, w_down, g_final, batch, tm, final_norm):
    n, d = x.shape
    nt = n // batch // tm
    row = lambda b, i: (b * nt + i, 0)
    once = pl.Buffered(1)
    return pl.pallas_call(
        functools.partial(_ffn_kernel, final_norm=final_norm),
        out_shape=jax.ShapeDtypeStruct((n, d), F32),
        grid=(batch, nt),
        in_specs=[pl.BlockSpec((tm, d), row), _const_spec((1, d)),
                  pl.BlockSpec(w_up.shape, lambda b, i: (0, 0), pipeline_mode=once),
                  _const_spec(conv_w.shape), _const_spec(conv_b.shape),
                  pl.BlockSpec(w_down.shape, lambda b, i: (0, 0), pipeline_mode=once),
                  _const_spec((1, d))],
        out_specs=pl.BlockSpec((tm, d), row),
        scratch_shapes=[pltpu.VMEM((tm, FFN_DIM), BF16),
                        pltpu.VMEM((FFN_DIM // FFN_CHUNK, 8, FFN_CHUNK), F32)],
        compiler_params=_params(("parallel", "arbitrary")),
        name="conv_ffn",
    )(x, g, w_up, conv_w, conv_b, w_down, g_final)


def _proj_b_kernel(x_ref, gkv_ref, gmix_ref, wkv_ref, wqt_ref, wgt_ref, wmq_ref,
                   cmp_ref, kvs_ref, qt_ref, gt_ref, mq_ref):
    xn = _normalize(x_ref[...])
    hkv = (xn * gkv_ref[...]).astype(BF16)
    h1 = (xn * gmix_ref[...]).astype(BF16)
    cmp_ref[0] = _dot(hkv, wkv_ref[:, 0:KV_WIDTH]).astype(BF16)
    cmp_ref[1] = _dot(hkv, wkv_ref[:, KV_WIDTH:2 * KV_WIDTH]).astype(BF16)
    kvs_ref[...] = _dot(hkv, wkv_ref[:, 2 * KV_WIDTH:]).astype(BF16)
    qt_ref[0] = (_dot_nt(wqt_ref[...], h1) * (NSA_DH ** -0.5 * LOG2E)).astype(BF16)
    gt_ref[0] = _sigmoid(_dot_nt(wgt_ref[...], h1))
    mq_ref[...] = _dot(h1, wmq_ref[...]).astype(BF16)


def _proj_b(x, g_kv, g_mix, w_kv, wqt, wgt, wmq, batch, tm):
    n, d = x.shape
    t = n // batch
    nt = t // tm
    row = lambda b, i: (b * nt + i, 0)
    outs = [
        jax.ShapeDtypeStruct((2, n, KV_WIDTH), BF16),
        jax.ShapeDtypeStruct((n, 4 * KV_WIDTH), BF16),
        jax.ShapeDtypeStruct((batch, NSA_WIDTH, t), BF16),
        jax.ShapeDtypeStruct((batch, wgt.shape[0], t), F32),
        jax.ShapeDtypeStruct((n, MEM_WIDTH), BF16),
    ]
    return pl.pallas_call(
        _proj_b_kernel,
        out_shape=outs,
        grid=(batch, nt),
        in_specs=[pl.BlockSpec((tm, d), row), _const_spec((1, d)), _const_spec((1, d)),
                  _const_spec(w_kv.shape), _const_spec(wqt.shape), _const_spec(wgt.shape),
                  _const_spec(wmq.shape)],
        out_specs=[pl.BlockSpec((2, tm, KV_WIDTH), lambda b, i: (0, b * nt + i, 0)),
                   pl.BlockSpec((tm, 4 * KV_WIDTH), row),
                   pl.BlockSpec((1, NSA_WIDTH, tm), lambda b, i: (b, 0, i)),
                   pl.BlockSpec((1, wgt.shape[0], tm), lambda b, i: (b, 0, i)),
                   pl.BlockSpec((tm, MEM_WIDTH), row)],
        compiler_params=_params(("parallel", "parallel")),
        name="proj_b",
    )(x, g_kv, g_mix, w_kv, wqt, wgt, wmq)


def _compress_kernel(a_ref, w1_ref, pe_ref, w2_ref, o_ref):
    a = a_ref[0, 0]
    w1 = w1_ref[0]
    u = _dot(a, w1)
    cpe = _dot(pe_ref[0], w1)
    nrow = a.shape[0]
    hid = []
    for g in range(NSA_GROUPS):
        c0 = slice((2 * g) * CMP_HIDDEN, (2 * g + 1) * CMP_HIDDEN)
        c1 = slice((2 * g + 1) * CMP_HIDDEN, (2 * g + 2) * CMP_HIDDEN)
        nxt = pltpu.roll(u[:, c1], nrow - 1, 0)
        hid.append(u[:, c0] + nxt + cpe[0:1, c0] + cpe[8:9, c1])
    hcat = jax.nn.gelu(jnp.concatenate(hid, axis=-1), approximate=True).astype(BF16)
    o_ref[0, 0] = _dot(hcat, w2_ref[0]).astype(BF16)


def _compress(a, w1, pe, w2):
    _, batch, nsub, width = a.shape
    return pl.pallas_call(
        _compress_kernel,
        out_shape=jax.ShapeDtypeStruct((2, batch, nsub, KV_WIDTH), BF16),
        grid=(2, batch),
        in_specs=[pl.BlockSpec((1, 1, nsub, width), lambda s, b: (s, b, 0, 0)),
                  pl.BlockSpec((1,) + w1.shape[1:], lambda s, b: (s, 0, 0)),
                  pl.BlockSpec((1,) + pe.shape[1:], lambda s, b: (s, 0, 0)),
                  pl.BlockSpec((1,) + w2.shape[1:], lambda s, b: (s, 0, 0))],
        out_specs=pl.BlockSpec((1, 1, nsub, KV_WIDTH), lambda s, b: (s, b, 0, 0)),
        compiler_params=_params(("parallel", "parallel")),
        name="compress",
    )(a, w1, pe, w2)


def _alibi_tiles(ncmp_pad):
    slopes = LOG2E * 2.0 ** (-8.0 * np.arange(1, NSA_HEADS + 1) / NSA_HEADS)
    tl = np.tile(np.arange(Q_BLOCK), NSA_HPG)[None, :]
    cmp_end = (np.arange(ncmp_pad) * CMP_STRIDE + CMP_BLOCK - 1)[:, None]
    key = np.arange(SEL_BLOCK)[:, None]
    dist_win = tl + WINDOW - np.arange(WIN_KEYS)[:, None]
    c_cmp = np.zeros((NSA_GROUPS, ncmp_pad, NSA_HPG * Q_BLOCK), np.float32)
    c_blk = np.zeros((NSA_GROUPS, SEL_BLOCK, NSA_HPG * Q_BLOCK), np.float32)
    c_win = np.zeros((NSA_GROUPS, WIN_KEYS, NSA_HPG * Q_BLOCK), np.float32)
    srow = np.zeros((NSA_GROUPS, 1, NSA_HPG * Q_BLOCK), np.float32)
    for g in range(NSA_GROUPS):
        s = np.repeat(slopes[g * NSA_HPG:(g + 1) * NSA_HPG], Q_BLOCK)[None, :]
        c_cmp[g] = -s * (tl - cmp_end)
        c_blk[g] = -s * (tl - key)
        c_win[g] = np.where((dist_win >= 0) & (dist_win < WINDOW), -s * dist_win, -np.inf)
        srow[g] = s
    d_cmp = (cmp_end - tl).astype(np.float32)
    return c_cmp, d_cmp, c_blk, c_win, srow


def _group_queries(qt_ref, g):
    zeros = jnp.zeros((NSA_DH, Q_BLOCK), BF16)
    cols = []
    for hh in range(NSA_HPG):
        h = g * NSA_HPG + hh
        piece = qt_ref[0, h * NSA_DH:(h + 1) * NSA_DH, :]
        cols.append(jnp.concatenate([piece, zeros] if g == 0 else [zeros, piece], axis=0))
    return jnp.concatenate(cols, axis=1)


def _nsa_a_kernel(qt_ref, ckv_ref, ccmp_ref, dcmp_ref, srow_ref, ovt_ref, oc_ref, sel_ref, cnt_ref, imp_ref):
    qb = pl.program_id(1)
    base = (qb * Q_BLOCK).astype(F32)
    nblk = sel_ref.shape[3]
    jidx = lax.broadcasted_iota(jnp.int32, (nblk, Q_BLOCK), 0)
    tl = lax.broadcasted_iota(jnp.int32, (nblk, Q_BLOCK), 1)
    cur = 2 * qb + jnp.where(tl >= SEL_BLOCK, 1, 0)
    valid_blk = jidx <= cur
    n_chunks = (qb * (Q_BLOCK // CMP_STRIDE) + (Q_BLOCK - CMP_BLOCK) // CMP_STRIDE) // CMP_CHUNK + 1
    groups = range(NSA_GROUPS)
    qts = [_group_queries(qt_ref, g) for g in groups]
    shifts = [srow_ref[g] * base for g in groups]

    def cmp_branch(n_rows):
        kc = ckv_ref[0, 0, 0:n_rows, :]
        vc = ckv_ref[1, 0, 0:n_rows, :]
        ov = ovt_ref[:, 0:n_rows]
        ov2 = jnp.concatenate([ov, ov], axis=1)
        raws = [_dot(kc, qts[g]) for g in groups]
        visible = dcmp_ref[0:n_rows, :] <= base
        for g in groups:
            s = raws[g] + jnp.where(visible, ccmp_ref[g, 0:n_rows, :] - shifts[g], NEG_INF)
            m = jnp.max(s, axis=0, keepdims=True)
            e = jnp.exp2(s - jnp.where(m == NEG_INF, 0.0, m))
            inv = 1.0 / jnp.maximum(jnp.sum(e, axis=0, keepdims=True), 1e-30)
            e1 = e.astype(BF16)
            e2 = (e - e1.astype(F32)).astype(BF16)
            oc = _dot_tn(vc, e1)[g * NSA_DH:(g + 1) * NSA_DH] * inv
            imp = _dot(ov2, jnp.concatenate([e1, e2], axis=0)) * inv
            for hh in range(NSA_HPG):
                h = g * NSA_HPG + hh
                oc_ref[0, h * NSA_DH:(h + 1) * NSA_DH, :] = oc[:, hh * Q_BLOCK:(hh + 1) * Q_BLOCK]
            imp_g = imp[:, 0:Q_BLOCK]
            for hh in range(1, NSA_HPG):
                imp_g = imp_g + imp[:, hh * Q_BLOCK:(hh + 1) * Q_BLOCK]
            imp_ref[g] = imp_g

    for nc in range(1, ckv_ref.shape[2] // CMP_CHUNK + 1):
        pl.when(n_chunks == nc)(functools.partial(cmp_branch, nc * CMP_CHUNK))
    imps = [imp_ref[g] for g in groups]

    def emit(g, sel):
        sel_ref[0, g, 0] = sel
        cnt_ref[0, g, 0] = _dot_nt(jnp.ones((8, Q_BLOCK), BF16), sel.astype(BF16))

    @pl.when(qb < RANK_FIRST_TILE)
    def _():
        for g in groups:
            emit(g, jnp.where(valid_blk, 1.0, 0.0))

    @pl.when(qb >= RANK_FIRST_TILE)
    def _():
        jf = jidx.astype(F32)
        cand = (jidx >= 1) & (jidx <= cur - 2)
        v0 = [jnp.where(cand, imps[g], NEG_INF) for g in groups]
        v = list(v0)
        for _ in range(SEL_TOPN - 3):
            for g in groups:
                mx = jnp.max(v[g], axis=0, keepdims=True)
                first = jnp.min(jnp.where(v[g] == mx, jf, float(nblk)), axis=0, keepdims=True)
                v[g] = jnp.where(jf == first, NEG_INF, v[g])
        always = (jidx == 0) | (jidx == cur) | (jidx == cur - 1)
        for g in groups:
            emit(g, jnp.where((v[g] != v0[g]) | always, 1.0, 0.0))


def _nsa_a(qt, ckv, c_cmp, d_cmp, srow, ovt):
    batch, _, t = qt.shape
    nqb = t // Q_BLOCK
    ns = t // SEL_BLOCK
    ncmp = ckv.shape[2]
    outs = [jax.ShapeDtypeStruct((batch, NSA_WIDTH, t), F32),
            jax.ShapeDtypeStruct((batch, NSA_GROUPS, nqb, ns, Q_BLOCK), F32),
            jax.ShapeDtypeStruct((batch, NSA_GROUPS, nqb, 8, ns), F32)]
    return pl.pallas_call(
        _nsa_a_kernel,
        out_shape=outs,
        grid=(batch, nqb),
        in_specs=[pl.BlockSpec((1, NSA_WIDTH, Q_BLOCK), lambda b, i: (b, 0, i)),
                  pl.BlockSpec((2, 1, ncmp, KV_WIDTH), lambda b, i: (0, b, 0, 0)),
                  _const_spec(c_cmp.shape), _const_spec(d_cmp.shape), _const_spec(srow.shape),
                  _const_spec(ovt.shape)],
        out_specs=[pl.BlockSpec((1, NSA_WIDTH, Q_BLOCK), lambda b, i: (b, 0, i)),
                   pl.BlockSpec((1, NSA_GROUPS, 1, ns, Q_BLOCK), lambda b, i: (b, 0, i, 0, 0)),
                   pl.BlockSpec((1, NSA_GROUPS, 1, 8, ns), lambda b, i: (b, 0, i, 0, 0))],
        scratch_shapes=[pltpu.VMEM((NSA_GROUPS, ns, Q_BLOCK), F32)],
        compiler_params=_params(("parallel", "parallel")),
        name="nsa_select",
    )(qt, ckv, c_cmp, d_cmp, srow, ovt)


def _block_softmax_step(carry, raw, consts, shifts, v_rows, rows_g):
    m, l8, acc = carry
    width = raw.shape[1]
    fold = lambda x: x.reshape(SEL_BLOCK // 8, 8, width)
    us = [raw[i * SEL_BLOCK:(i + 1) * SEL_BLOCK] + c for i, c in enumerate(consts)]
    top8 = None
    for u, sh in zip(us, shifts):
        t = jnp.max(fold(u), axis=0) + sh
        top8 = t if top8 is None else jnp.maximum(top8, t)
    m_new = jnp.maximum(m, jnp.max(top8, axis=0, keepdims=True))
    m_safe = jnp.where(m_new == NEG_INF, 0.0, m_new)
    alpha = jnp.where(m == NEG_INF, 0.0, jnp.exp2(m - m_safe))
    ps = [jnp.exp2(u - (m_safe - sh)) for u, sh in zip(us, shifts)]
    l8 = alpha * l8
    for p in ps:
        l8 = l8 + jnp.sum(fold(p), axis=0)
    pv = _dot_tn(v_rows, jnp.concatenate(ps, axis=0).astype(BF16))
    return m_new, l8, alpha * acc + pv[rows_g]


def _nsa_b_kernel(idx_ref, qt_ref, ks_ref, vs_ref, kw_ref, vw_ref, sel_ref, oc_ref, gt_ref, cblk_ref, cwin_ref,
                  srow_ref, o_ref, *, idx_words):
    b = pl.program_id(0)
    qb = pl.program_id(1)
    base = qb * Q_BLOCK
    width = NSA_HPG * Q_BLOCK
    groups = range(NSA_GROUPS)
    qts = [_group_queries(qt_ref, g) for g in groups]
    srows = [srow_ref[g] for g in groups]
    rows = [slice(g * NSA_DH, (g + 1) * NSA_DH) for g in groups]
    entries = [((b * NSA_GROUPS + g) * pl.num_programs(1) + qb) * idx_words for g in groups]
    n_act = [idx_ref[entries[g] + idx_words - 1] for g in groups]
    init = (jnp.full((1, width), NEG_INF, F32), jnp.zeros((8, width), F32), jnp.zeros((NSA_DH, width), F32))
    zero_row = jnp.zeros((1, width), F32)

    def gate_row(g, j, live):
        picked = sel_ref[0, g, 0, pl.ds(j, 1), :]
        gate = jnp.where((picked > 0.0) & live, 0.0, NEG_INF)
        return jnp.concatenate([gate] * NSA_HPG, axis=1)

    def stage(it):
        out = []
        for g in groups:
            words = [idx_ref[entries[g] + it * (SLC_CHUNK // 4) + w] for w in range(SLC_CHUNK // 4)]
            kts, vts, shifts = [], [], []
            for i in range(SLC_CHUNK):
                live = it * SLC_CHUNK + i < n_act[g]
                j = jnp.where(live, lax.shift_right_logical(words[i // 4], 8 * (i % 4)) & 0xFF, 0)
                k0 = pl.multiple_of(j * SEL_BLOCK, SEL_BLOCK)
                kts.append(ks_ref[pl.ds(k0, SEL_BLOCK), :])
                vts.append(vs_ref[pl.ds(k0, SEL_BLOCK), :])
                shifts.append(gate_row(g, j, live) - srows[g] * (base - k0).astype(F32))
            out.append((_dot(jnp.concatenate(kts, axis=0), qts[g]), tuple(shifts), jnp.concatenate(vts, axis=0)))
        return tuple(out)

    def body(it, states):
        staged = stage(it)
        return tuple(_block_softmax_step(states[g], staged[g][0], [cblk_ref[g]] * SLC_CHUNK, staged[g][1],
                                         staged[g][2], rows[g]) for g in groups)

    n_it = (jnp.maximum(n_act[0], n_act[1]) + SLC_CHUNK - 1) // SLC_CHUNK
    slc = list(lax.fori_loop(0, n_it, body, (init,) * NSA_GROUPS))

    diag = range(WINDOW // SEL_BLOCK, WIN_KEYS // SEL_BLOCK)
    r0 = pl.multiple_of(base, Q_BLOCK)
    ks_d = ks_ref[pl.ds(r0, Q_BLOCK), :]
    vs_d = vs_ref[pl.ds(r0, Q_BLOCK), :]
    raw_d = [_dot(ks_d, qts[g]) for g in groups]
    n_win = WIN_KEYS // SEL_BLOCK
    win_staged = []
    for c in range(WIN_KEYS // WIN_CHUNK):
        blocks = range(c * n_win // 2, (c + 1) * n_win // 2)
        kws, vws, offs = [], [], []
        for r in blocks:
            j = 2 * qb - WINDOW // SEL_BLOCK + r
            k0 = pl.multiple_of(jnp.maximum(j, 0) * SEL_BLOCK, SEL_BLOCK)
            kws.append(kw_ref[pl.ds(k0, SEL_BLOCK), :])
            vws.append(vw_ref[pl.ds(k0, SEL_BLOCK), :])
            offs.append(zero_row + jnp.where(j >= 0, 0.0, NEG_INF))
        kw = jnp.concatenate(kws, axis=0)
        win_staged.append((blocks, [_dot(kw, qts[g]) for g in groups], offs, jnp.concatenate(vws, axis=0)))

    for g in groups:
        consts = [cwin_ref[g, r * SEL_BLOCK:(r + 1) * SEL_BLOCK, :] for r in diag]
        shifts = [gate_row(g, 2 * qb + i, True) for i in range(len(diag))]
        slc[g] = _block_softmax_step(slc[g], raw_d[g], consts, shifts, vs_d, rows[g])
    win = [init] * NSA_GROUPS
    for blocks, raws, offs, vw in win_staged:
        for g in groups:
            consts = [cwin_ref[g, r * SEL_BLOCK:(r + 1) * SEL_BLOCK, :] for r in blocks]
            win[g] = _block_softmax_step(win[g], raws[g], consts, offs, vw, rows[g])

    heads_out = []
    for g in groups:
        o_s = slc[g][2] * (1.0 / jnp.maximum(jnp.sum(slc[g][1], axis=0, keepdims=True), 1e-30))
        o_w = win[g][2] * (1.0 / jnp.maximum(jnp.sum(win[g][1], axis=0, keepdims=True), 1e-30))
        for hh in range(NSA_HPG):
            h = g * NSA_HPG + hh
            cs = slice(hh * Q_BLOCK, (hh + 1) * Q_BLOCK)
            gates = gt_ref[0, 3 * h:3 * h + 3, :]
            heads_out.append(gates[0:1] * oc_ref[0, h * NSA_DH:(h + 1) * NSA_DH, :]
                             + gates[1:2] * o_s[:, cs] + gates[2:3] * o_w[:, cs])
    o_ref[...] = jnp.concatenate(heads_out, axis=0).T.astype(BF16)


def _nsa_b(idx, qt, kvs, sel, oc, gt, c_blk, c_win, srow):
    batch, _, t = qt.shape
    nqb = t // Q_BLOCK
    ns = t // SEL_BLOCK
    kv_spec = lambda c: pl.BlockSpec((t, KV_WIDTH), lambda b, i, idx_ref: (b, c))
    grid_spec = pltpu.PrefetchScalarGridSpec(
        num_scalar_prefetch=1,
        grid=(batch, nqb),
        in_specs=[pl.BlockSpec((1, NSA_WIDTH, Q_BLOCK), lambda b, i, r: (b, 0, i)),
                  kv_spec(0), kv_spec(1), kv_spec(2), kv_spec(3),
                  pl.BlockSpec((1, NSA_GROUPS, 1, ns, Q_BLOCK), lambda b, i, r: (b, 0, i, 0, 0)),
                  pl.BlockSpec((1, NSA_WIDTH, Q_BLOCK), lambda b, i, r: (b, 0, i)),
                  pl.BlockSpec((1, gt.shape[1], Q_BLOCK), lambda b, i, r: (b, 0, i)),
                  pl.BlockSpec(c_blk.shape, lambda b, i, r: (0, 0, 0)),
                  pl.BlockSpec(c_win.shape, lambda b, i, r: (0, 0, 0)),
                  pl.BlockSpec(srow.shape, lambda b, i, r: (0, 0, 0))],
        out_specs=pl.BlockSpec((Q_BLOCK, NSA_WIDTH), lambda b, i, r: (b * nqb + i, 0)),
    )
    return pl.pallas_call(
        functools.partial(_nsa_b_kernel, idx_words=ns // 4 + 1),
        out_shape=jax.ShapeDtypeStruct((batch * t, NSA_WIDTH), BF16),
        grid_spec=grid_spec,
        compiler_params=_params(("parallel", "parallel")),
        name="nsa_attend",
    )(idx, qt, kvs, kvs, kvs, kvs, sel, oc, gt, c_blk, c_win, srow)


def _pad_cols(w, width):
    return jnp.pad(w, ((0, 0), (0, width - w.shape[1])))


def _compress_weights(w1, pe, w2):
    nsub = CMP_STRIDE
    w1r = w1.reshape(2, nsub, NSA_DH, CMP_HIDDEN)
    same_group = jnp.eye(NSA_GROUPS, dtype=F32)
    big = same_group[None, :, None, :, None, None] * w1r.transpose(1, 2, 0, 3)[:, None, :, None, :, :]
    big = big.reshape(nsub * KV_WIDTH, NSA_GROUPS * 2 * CMP_HIDDEN)
    per = pe.reshape(2, nsub, 1, NSA_DH)
    pe2 = jnp.broadcast_to(per, (2, nsub, NSA_GROUPS, NSA_DH)).reshape(2, 1, nsub * KV_WIDTH)
    pe2 = jnp.broadcast_to(pe2, (2, 8, nsub * KV_WIDTH)).reshape(16, nsub * KV_WIDTH)
    w2bd = same_group[:, None, :, None] * w2[None, :, None, :]
    return big.astype(BF16), pe2.astype(BF16), w2bd.reshape(NSA_GROUPS * CMP_HIDDEN, KV_WIDTH).astype(BF16)


def _selection_overlap_t(ncmp_pad, ns):
    cs = np.arange(ncmp_pad) * CMP_STRIDE
    ss = np.arange(ns) * SEL_BLOCK
    ov = np.minimum(cs[:, None] + CMP_BLOCK, ss[None, :] + SEL_BLOCK) - np.maximum(cs[:, None], ss[None, :])
    return (np.clip(ov, 0, None).astype(np.float32) / CMP_BLOCK).T


def _active_blocks(cnt):
    ns = cnt.shape[-1]
    before_tile = jnp.arange(ns)[None, :] < (Q_BLOCK // SEL_BLOCK) * jnp.arange(cnt.shape[2])[:, None]
    flags = (cnt[:, :, :, 0, :] > 0.0) & before_tile
    order = jnp.argsort(jnp.where(flags, 0, 1).astype(jnp.int32), axis=-1, stable=True).astype(jnp.int32)
    packed = jnp.sum(order.reshape(order.shape[:-1] + (ns // 4, 4)) << (8 * jnp.arange(4, dtype=jnp.int32)), axis=-1)
    n_act = jnp.sum(flags, axis=-1, dtype=jnp.int32)
    return jnp.concatenate([packed, n_act[..., None]], axis=-1).reshape(-1)


def kernel(x, mem, g_mix, g_ffn, g_mem, w_mem_kv, w_up, conv_w, conv_b, w_down,
           a_w_in, a_w_alpha, a_b_alpha, a_g_head, a_w_out,
           g_kv, w_kv, pe_k, pe_v, w_ck1, w_ck2, w_cv1, w_cv2,
           b_w_in, b_w_out, g_final):
    batch, t, d = x.shape
    n = batch * t
    m = mem.shape[1]
    tm = min(1024, t)
    xf = x.reshape(n, d)
    memf = mem.reshape(batch * m, d)
    row = lambda v: v.reshape(1, -1)

    mkv0 = _rms_proj(memf, row(g_mem[0]), w_mem_kv[0].astype(BF16), m, BF16).reshape(batch, m, 2 * MEM_WIDTH)
    wa = a_w_in[0]
    c_alr = 2 * GLA_QK + 2 * GLA_V
    w_a = jnp.concatenate([wa[:, :c_alr], wa[:, c_alr + GLA_RANK:], _pad_cols(wa[:, c_alr:c_alr + GLA_RANK], LANE)],
                          axis=1).astype(BF16)
    w_alpha = jnp.pad(a_w_alpha[0], ((0, LANE - GLA_RANK), (0, 0))).astype(BF16)
    q, k, gl, v, r, mq = _inproj_a(xf, row(g_mix[0]), w_a, w_alpha, row(a_b_alpha[0]), tm)
    o = _gla(q, k, gl, v, r, row(a_g_head[0]), batch, tm)
    xf = _attn_out(o, mq, mkv0, a_w_out[0].astype(BF16), xf, batch, tm)
    xf = _ffn(xf, row(g_ffn[0]), w_up[0].astype(BF16), conv_w[0], row(conv_b[0]), w_down[0].astype(BF16),
              row(g_final), batch, tm, final_norm=False)

    wb = b_w_in[0]
    wqt = wb[:, :NSA_WIDTH].T.astype(BF16)
    n_gate = 3 * NSA_HEADS
    wgt = jnp.pad(wb[:, NSA_WIDTH:NSA_WIDTH + n_gate].T, ((0, 32 - n_gate), (0, 0))).astype(BF16)
    wmq = wb[:, NSA_WIDTH + n_gate:].astype(BF16)
    ckv_in, kvs, qt, gt, mq1 = _proj_b(xf, row(g_kv), row(g_mix[1]), w_kv.astype(BF16), wqt, wgt, wmq, batch, tm)
    nsub = t // CMP_STRIDE
    w1k, pek, w2k = _compress_weights(w_ck1, pe_k, w_ck2)
    w1v, pev, w2v = _compress_weights(w_cv1, pe_v, w_cv2)
    ckv = _compress(ckv_in.reshape(2, batch, nsub, CMP_STRIDE * KV_WIDTH),
                    jnp.stack([w1k, w1v]), jnp.stack([pek, pev]), jnp.stack([w2k, w2v]))

    ns = t // SEL_BLOCK
    c_cmp, d_cmp, c_blk, c_win, srow = _alibi_tiles(nsub)
    ovt = jnp.asarray(_selection_overlap_t(nsub, ns), BF16)
    oc, sel, cnt = _nsa_a(qt, ckv, jnp.asarray(c_cmp), jnp.asarray(d_cmp), jnp.asarray(srow), ovt)
    idx = _active_blocks(cnt)
    o1 = _nsa_b(idx, qt, kvs, sel, oc, gt, jnp.asarray(c_blk), jnp.asarray(c_win), jnp.asarray(srow))
    mkv1 = _rms_proj(memf, row(g_mem[1]), w_mem_kv[1].astype(BF16), m, BF16).reshape(batch, m, 2 * MEM_WIDTH)
    xf = _attn_out(o1, mq1, mkv1, b_w_out[0].astype(BF16), xf, batch, tm)
    xf = _ffn(xf, row(g_ffn[1]), w_up[1].astype(BF16), conv_w[1], row(conv_b[1]), w_down[1].astype(BF16),
              row(g_final), batch, tm, final_norm=True)
    return xf.reshape(batch, t, d)
```

```python
import functools

import numpy as np
import jax
import jax.numpy as jnp
from jax import lax
from jax.experimental import pallas as pl
from jax.experimental.pallas import tpu as pltpu

F32 = jnp.float32
BF16 = jnp.bfloat16
EPS = 1e-6
NEG_INF = float("-inf")
LOG2E = 1.4426950408889634

V7X_VMEM_BYTES = 64 * 1024 * 1024
VMEM_LIMIT = V7X_VMEM_BYTES - 8 * 1024 * 1024

D_MODEL = 1024
MEM_HEADS = 4
MEM_DH = 128
MEM_WIDTH = MEM_HEADS * MEM_DH
GLA_HEADS = 4
GLA_DK = 64
GLA_DV = 128
GLA_QK = GLA_HEADS * GLA_DK
GLA_V = GLA_HEADS * GLA_DV
GLA_RANK = 16
GLA_TAU = 16.0
GLA_CHUNK = 64
GLA_LEVELS = 6
NSA_HEADS = 8
NSA_GROUPS = 2
NSA_HPG = NSA_HEADS // NSA_GROUPS
NSA_DH = 64
NSA_WIDTH = NSA_HEADS * NSA_DH
KV_WIDTH = NSA_GROUPS * NSA_DH
CMP_BLOCK = 32
CMP_STRIDE = 16
CMP_HIDDEN = 256
SEL_BLOCK = 64
SEL_TOPN = 16
WINDOW = 512
Q_BLOCK = 128
WIN_KEYS = WINDOW + Q_BLOCK
WIN_CHUNK = WIN_KEYS // 2
CMP_CHUNK = 128
SLC_CHUNK = 6
SLC_WORDS = -(-SLC_CHUNK // 4)
RANK_FIRST_TILE = SEL_TOPN * SEL_BLOCK // Q_BLOCK
FFN_DIM = 2816
FFN_CHUNK = 256
LANE = 128


def _dot(a, b):
    return jnp.dot(a, b, preferred_element_type=F32)


def _dot_nt(a, b):
    return lax.dot_general(a, b, (((1,), (1,)), ((), ())), preferred_element_type=F32)


def _dot_tn(a, b):
    return lax.dot_general(a, b, (((0,), (0,)), ((), ())), preferred_element_type=F32)


def _params(sem):
    return pltpu.CompilerParams(dimension_semantics=sem, vmem_limit_bytes=VMEM_LIMIT)


def _const_spec(shape):
    n = len(shape)
    return pl.BlockSpec(shape, lambda *_: (0,) * n)


def _normalize(x):
    return x * lax.rsqrt(jnp.mean(x * x, axis=-1, keepdims=True) + EPS)


def _sigmoid(x):
    return 1.0 / (1.0 + jnp.exp(-x))


def _rms_proj_kernel(x_ref, g_ref, w_ref, o_ref):
    h = (_normalize(x_ref[...]) * g_ref[...]).astype(BF16)
    o_ref[...] = _dot(h, w_ref[...]).astype(o_ref.dtype)


def _rms_proj(x, g, w, tm, out_dtype):
    n, d = x.shape
    p = w.shape[1]
    return pl.pallas_call(
        _rms_proj_kernel,
        out_shape=jax.ShapeDtypeStruct((n, p), out_dtype),
        grid=(n // tm,),
        in_specs=[pl.BlockSpec((tm, d), lambda i: (i, 0)), _const_spec((1, d)), _const_spec((d, p))],
        out_specs=pl.BlockSpec((tm, p), lambda i: (i, 0)),
        compiler_params=_params(("parallel",)),
        name="rms_proj",
    )(x, g, w)


def _inproj_a_kernel(x_ref, g_ref, w_ref, wa_ref, ba_ref, q_ref, k_ref, gl_ref, v_ref, r_ref, mq_ref):
    h = (_normalize(x_ref[...]) * g_ref[...]).astype(BF16)
    c = 0
    q_ref[...] = _dot(h, w_ref[:, c:c + GLA_QK]) * (GLA_DK ** -0.5)
    c += GLA_QK
    k_ref[...] = _dot(h, w_ref[:, c:c + GLA_QK])
    c += GLA_QK
    v_ref[...] = _dot(h, w_ref[:, c:c + GLA_V]).astype(BF16)
    c += GLA_V
    r_ref[...] = _dot(h, w_ref[:, c:c + GLA_V])
    c += GLA_V
    mq_ref[...] = _dot(h, w_ref[:, c:c + MEM_WIDTH]).astype(BF16)
    c += MEM_WIDTH
    alr = _dot(h, w_ref[:, c:c + LANE]).astype(BF16)
    z = _dot(alr, wa_ref[...]) + ba_ref[...]
    log_sig = jnp.minimum(z, 0.0) - jnp.log1p(jnp.exp(-jnp.abs(z)))
    gl_ref[...] = log_sig * (1.0 / GLA_TAU)


def _inproj_a(x, g, w, wa, ba, tm):
    n, d = x.shape
    row = lambda i: (i, 0)
    outs = [
        jax.ShapeDtypeStruct((n, GLA_QK), F32), jax.ShapeDtypeStruct((n, GLA_QK), F32),
        jax.ShapeDtypeStruct((n, GLA_QK), F32), jax.ShapeDtypeStruct((n, GLA_V), BF16),
        jax.ShapeDtypeStruct((n, GLA_V), F32), jax.ShapeDtypeStruct((n, MEM_WIDTH), BF16),
    ]
    return pl.pallas_call(
        _inproj_a_kernel,
        out_shape=outs,
        grid=(n // tm,),
        in_specs=[pl.BlockSpec((tm, d), row), _const_spec((1, d)), _const_spec(w.shape),
                  _const_spec(wa.shape), _const_spec(ba.shape)],
        out_specs=[pl.BlockSpec((tm, s.shape[1]), row) for s in outs],
        compiler_params=_params(("parallel",)),
        name="inproj_a",
    )(x, g, w, wa, ba)


def _gla_constants():
    c = GLA_CHUNK
    w = np.zeros((8 * c, c), np.float32)
    masks = np.zeros((GLA_LEVELS + 1, c, c), np.float32)
    masks[0] = np.eye(c)
    for l in range(1, GLA_LEVELS + 1):
        blk, half = 2 ** l, 2 ** (l - 1)
        for i in range(c):
            mid = (i // blk) * blk + half - 1
            if i % blk >= half:
                w[(l - 1) * c + i, mid + 1:i + 1] = 1.0
            else:
                w[(l - 1) * c + i, i + 1:mid + 1] = 1.0
        for t in range(c):
            for s in range(c):
                if t // blk == s // blk and t % blk >= half and s % blk < half:
                    masks[l, t, s] = 1.0
    for i in range(c):
        w[6 * c + i, :i + 1] = 1.0
        w[7 * c + i, i + 1:] = 1.0
    return np.concatenate([w, w], axis=1), np.tile(masks, (1, 1, GLA_HEADS))


def _gla_kernel(q_ref, k_ref, gl_ref, v_ref, r_ref, gh_ref, wcat_ref, msk_ref, o_ref, st_ref, *, n_chunks):
    c = GLA_CHUNK

    @pl.when(pl.program_id(1) == 0)
    def _():
        st_ref[...] = jnp.zeros_like(st_ref)

    lane_qk = lax.broadcasted_iota(jnp.int32, (1, GLA_QK), 1) // GLA_DK
    lane_v = lax.broadcasted_iota(jnp.int32, (1, GLA_V), 1) // GLA_DV
    row_qk = lax.broadcasted_iota(jnp.int32, (GLA_QK, 1), 0) // GLA_DK
    ones = jnp.ones((2 * c, LANE), BF16)
    wcat = wcat_ref[...]

    def stack_heads(x, lane_head):
        return jnp.concatenate([jnp.where(lane_head == h, x, jnp.zeros_like(x)) for h in range(GLA_HEADS)], axis=0)

    chunks = [slice(ci * c, (ci + 1) * c) for ci in range(n_chunks)]
    es, decs = [], []
    for rows in chunks:
        g = gl_ref[rows, :] * LOG2E
        g_hi = g.astype(BF16)
        g_split = jnp.concatenate([g_hi, (g - g_hi.astype(F32)).astype(BF16)], axis=0)
        es.append(jnp.exp2(_dot(wcat, g_split)))
        decs.append(jnp.exp2(_dot_tn(g_split, ones)))

    o_intras, q_ins, upds = [], [], []
    for rows, e in zip(chunks, es):
        q = q_ref[rows, :]
        k = k_ref[rows, :]
        v = v_ref[rows, :]
        attn = jnp.zeros((c, GLA_HEADS * c), F32)
        for l in range(GLA_LEVELS + 1):
            if l == 0:
                ql, kl = q, k
            else:
                el = e[(l - 1) * c:l * c]
                ql, kl = q * el, k * el
            attn = attn + _dot_nt(ql.astype(BF16), stack_heads(kl, lane_qk).astype(BF16)) * msk_ref[l]
        o_intras.append(_dot(attn.astype(BF16), stack_heads(v, lane_v)))
        q_ins.append((q * e[6 * c:7 * c]).astype(BF16))
        kk = stack_heads(k * e[7 * c:8 * c], lane_qk).astype(BF16)
        v_rows = jnp.concatenate([v[:, h * GLA_DV:(h + 1) * GLA_DV] for h in range(GLA_HEADS)], axis=0)
        upds.append(_dot_tn(kk, v_rows))

    st = st_ref[...]
    outs = []
    for o_intra, q_in, dec, upd in zip(o_intras, q_ins, decs, upds):
        st_bd = jnp.concatenate([jnp.where(row_qk == h, st, 0.0).astype(BF16) for h in range(GLA_HEADS)], axis=1)
        outs.append(o_intra + _dot(q_in, st_bd))
        st = dec * st + upd
    st_ref[...] = st

    for rows, o in zip(chunks, outs):
        for h in range(GLA_HEADS):
            cs = slice(h * GLA_DV, (h + 1) * GLA_DV)
            on = _normalize(o[:, cs]) * gh_ref[...]
            r = r_ref[rows, cs]
            o_ref[rows, cs] = (on * (r * _sigmoid(r))).astype(BF16)


def _gla(q, k, gl, v, r, g_head, batch, ct):
    n = q.shape[0]
    t = n // batch
    nt = t // ct
    wcat, masks = _gla_constants()
    row = lambda b, i: (b * nt + i, 0)
    return pl.pallas_call(
        functools.partial(_gla_kernel, n_chunks=ct // GLA_CHUNK),
        out_shape=jax.ShapeDtypeStruct((n, GLA_V), BF16),
        grid=(batch, nt),
        in_specs=[pl.BlockSpec((ct, GLA_QK), row), pl.BlockSpec((ct, GLA_QK), row),
                  pl.BlockSpec((ct, GLA_QK), row), pl.BlockSpec((ct, GLA_V), row),
                  pl.BlockSpec((ct, GLA_V), row), _const_spec((1, GLA_DV)),
                  _const_spec(wcat.shape), _const_spec(masks.shape)],
        out_specs=pl.BlockSpec((ct, GLA_V), row),
        scratch_shapes=[pltpu.VMEM((GLA_QK, GLA_DV), F32)],
        compiler_params=_params(("parallel", "arbitrary")),
        name="gla",
    )(q, k, gl, v, r, g_head, jnp.asarray(wcat, BF16), jnp.asarray(masks, F32))


def _attn_out_kernel(o_ref, mq_ref, mk_ref, mv_ref, w_ref, x_ref, out_ref):
    heads = [slice(h * MEM_DH, (h + 1) * MEM_DH) for h in range(MEM_HEADS)]
    scores = [_dot_nt(mq_ref[:, cs], mk_ref[0, :, cs]) * (MEM_DH ** -0.5 * LOG2E) for cs in heads]
    parts = [o_ref[...]]
    for s, cs in zip(scores, heads):
        e = jnp.exp2(s - jnp.max(s, axis=-1, keepdims=True))
        p = e / jnp.sum(e, axis=-1, keepdims=True)
        parts.append(_dot(p.astype(BF16), mv_ref[0, :, cs]).astype(BF16))
    cat = jnp.concatenate(parts, axis=-1)
    out_ref[...] = x_ref[...] + _dot(cat, w_ref[...])


def _attn_out(o, mq, mkv, w_out, x, batch, tm):
    n, d = x.shape
    t = n // batch
    nt = t // tm
    m = mkv.shape[1]
    row = lambda b, i: (b * nt + i, 0)
    return pl.pallas_call(
        _attn_out_kernel,
        out_shape=jax.ShapeDtypeStruct((n, d), F32),
        grid=(batch, nt),
        in_specs=[pl.BlockSpec((tm, o.shape[1]), row), pl.BlockSpec((tm, MEM_WIDTH), row),
                  pl.BlockSpec((1, m, MEM_WIDTH), lambda b, i: (b, 0, 0)),
                  pl.BlockSpec((1, m, MEM_WIDTH), lambda b, i: (b, 0, 1)),
                  _const_spec(w_out.shape), pl.BlockSpec((tm, d), row)],
        out_specs=pl.BlockSpec((tm, d), row),
        compiler_params=_params(("parallel", "parallel")),
        name="attn_out",
    )(o, mq, mkv, mkv, w_out, x)


def _ffn_kernel(x_ref, g_ref, wup_ref, cw_ref, cb_ref, wd_ref, gf_ref, out_ref, act_ref, tail_ref, *,
                final_norm):
    @pl.when(pl.program_id(1) == 0)
    def _():
        tail_ref[...] = jnp.zeros_like(tail_ref)

    x = x_ref[...]
    tm = x.shape[0]
    h = (_normalize(x) * g_ref[...]).astype(BF16)
    rid = lax.broadcasted_iota(jnp.int32, (8, FFN_CHUNK), 0)
    for j in range(FFN_DIM // FFN_CHUNK):
        cs = slice(j * FFN_CHUNK, (j + 1) * FFN_CHUNK)
        a = _dot(h, wup_ref[:, cs])
        b = _dot(h, wup_ref[:, FFN_DIM + j * FFN_CHUNK:FFN_DIM + (j + 1) * FFN_CHUNK])
        tail = tail_ref[j]
        r1 = pltpu.roll(a, 1, 0)
        r2 = pltpu.roll(a, 2, 0)
        top1 = jnp.where(rid == 0, tail[7:8], r1[:8])
        top2 = jnp.where(rid == 0, tail[6:7], jnp.where(rid == 1, tail[7:8], r2[:8]))
        a1 = jnp.concatenate([top1, r1[8:]], axis=0)
        a2 = jnp.concatenate([top2, r2[8:]], axis=0)
        tail_ref[j] = a[tm - 8:]
        ac = a2 * cw_ref[0:1, cs] + a1 * cw_ref[1:2, cs] + a * cw_ref[2:3, cs] + cb_ref[:, cs]
        act_ref[:, cs] = (ac * _sigmoid(ac) * b).astype(BF16)
    y = x + _dot(act_ref[...], wd_ref[...])
    if final_norm:
        y = _normalize(y) * gf_ref[...]
    out_ref[...] = y


def _ffn(x, g, w_up, conv_w, conv_b, w_down, g_final, batch, tm, final_norm):
    n, d = x.shape
    nt = n // batch // tm
    row = lambda b, i: (b * nt + i, 0)
    once = pl.Buffered(1)
    return pl.pallas_call(
        functools.partial(_ffn_kernel, final_norm=final_norm),
        out_shape=jax.ShapeDtypeStruct((n, d), F32),
        grid=(batch, nt),
        in_specs=[pl.BlockSpec((tm, d), row), _const_spec((1, d)),
                  pl.BlockSpec(w_up.shape, lambda b, i: (0, 0), pipeline_mode=once),
                  _const_spec(conv_w.shape), _const_spec(conv_b.shape),
                  pl.BlockSpec(w_down.shape, lambda b, i: (0, 0), pipeline_mode=once),
                  _const_spec((1, d))],
        out_specs=pl.BlockSpec((tm, d), row),
        scratch_shapes=[pltpu.VMEM((tm, FFN_DIM), BF16),
                        pltpu.VMEM((FFN_DIM // FFN_CHUNK, 8, FFN_CHUNK), F32)],
        compiler_params=_params(("parallel", "arbitrary")),
        name="conv_ffn",
    )(x, g, w_up, conv_w, conv_b, w_down, g_final)


def _proj_b_kernel(x_ref, gkv_ref, gmix_ref, wkv_ref, wqt_ref, wgt_ref, wmq_ref,
                   cmp_ref, kvs_ref, qt_ref, gt_ref, mq_ref):
    xn = _normalize(x_ref[...])
    hkv = (xn * gkv_ref[...]).astype(BF16)
    h1 = (xn * gmix_ref[...]).astype(BF16)
    cmp_ref[0] = _dot(hkv, wkv_ref[:, 0:KV_WIDTH]).astype(BF16)
    cmp_ref[1] = _dot(hkv, wkv_ref[:, KV_WIDTH:2 * KV_WIDTH]).astype(BF16)
    kvs_ref[...] = _dot(hkv, wkv_ref[:, 2 * KV_WIDTH:]).astype(BF16)
    qt_ref[0] = (_dot_nt(wqt_ref[...], h1) * (NSA_DH ** -0.5 * LOG2E)).astype(BF16)
    gt_ref[0] = _sigmoid(_dot_nt(wgt_ref[...], h1))
    mq_ref[...] = _dot(h1, wmq_ref[...]).astype(BF16)


def _proj_b(x, g_kv, g_mix, w_kv, wqt, wgt, wmq, batch, tm):
    n, d = x.shape
    t = n // batch
    nt = t // tm
    row = lambda b, i: (b * nt + i, 0)
    outs = [
        jax.ShapeDtypeStruct((2, n, KV_WIDTH), BF16),
        jax.ShapeDtypeStruct((n, 4 * KV_WIDTH), BF16),
        jax.ShapeDtypeStruct((batch, NSA_WIDTH, t), BF16),
        jax.ShapeDtypeStruct((batch, wgt.shape[0], t), F32),
        jax.ShapeDtypeStruct((n, MEM_WIDTH), BF16),
    ]
    return pl.pallas_call(
        _proj_b_kernel,
        out_shape=outs,
        grid=(batch, nt),
        in_specs=[pl.BlockSpec((tm, d), row), _const_spec((1, d)), _const_spec((1, d)),
                  _const_spec(w_kv.shape), _const_spec(wqt.shape), _const_spec(wgt.shape),
                  _const_spec(wmq.shape)],
        out_specs=[pl.BlockSpec((2, tm, KV_WIDTH), lambda b, i: (0, b * nt + i, 0)),
                   pl.BlockSpec((tm, 4 * KV_WIDTH), row),
                   pl.BlockSpec((1, NSA_WIDTH, tm), lambda b, i: (b, 0, i)),
                   pl.BlockSpec((1, wgt.shape[0], tm), lambda b, i: (b, 0, i)),
                   pl.BlockSpec((tm, MEM_WIDTH), row)],
        compiler_params=_params(("parallel", "parallel")),
        name="proj_b",
    )(x, g_kv, g_mix, w_kv, wqt, wgt, wmq)


def _compress_kernel(a_ref, w1_ref, pe_ref, w2_ref, o_ref):
    a = a_ref[0, 0]
    w1 = w1_ref[0]
    u = _dot(a, w1)
    cpe = _dot(pe_ref[0], w1)
    nrow = a.shape[0]
    hid = []
    for g in range(NSA_GROUPS):
        c0 = slice((2 * g) * CMP_HIDDEN, (2 * g + 1) * CMP_HIDDEN)
        c1 = slice((2 * g + 1) * CMP_HIDDEN, (2 * g + 2) * CMP_HIDDEN)
        nxt = pltpu.roll(u[:, c1], nrow - 1, 0)
        hid.append(u[:, c0] + nxt + cpe[0:1, c0] + cpe[8:9, c1])
    hcat = jax.nn.gelu(jnp.concatenate(hid, axis=-1), approximate=True).astype(BF16)
    o_ref[0, 0] = _dot(hcat, w2_ref[0]).astype(BF16)


def _compress(a, w1, pe, w2):
    _, batch, nsub, width = a.shape
    return pl.pallas_call(
        _compress_kernel,
        out_shape=jax.ShapeDtypeStruct((2, batch, nsub, KV_WIDTH), BF16),
        grid=(2, batch),
        in_specs=[pl.BlockSpec((1, 1, nsub, width), lambda s, b: (s, b, 0, 0)),
                  pl.BlockSpec((1,) + w1.shape[1:], lambda s, b: (s, 0, 0)),
                  pl.BlockSpec((1,) + pe.shape[1:], lambda s, b: (s, 0, 0)),
                  pl.BlockSpec((1,) + w2.shape[1:], lambda s, b: (s, 0, 0))],
        out_specs=pl.BlockSpec((1, 1, nsub, KV_WIDTH), lambda s, b: (s, b, 0, 0)),
        compiler_params=_params(("parallel", "parallel")),
        name="compress",
    )(a, w1, pe, w2)


def _alibi_tiles(ncmp_pad):
    slopes = LOG2E * 2.0 ** (-8.0 * np.arange(1, NSA_HEADS + 1) / NSA_HEADS)
    tl = np.tile(np.arange(Q_BLOCK), NSA_HPG)[None, :]
    cmp_end = (np.arange(ncmp_pad) * CMP_STRIDE + CMP_BLOCK - 1)[:, None]
    key = np.arange(SEL_BLOCK)[:, None]
    dist_win = tl + WINDOW - np.arange(WIN_KEYS)[:, None]
    c_cmp = np.zeros((NSA_GROUPS, ncmp_pad, NSA_HPG * Q_BLOCK), np.float32)
    c_blk = np.zeros((NSA_GROUPS, SEL_BLOCK, NSA_HPG * Q_BLOCK), np.float32)
    c_win = np.zeros((NSA_GROUPS, WIN_KEYS, NSA_HPG * Q_BLOCK), np.float32)
    srow = np.zeros((NSA_GROUPS, 1, NSA_HPG * Q_BLOCK), np.float32)
    for g in range(NSA_GROUPS):
        s = np.repeat(slopes[g * NSA_HPG:(g + 1) * NSA_HPG], Q_BLOCK)[None, :]
        c_cmp[g] = -s * (tl - cmp_end)
        c_blk[g] = -s * (tl - key)
        c_win[g] = np.where((dist_win >= 0) & (dist_win < WINDOW), -s * dist_win, -np.inf)
        srow[g] = s
    d_cmp = (cmp_end - tl).astype(np.float32)
    return c_cmp, d_cmp, c_blk, c_win, srow


def _group_queries(qt_ref, g):
    zeros = jnp.zeros((NSA_DH, Q_BLOCK), BF16)
    cols = []
    for hh in range(NSA_HPG):
        h = g * NSA_HPG + hh
        piece = qt_ref[0, h * NSA_DH:(h + 1) * NSA_DH, :]
        cols.append(jnp.concatenate([piece, zeros] if g == 0 else [zeros, piece], axis=0))
    return jnp.concatenate(cols, axis=1)


def _nsa_a_kernel(qt_ref, ckv_ref, ccmp_ref, dcmp_ref, srow_ref, ovt_ref, oc_ref, sel_ref, cnt_ref, imp_ref):
    qb = pl.program_id(1)
    base = (qb * Q_BLOCK).astype(F32)
    nblk = sel_ref.shape[3]
    jidx = lax.broadcasted_iota(jnp.int32, (nblk, Q_BLOCK), 0)
    tl = lax.broadcasted_iota(jnp.int32, (nblk, Q_BLOCK), 1)
    cur = 2 * qb + jnp.where(tl >= SEL_BLOCK, 1, 0)
    valid_blk = jidx <= cur
    n_chunks = (qb * (Q_BLOCK // CMP_STRIDE) + (Q_BLOCK - CMP_BLOCK) // CMP_STRIDE) // CMP_CHUNK + 1
    groups = range(NSA_GROUPS)
    qts = [_group_queries(qt_ref, g) for g in groups]
    shifts = [srow_ref[g] * base for g in groups]

    def cmp_branch(n_rows):
        kc = ckv_ref[0, 0, 0:n_rows, :]
        vc = ckv_ref[1, 0, 0:n_rows, :]
        ov = ovt_ref[:, 0:n_rows]
        ov2 = jnp.concatenate([ov, ov], axis=1)
        raws = [_dot(kc, qts[g]) for g in groups]
        visible = dcmp_ref[0:n_rows, :] <= base
        for g in groups:
            s = raws[g] + jnp.where(visible, ccmp_ref[g, 0:n_rows, :] - shifts[g], NEG_INF)
            m = jnp.max(s, axis=0, keepdims=True)
            e = jnp.exp2(s - jnp.where(m == NEG_INF, 0.0, m))
            inv = 1.0 / jnp.maximum(jnp.sum(e, axis=0, keepdims=True), 1e-30)
            e1 = e.astype(BF16)
            e2 = (e - e1.astype(F32)).astype(BF16)
            oc = _dot_tn(vc, e1)[g * NSA_DH:(g + 1) * NSA_DH] * inv
            imp = _dot(ov2, jnp.concatenate([e1, e2], axis=0)) * inv
            for hh in range(NSA_HPG):
                h = g * NSA_HPG + hh
                oc_ref[0, h * NSA_DH:(h + 1) * NSA_DH, :] = oc[:, hh * Q_BLOCK:(hh + 1) * Q_BLOCK]
            imp_g = imp[:, 0:Q_BLOCK]
            for hh in range(1, NSA_HPG):
                imp_g = imp_g + imp[:, hh * Q_BLOCK:(hh + 1) * Q_BLOCK]
            imp_ref[g] = imp_g

    for nc in range(1, ckv_ref.shape[2] // CMP_CHUNK + 1):
        pl.when(n_chunks == nc)(functools.partial(cmp_branch, nc * CMP_CHUNK))
    imps = [imp_ref[g] for g in groups]

    def emit(g, sel):
        sel_ref[0, g, 0] = sel
        cnt_ref[0, g, 0] = _dot_nt(jnp.ones((8, Q_BLOCK), BF16), sel.astype(BF16))

    @pl.when(qb < RANK_FIRST_TILE)
    def _():
        for g in groups:
            emit(g, jnp.where(valid_blk, 1.0, 0.0))

    @pl.when(qb >= RANK_FIRST_TILE)
    def _():
        jf = jidx.astype(F32)
        cand = (jidx >= 1) & (jidx <= cur - 2)
        v0 = [jnp.where(cand, imps[g], NEG_INF) for g in groups]
        v = list(v0)
        for _ in range(SEL_TOPN - 3):
            for g in groups:
                mx = jnp.max(v[g], axis=0, keepdims=True)
                first = jnp.min(jnp.where(v[g] == mx, jf, float(nblk)), axis=0, keepdims=True)
                v[g] = jnp.where(jf == first, NEG_INF, v[g])
        always = (jidx == 0) | (jidx == cur) | (jidx == cur - 1)
        for g in groups:
            emit(g, jnp.where((v[g] != v0[g]) | always, 1.0, 0.0))


def _nsa_a(qt, ckv, c_cmp, d_cmp, srow, ovt):
    batch, _, t = qt.shape
    nqb = t // Q_BLOCK
    ns = t // SEL_BLOCK
    ncmp = ckv.shape[2]
    outs = [jax.ShapeDtypeStruct((batch, NSA_WIDTH, t), F32),
            jax.ShapeDtypeStruct((batch, NSA_GROUPS, nqb, ns, Q_BLOCK), F32),
            jax.ShapeDtypeStruct((batch, NSA_GROUPS, nqb, 8, ns), F32)]
    return pl.pallas_call(
        _nsa_a_kernel,
        out_shape=outs,
        grid=(batch, nqb),
        in_specs=[pl.BlockSpec((1, NSA_WIDTH, Q_BLOCK), lambda b, i: (b, 0, i)),
                  pl.BlockSpec((2, 1, ncmp, KV_WIDTH), lambda b, i: (0, b, 0, 0)),
                  _const_spec(c_cmp.shape), _const_spec(d_cmp.shape), _const_spec(srow.shape),
                  _const_spec(ovt.shape)],
        out_specs=[pl.BlockSpec((1, NSA_WIDTH, Q_BLOCK), lambda b, i: (b, 0, i)),
                   pl.BlockSpec((1, NSA_GROUPS, 1, ns, Q_BLOCK), lambda b, i: (b, 0, i, 0, 0)),
                   pl.BlockSpec((1, NSA_GROUPS, 1, 8, ns), lambda b, i: (b, 0, i, 0, 0))],
        scratch_shapes=[pltpu.VMEM((NSA_GROUPS, ns, Q_BLOCK), F32)],
        compiler_params=_params(("parallel", "parallel")),
        name="nsa_select",
    )(qt, ckv, c_cmp, d_cmp, srow, ovt)


def _block_softmax_step(carry, raw, consts, shifts, v_rows, g):
    m, l, acc = carry
    width = raw.shape[1]
    fold = lambda x: x.reshape(SEL_BLOCK // 8, 8, width)
    us = [raw[i * SEL_BLOCK:(i + 1) * SEL_BLOCK] + c for i, c in enumerate(consts)]
    top8 = None
    for u, sh in zip(us, shifts):
        t = jnp.max(fold(u), axis=0) + sh
        top8 = t if top8 is None else jnp.maximum(top8, t)
    m_new = jnp.maximum(m, jnp.max(top8, axis=0, keepdims=True))
    m_safe = jnp.where(m_new == NEG_INF, 0.0, m_new)
    alpha = jnp.where(m == NEG_INF, 0.0, jnp.exp2(m - m_safe))
    ps = [jnp.exp2((u - (m_safe - sh)).astype(BF16)) for u, sh in zip(us, shifts)]
    lane_grp = lax.broadcasted_iota(jnp.int32, v_rows.shape, 1) // NSA_DH
    v_aug = jnp.where(lane_grp == g, v_rows, jnp.ones_like(v_rows))
    pv = _dot_tn(v_aug, jnp.concatenate(ps, axis=0))
    og = 1 - g
    return m_new, alpha * l + pv[og * NSA_DH:og * NSA_DH + 1], alpha * acc + pv[g * NSA_DH:(g + 1) * NSA_DH]


def _nsa_b_kernel(idx_ref, qt_ref, ks_ref, vs_ref, kw_ref, vw_ref, sel_ref, oc_ref, gt_ref, cblk_ref, cwin_ref,
                  srow_ref, o_ref, *, idx_words):
    b = pl.program_id(0)
    qb = pl.program_id(1)
    base = qb * Q_BLOCK
    width = NSA_HPG * Q_BLOCK
    groups = range(NSA_GROUPS)
    qts = [_group_queries(qt_ref, g) for g in groups]
    srows = [srow_ref[g] for g in groups]
    entries = [((b * NSA_GROUPS + g) * pl.num_programs(1) + qb) * idx_words for g in groups]
    n_act = [idx_ref[entries[g]] for g in groups]
    init = (jnp.full((1, width), NEG_INF, F32), jnp.zeros((1, width), F32), jnp.zeros((NSA_DH, width), F32))
    zero_row = jnp.zeros((1, width), F32)

    def gate_row(g, j, live):
        picked = sel_ref[0, g, 0, pl.ds(j, 1), :]
        gate = jnp.where((picked > 0.0) & live, 0.0, NEG_INF)
        return jnp.concatenate([gate] * NSA_HPG, axis=1)

    def stage(it):
        out = []
        for g in groups:
            words = [idx_ref[entries[g] + 1 + it * SLC_WORDS + w] for w in range(SLC_WORDS)]
            kts, vts, shifts = [], [], []
            for i in range(SLC_CHUNK):
                live = it * SLC_CHUNK + i < n_act[g]
                j = jnp.where(live, lax.shift_right_logical(words[i // 4], 8 * (i % 4)) & 0xFF, 0)
                k0 = pl.multiple_of(j * SEL_BLOCK, SEL_BLOCK)
                kts.append(ks_ref[pl.ds(k0, SEL_BLOCK), :])
                vts.append(vs_ref[pl.ds(k0, SEL_BLOCK), :])
                shifts.append(gate_row(g, j, live) - srows[g] * (base - k0).astype(F32))
            out.append((_dot(jnp.concatenate(kts, axis=0), qts[g]), tuple(shifts), jnp.concatenate(vts, axis=0)))
        return tuple(out)

    def body(it, states):
        staged = stage(it)
        return tuple(_block_softmax_step(states[g], staged[g][0], [cblk_ref[g]] * SLC_CHUNK, staged[g][1],
                                         staged[g][2], g) for g in groups)

    n_it = (jnp.maximum(n_act[0], n_act[1]) + SLC_CHUNK - 1) // SLC_CHUNK
    slc = list(lax.fori_loop(0, n_it, body, (init,) * NSA_GROUPS))

    diag = range(WINDOW // SEL_BLOCK, WIN_KEYS // SEL_BLOCK)
    r0 = pl.multiple_of(base, Q_BLOCK)
    n_win = WIN_KEYS // SEL_BLOCK
    win = [init] * NSA_GROUPS
    steps = []
    for g in groups:
        shifts = [gate_row(g, 2 * qb + i, True) for i in range(len(diag))]
        steps.append((slc, g, ks_ref[pl.ds(r0, Q_BLOCK), :], vs_ref[pl.ds(r0, Q_BLOCK), :], diag, shifts))
    for c in range(WIN_KEYS // WIN_CHUNK):
        blocks = range(c * n_win // 2, (c + 1) * n_win // 2)
        kws, vws, offs = [], [], []
        for r in blocks:
            j = 2 * qb - WINDOW // SEL_BLOCK + r
            k0 = pl.multiple_of(jnp.maximum(j, 0) * SEL_BLOCK, SEL_BLOCK)
            kws.append(kw_ref[pl.ds(k0, SEL_BLOCK), :])
            vws.append(vw_ref[pl.ds(k0, SEL_BLOCK), :])
            offs.append(zero_row + jnp.where(j >= 0, 0.0, NEG_INF))
        kw = jnp.concatenate(kws, axis=0)
        vw = jnp.concatenate(vws, axis=0)
        for g in groups:
            steps.append((win, g, kw, vw, blocks, offs))
    raws = [_dot(keys, qts[g]) for _, g, keys, _, _, _ in steps]
    for raw, (state, g, _, values, blocks, shifts) in zip(raws, steps):
        consts = [cwin_ref[g, r * SEL_BLOCK:(r + 1) * SEL_BLOCK, :] for r in blocks]
        state[g] = _block_softmax_step(state[g], raw, consts, shifts, values, g)

    heads_out = []
    for g in groups:
        o_s = slc[g][2] * (1.0 / jnp.maximum(slc[g][1], 1e-30))
        o_w = win[g][2] * (1.0 / jnp.maximum(win[g][1], 1e-30))
        for hh in range(NSA_HPG):
            h = g * NSA_HPG + hh
            cs = slice(hh * Q_BLOCK, (hh + 1) * Q_BLOCK)
            gates = gt_ref[0, 3 * h:3 * h + 3, :]
            heads_out.append(gates[0:1] * oc_ref[0, h * NSA_DH:(h + 1) * NSA_DH, :]
                             + gates[1:2] * o_s[:, cs] + gates[2:3] * o_w[:, cs])
    o_ref[...] = jnp.concatenate(heads_out, axis=0).T.astype(BF16)


def _nsa_b(idx, qt, kvs, sel, oc, gt, c_blk, c_win, srow):
    batch, _, t = qt.shape
    nqb = t // Q_BLOCK
    ns = t // SEL_BLOCK
    kv_spec = lambda c: pl.BlockSpec((t, KV_WIDTH), lambda b, i, idx_ref: (b, c))
    grid_spec = pltpu.PrefetchScalarGridSpec(
        num_scalar_prefetch=1,
        grid=(batch, nqb),
        in_specs=[pl.BlockSpec((1, NSA_WIDTH, Q_BLOCK), lambda b, i, r: (b, 0, i)),
                  kv_spec(0), kv_spec(1), kv_spec(2), kv_spec(3),
                  pl.BlockSpec((1, NSA_GROUPS, 1, ns, Q_BLOCK), lambda b, i, r: (b, 0, i, 0, 0)),
                  pl.BlockSpec((1, NSA_WIDTH, Q_BLOCK), lambda b, i, r: (b, 0, i)),
                  pl.BlockSpec((1, gt.shape[1], Q_BLOCK), lambda b, i, r: (b, 0, i)),
                  pl.BlockSpec(c_blk.shape, lambda b, i, r: (0, 0, 0)),
                  pl.BlockSpec(c_win.shape, lambda b, i, r: (0, 0, 0)),
                  pl.BlockSpec(srow.shape, lambda b, i, r: (0, 0, 0))],
        out_specs=pl.BlockSpec((Q_BLOCK, NSA_WIDTH), lambda b, i, r: (b * nqb + i, 0)),
    )
    return pl.pallas_call(
        functools.partial(_nsa_b_kernel, idx_words=1 + _slc_steps(ns) * SLC_WORDS),
        out_shape=jax.ShapeDtypeStruct((batch * t, NSA_WIDTH), BF16),
        grid_spec=grid_spec,
        compiler_params=_params(("parallel", "parallel")),
        name="nsa_attend",
    )(idx, qt, kvs, kvs, kvs, kvs, sel, oc, gt, c_blk, c_win, srow)


def _pad_cols(w, width):
    return jnp.pad(w, ((0, 0), (0, width - w.shape[1])))


def _compress_weights(w1, pe, w2):
    nsub = CMP_STRIDE
    w1r = w1.reshape(2, nsub, NSA_DH, CMP_HIDDEN)
    same_group = jnp.eye(NSA_GROUPS, dtype=F32)
    big = same_group[None, :, None, :, None, None] * w1r.transpose(1, 2, 0, 3)[:, None, :, None, :, :]
    big = big.reshape(nsub * KV_WIDTH, NSA_GROUPS * 2 * CMP_HIDDEN)
    per = pe.reshape(2, nsub, 1, NSA_DH)
    pe2 = jnp.broadcast_to(per, (2, nsub, NSA_GROUPS, NSA_DH)).reshape(2, 1, nsub * KV_WIDTH)
    pe2 = jnp.broadcast_to(pe2, (2, 8, nsub * KV_WIDTH)).reshape(16, nsub * KV_WIDTH)
    w2bd = same_group[:, None, :, None] * w2[None, :, None, :]
    return big.astype(BF16), pe2.astype(BF16), w2bd.reshape(NSA_GROUPS * CMP_HIDDEN, KV_WIDTH).astype(BF16)


def _selection_overlap_t(ncmp_pad, ns):
    cs = np.arange(ncmp_pad) * CMP_STRIDE
    ss = np.arange(ns) * SEL_BLOCK
    ov = np.minimum(cs[:, None] + CMP_BLOCK, ss[None, :] + SEL_BLOCK) - np.maximum(cs[:, None], ss[None, :])
    return (np.clip(ov, 0, None).astype(np.float32) / CMP_BLOCK).T


def _slc_steps(ns):
    return -(-ns // SLC_CHUNK)


def _active_blocks(cnt):
    ns = cnt.shape[-1]
    before_tile = jnp.arange(ns)[None, :] < (Q_BLOCK // SEL_BLOCK) * jnp.arange(cnt.shape[2])[:, None]
    flags = (cnt[:, :, :, 0, :] > 0.0) & before_tile
    order = jnp.argsort(jnp.where(flags, 0, 1).astype(jnp.int32), axis=-1, stable=True).astype(jnp.int32)
    lead = order.shape[:-1]
    steps = _slc_steps(ns)
    order = jnp.pad(order, [(0, 0)] * len(lead) + [(0, steps * SLC_CHUNK - ns)]).reshape(lead + (steps, SLC_CHUNK))
    order = jnp.pad(order, [(0, 0)] * (len(lead) + 1) + [(0, 4 * SLC_WORDS - SLC_CHUNK)])
    packed = jnp.sum(order.reshape(lead + (steps * SLC_WORDS, 4)) << (8 * jnp.arange(4, dtype=jnp.int32)), axis=-1)
    n_act = jnp.sum(flags, axis=-1, dtype=jnp.int32)
    return jnp.concatenate([n_act[..., None], packed], axis=-1).reshape(-1)


def kernel(x, mem, g_mix, g_ffn, g_mem, w_mem_kv, w_up, conv_w, conv_b, w_down,
           a_w_in, a_w_alpha, a_b_alpha, a_g_head, a_w_out,
           g_kv, w_kv, pe_k, pe_v, w_ck1, w_ck2, w_cv1, w_cv2,
           b_w_in, b_w_out, g_final):
    batch, t, d = x.shape
    n = batch * t
    m = mem.shape[1]
    tm = min(1024, t)
    xf = x.reshape(n, d)
    memf = mem.reshape(batch * m, d)
    row = lambda v: v.reshape(1, -1)

    mkv0 = _rms_proj(memf, row(g_mem[0]), w_mem_kv[0].astype(BF16), m, BF16).reshape(batch, m, 2 * MEM_WIDTH)
    wa = a_w_in[0]
    c_alr = 2 * GLA_QK + 2 * GLA_V
    w_a = jnp.concatenate([wa[:, :c_alr], wa[:, c_alr + GLA_RANK:], _pad_cols(wa[:, c_alr:c_alr + GLA_RANK], LANE)],
                          axis=1).astype(BF16)
    w_alpha = jnp.pad(a_w_alpha[0], ((0, LANE - GLA_RANK), (0, 0))).astype(BF16)
    q, k, gl, v, r, mq = _inproj_a(xf, row(g_mix[0]), w_a, w_alpha, row(a_b_alpha[0]), tm)
    o = _gla(q, k, gl, v, r, row(a_g_head[0]), batch, tm)
    xf = _attn_out(o, mq, mkv0, a_w_out[0].astype(BF16), xf, batch, tm)
    xf = _ffn(xf, row(g_ffn[0]), w_up[0].astype(BF16), conv_w[0], row(conv_b[0]), w_down[0].astype(BF16),
              row(g_final), batch, tm, final_norm=False)

    wb = b_w_in[0]
    wqt = wb[:, :NSA_WIDTH].T.astype(BF16)
    n_gate = 3 * NSA_HEADS
    wgt = jnp.pad(wb[:, NSA_WIDTH:NSA_WIDTH + n_gate].T, ((0, 32 - n_gate), (0, 0))).astype(BF16)
    wmq = wb[:, NSA_WIDTH + n_gate:].astype(BF16)
    ckv_in, kvs, qt, gt, mq1 = _proj_b(xf, row(g_kv), row(g_mix[1]), w_kv.astype(BF16), wqt, wgt, wmq, batch, tm)
    nsub = t // CMP_STRIDE
    w1k, pek, w2k = _compress_weights(w_ck1, pe_k, w_ck2)
    w1v, pev, w2v = _compress_weights(w_cv1, pe_v, w_cv2)
    ckv = _compress(ckv_in.reshape(2, batch, nsub, CMP_STRIDE * KV_WIDTH),
                    jnp.stack([w1k, w1v]), jnp.stack([pek, pev]), jnp.stack([w2k, w2v]))

    ns = t // SEL_BLOCK
    c_cmp, d_cmp, c_blk, c_win, srow = _alibi_tiles(nsub)
    ovt = jnp.asarray(_selection_overlap_t(nsub, ns), BF16)
    oc, sel, cnt = _nsa_a(qt, ckv, jnp.asarray(c_cmp), jnp.asarray(d_cmp), jnp.asarray(srow), ovt)
    idx = _active_blocks(cnt)
    o1 = _nsa_b(idx, qt, kvs, sel, oc, gt, jnp.asarray(c_blk), jnp.asarray(c_win), jnp.asarray(srow))
    mkv1 = _rms_proj(memf, row(g_mem[1]), w_mem_kv[1].astype(BF16), m, BF16).reshape(batch, m, 2 * MEM_WIDTH)
    xf = _attn_out(o1, mq1, mkv1, b_w_out[0].astype(BF16), xf, batch, tm)
    xf = _ffn(xf, row(g_ffn[1]), w_up[1].astype(BF16), conv_w[1], row(conv_b[1]), w_down[1].astype(BF16),
              row(g_final), batch, tm, final_norm=True)
    return xf.reshape(batch, t, d)
```

```python
import functools

import numpy as np
import jax
import jax.numpy as jnp
from jax import lax
from jax.experimental import pallas as pl
from jax.experimental.pallas import tpu as pltpu

F32 = jnp.float32
BF16 = jnp.bfloat16
EPS = 1e-6
NEG_INF = float("-inf")
LOG2E = 1.4426950408889634

V7X_VMEM_BYTES = 64 * 1024 * 1024
VMEM_LIMIT = V7X_VMEM_BYTES - 8 * 1024 * 1024

D_MODEL = 1024
MEM_HEADS = 4
MEM_DH = 128
MEM_WIDTH = MEM_HEADS * MEM_DH
GLA_HEADS = 4
GLA_DK = 64
GLA_DV = 128
GLA_QK = GLA_HEADS * GLA_DK
GLA_V = GLA_HEADS * GLA_DV
GLA_RANK = 16
GLA_TAU = 16.0
GLA_CHUNK = 64
GLA_LEVELS = 6
NSA_HEADS = 8
NSA_GROUPS = 2
NSA_HPG = NSA_HEADS // NSA_GROUPS
NSA_DH = 64
NSA_WIDTH = NSA_HEADS * NSA_DH
KV_WIDTH = NSA_GROUPS * NSA_DH
CMP_BLOCK = 32
CMP_STRIDE = 16
CMP_HIDDEN = 256
SEL_BLOCK = 64
SEL_TOPN = 16
WINDOW = 512
Q_BLOCK = 128
WIN_KEYS = WINDOW + Q_BLOCK
WIN_CHUNK = WIN_KEYS // 2
CMP_CHUNK = 128
SLC_CHUNK = 6
SLC_WORDS = -(-SLC_CHUNK // 4)
RANK_FIRST_TILE = SEL_TOPN * SEL_BLOCK // Q_BLOCK
SEL_TILES = 2
FFN_DIM = 2816
FFN_CHUNK = 256
LANE = 128


def _dot(a, b):
    return jnp.dot(a, b, preferred_element_type=F32)


def _dot_nt(a, b):
    return lax.dot_general(a, b, (((1,), (1,)), ((), ())), preferred_element_type=F32)


def _dot_tn(a, b):
    return lax.dot_general(a, b, (((0,), (0,)), ((), ())), preferred_element_type=F32)


def _params(sem):
    return pltpu.CompilerParams(dimension_semantics=sem, vmem_limit_bytes=VMEM_LIMIT)


def _const_spec(shape):
    n = len(shape)
    return pl.BlockSpec(shape, lambda *_: (0,) * n)


def _normalize(x):
    return x * lax.rsqrt(jnp.mean(x * x, axis=-1, keepdims=True) + EPS)


def _sigmoid(x):
    return 1.0 / (1.0 + jnp.exp(-x))


def _rms_proj_kernel(x_ref, g_ref, w_ref, o_ref):
    h = (_normalize(x_ref[...]) * g_ref[...]).astype(BF16)
    o_ref[...] = _dot(h, w_ref[...]).astype(o_ref.dtype)


def _rms_proj(x, g, w, tm, out_dtype):
    n, d = x.shape
    p = w.shape[1]
    return pl.pallas_call(
        _rms_proj_kernel,
        out_shape=jax.ShapeDtypeStruct((n, p), out_dtype),
        grid=(n // tm,),
        in_specs=[pl.BlockSpec((tm, d), lambda i: (i, 0)), _const_spec((1, d)), _const_spec((d, p))],
        out_specs=pl.BlockSpec((tm, p), lambda i: (i, 0)),
        compiler_params=_params(("parallel",)),
        name="rms_proj",
    )(x, g, w)


def _inproj_a_kernel(x_ref, g_ref, w_ref, wa_ref, ba_ref, q_ref, k_ref, gl_ref, v_ref, r_ref, mq_ref):
    h = (_normalize(x_ref[...]) * g_ref[...]).astype(BF16)
    c = 0
    q_ref[...] = _dot(h, w_ref[:, c:c + GLA_QK]) * (GLA_DK ** -0.5)
    c += GLA_QK
    k_ref[...] = _dot(h, w_ref[:, c:c + GLA_QK])
    c += GLA_QK
    v_ref[...] = _dot(h, w_ref[:, c:c + GLA_V]).astype(BF16)
    c += GLA_V
    r_ref[...] = _dot(h, w_ref[:, c:c + GLA_V])
    c += GLA_V
    mq_ref[...] = _dot(h, w_ref[:, c:c + MEM_WIDTH]).astype(BF16)
    c += MEM_WIDTH
    alr = _dot(h, w_ref[:, c:c + LANE]).astype(BF16)
    z = _dot(alr, wa_ref[...]) + ba_ref[...]
    log_sig = jnp.minimum(z, 0.0) - jnp.log1p(jnp.exp(-jnp.abs(z)))
    gl_ref[...] = log_sig * (1.0 / GLA_TAU)


def _inproj_a(x, g, w, wa, ba, tm):
    n, d = x.shape
    row = lambda i: (i, 0)
    outs = [
        jax.ShapeDtypeStruct((n, GLA_QK), F32), jax.ShapeDtypeStruct((n, GLA_QK), F32),
        jax.ShapeDtypeStruct((n, GLA_QK), F32), jax.ShapeDtypeStruct((n, GLA_V), BF16),
        jax.ShapeDtypeStruct((n, GLA_V), F32), jax.ShapeDtypeStruct((n, MEM_WIDTH), BF16),
    ]
    return pl.pallas_call(
        _inproj_a_kernel,
        out_shape=outs,
        grid=(n // tm,),
        in_specs=[pl.BlockSpec((tm, d), row), _const_spec((1, d)), _const_spec(w.shape),
                  _const_spec(wa.shape), _const_spec(ba.shape)],
        out_specs=[pl.BlockSpec((tm, s.shape[1]), row) for s in outs],
        compiler_params=_params(("parallel",)),
        name="inproj_a",
    )(x, g, w, wa, ba)


def _gla_constants():
    c = GLA_CHUNK
    w = np.zeros((8 * c, c), np.float32)
    masks = np.zeros((GLA_LEVELS + 1, c, c), np.float32)
    masks[0] = np.eye(c)
    for l in range(1, GLA_LEVELS + 1):
        blk, half = 2 ** l, 2 ** (l - 1)
        for i in range(c):
            mid = (i // blk) * blk + half - 1
            if i % blk >= half:
                w[(l - 1) * c + i, mid + 1:i + 1] = 1.0
            else:
                w[(l - 1) * c + i, i + 1:mid + 1] = 1.0
        for t in range(c):
            for s in range(c):
                if t // blk == s // blk and t % blk >= half and s % blk < half:
                    masks[l, t, s] = 1.0
    for i in range(c):
        w[6 * c + i, :i + 1] = 1.0
        w[7 * c + i, i + 1:] = 1.0
    return np.concatenate([w, w], axis=1), np.tile(masks, (1, 1, GLA_HEADS))


def _gla_kernel(q_ref, k_ref, gl_ref, v_ref, r_ref, gh_ref, wcat_ref, msk_ref, o_ref, st_ref, *, n_chunks):
    c = GLA_CHUNK

    @pl.when(pl.program_id(1) == 0)
    def _():
        st_ref[...] = jnp.zeros_like(st_ref)

    lane_qk = lax.broadcasted_iota(jnp.int32, (1, GLA_QK), 1) // GLA_DK
    lane_v = lax.broadcasted_iota(jnp.int32, (1, GLA_V), 1) // GLA_DV
    row_qk = lax.broadcasted_iota(jnp.int32, (GLA_QK, 1), 0) // GLA_DK
    ones = jnp.ones((2 * c, LANE), BF16)
    wcat = wcat_ref[...]

    def stack_heads(x, lane_head):
        return jnp.concatenate([jnp.where(lane_head == h, x, jnp.zeros_like(x)) for h in range(GLA_HEADS)], axis=0)

    chunks = [slice(ci * c, (ci + 1) * c) for ci in range(n_chunks)]
    es, decs = [], []
    for rows in chunks:
        g = gl_ref[rows, :] * LOG2E
        g_hi = g.astype(BF16)
        g_split = jnp.concatenate([g_hi, (g - g_hi.astype(F32)).astype(BF16)], axis=0)
        es.append(jnp.exp2(_dot(wcat, g_split)))
        decs.append(jnp.exp2(_dot_tn(g_split, ones)))

    o_intras, q_ins, upds = [], [], []
    for rows, e in zip(chunks, es):
        q = q_ref[rows, :]
        k = k_ref[rows, :]
        v = v_ref[rows, :]
        attn = jnp.zeros((c, GLA_HEADS * c), F32)
        for l in range(GLA_LEVELS + 1):
            if l == 0:
                ql, kl = q, k
            else:
                el = e[(l - 1) * c:l * c]
                ql, kl = q * el, k * el
            attn = attn + _dot_nt(ql.astype(BF16), stack_heads(kl, lane_qk).astype(BF16)) * msk_ref[l]
        o_intras.append(_dot(attn.astype(BF16), stack_heads(v, lane_v)))
        q_ins.append((q * e[6 * c:7 * c]).astype(BF16))
        kk = stack_heads(k * e[7 * c:8 * c], lane_qk).astype(BF16)
        v_rows = jnp.concatenate([v[:, h * GLA_DV:(h + 1) * GLA_DV] for h in range(GLA_HEADS)], axis=0)
        upds.append(_dot_tn(kk, v_rows))

    st = st_ref[...]
    outs = []
    for o_intra, q_in, dec, upd in zip(o_intras, q_ins, decs, upds):
        st_bd = jnp.concatenate([jnp.where(row_qk == h, st, 0.0).astype(BF16) for h in range(GLA_HEADS)], axis=1)
        outs.append(o_intra + _dot(q_in, st_bd))
        st = dec * st + upd
    st_ref[...] = st

    for rows, o in zip(chunks, outs):
        for h in range(GLA_HEADS):
            cs = slice(h * GLA_DV, (h + 1) * GLA_DV)
            on = _normalize(o[:, cs]) * gh_ref[...]
            r = r_ref[rows, cs]
            o_ref[rows, cs] = (on * (r * _sigmoid(r))).astype(BF16)


def _gla(q, k, gl, v, r, g_head, batch, ct):
    n = q.shape[0]
    t = n // batch
    nt = t // ct
    wcat, masks = _gla_constants()
    row = lambda b, i: (b * nt + i, 0)
    return pl.pallas_call(
        functools.partial(_gla_kernel, n_chunks=ct // GLA_CHUNK),
        out_shape=jax.ShapeDtypeStruct((n, GLA_V), BF16),
        grid=(batch, nt),
        in_specs=[pl.BlockSpec((ct, GLA_QK), row), pl.BlockSpec((ct, GLA_QK), row),
                  pl.BlockSpec((ct, GLA_QK), row), pl.BlockSpec((ct, GLA_V), row),
                  pl.BlockSpec((ct, GLA_V), row), _const_spec((1, GLA_DV)),
                  _const_spec(wcat.shape), _const_spec(masks.shape)],
        out_specs=pl.BlockSpec((ct, GLA_V), row),
        scratch_shapes=[pltpu.VMEM((GLA_QK, GLA_DV), F32)],
        compiler_params=_params(("parallel", "arbitrary")),
        name="gla",
    )(q, k, gl, v, r, g_head, jnp.asarray(wcat, BF16), jnp.asarray(masks, F32))


def _attn_out_kernel(o_ref, mq_ref, mk_ref, mv_ref, w_ref, x_ref, out_ref):
    heads = [slice(h * MEM_DH, (h + 1) * MEM_DH) for h in range(MEM_HEADS)]
    scores = [_dot_nt(mq_ref[:, cs], mk_ref[0, :, cs]) * (MEM_DH ** -0.5 * LOG2E) for cs in heads]
    parts = [o_ref[...]]
    for s, cs in zip(scores, heads):
        e = jnp.exp2(s - jnp.max(s, axis=-1, keepdims=True))
        p = e / jnp.sum(e, axis=-1, keepdims=True)
        parts.append(_dot(p.astype(BF16), mv_ref[0, :, cs]).astype(BF16))
    cat = jnp.concatenate(parts, axis=-1)
    out_ref[...] = x_ref[...] + _dot(cat, w_ref[...])


def _attn_out(o, mq, mkv, w_out, x, batch, tm):
    n, d = x.shape
    t = n // batch
    nt = t // tm
    m = mkv.shape[1]
    row = lambda b, i: (b * nt + i, 0)
    return pl.pallas_call(
        _attn_out_kernel,
        out_shape=jax.ShapeDtypeStruct((n, d), F32),
        grid=(batch, nt),
        in_specs=[pl.BlockSpec((tm, o.shape[1]), row), pl.BlockSpec((tm, MEM_WIDTH), row),
                  pl.BlockSpec((1, m, MEM_WIDTH), lambda b, i: (b, 0, 0)),
                  pl.BlockSpec((1, m, MEM_WIDTH), lambda b, i: (b, 0, 1)),
                  _const_spec(w_out.shape), pl.BlockSpec((tm, d), row)],
        out_specs=pl.BlockSpec((tm, d), row),
        compiler_params=_params(("parallel", "parallel")),
        name="attn_out",
    )(o, mq, mkv, mkv, w_out, x)


def _ffn_kernel(x_ref, g_ref, wup_ref, cw_ref, cb_ref, wd_ref, gf_ref, out_ref, act_ref, tail_ref, *,
                final_norm):
    @pl.when(pl.program_id(1) == 0)
    def _():
        tail_ref[...] = jnp.zeros_like(tail_ref)

    x = x_ref[...]
    tm = x.shape[0]
    h = (_normalize(x) * g_ref[...]).astype(BF16)
    rid = lax.broadcasted_iota(jnp.int32, (8, FFN_CHUNK), 0)
    for j in range(FFN_DIM // FFN_CHUNK):
        cs = slice(j * FFN_CHUNK, (j + 1) * FFN_CHUNK)
        a = _dot(h, wup_ref[:, cs])
        b = _dot(h, wup_ref[:, FFN_DIM + j * FFN_CHUNK:FFN_DIM + (j + 1) * FFN_CHUNK])
        tail = tail_ref[j]
        r1 = pltpu.roll(a, 1, 0)
        r2 = pltpu.roll(a, 2, 0)
        top1 = jnp.where(rid == 0, tail[7:8], r1[:8])
        top2 = jnp.where(rid == 0, tail[6:7], jnp.where(rid == 1, tail[7:8], r2[:8]))
        a1 = jnp.concatenate([top1, r1[8:]], axis=0)
        a2 = jnp.concatenate([top2, r2[8:]], axis=0)
        tail_ref[j] = a[tm - 8:]
        ac = a2 * cw_ref[0:1, cs] + a1 * cw_ref[1:2, cs] + a * cw_ref[2:3, cs] + cb_ref[:, cs]
        act_ref[:, cs] = (ac * _sigmoid(ac) * b).astype(BF16)
    y = x + _dot(act_ref[...], wd_ref[...])
    if final_norm:
        y = _normalize(y) * gf_ref[...]
    out_ref[...] = y


def _ffn(x, g, w_up, conv_w, conv_b, w_down, g_final, batch, tm, final_norm):
    n, d = x.shape
    nt = n // batch // tm
    row = lambda b, i: (b * nt + i, 0)
    once = pl.Buffered(1)
    return pl.pallas_call(
        functools.partial(_ffn_kernel, final_norm=final_norm),
        out_shape=jax.ShapeDtypeStruct((n, d), F32),
        grid=(batch, nt),
        in_specs=[pl.BlockSpec((tm, d), row), _const_spec((1, d)),
                  pl.BlockSpec(w_up.shape, lambda b, i: (0, 0), pipeline_mode=once),
                  _const_spec(conv_w.shape), _const_spec(conv_b.shape),
                  pl.BlockSpec(w_down.shape, lambda b, i: (0, 0), pipeline_mode=once),
                  _const_spec((1, d))],
        out_specs=pl.BlockSpec((tm, d), row),
        scratch_shapes=[pltpu.VMEM((tm, FFN_DIM), BF16),
                        pltpu.VMEM((FFN_DIM // FFN_CHUNK, 8, FFN_CHUNK), F32)],
        compiler_params=_params(("parallel", "arbitrary")),
        name="conv_ffn",
    )(x, g, w_up, conv_w, conv_b, w_down, g_final)


def _proj_b_kernel(x_ref, gkv_ref, gmix_ref, wkv_ref, wqt_ref, wgt_ref, wmq_ref,
                   cmp_ref, kvs_ref, qt_ref, gt_ref, mq_ref):
    xn = _normalize(x_ref[...])
    hkv = (xn * gkv_ref[...]).astype(BF16)
    h1 = (xn * gmix_ref[...]).astype(BF16)
    cmp_ref[0] = _dot(hkv, wkv_ref[:, 0:KV_WIDTH]).astype(BF16)
    cmp_ref[1] = _dot(hkv, wkv_ref[:, KV_WIDTH:2 * KV_WIDTH]).astype(BF16)
    kvs_ref[...] = _dot(hkv, wkv_ref[:, 2 * KV_WIDTH:]).astype(BF16)
    qt_ref[0] = (_dot_nt(wqt_ref[...], h1) * (NSA_DH ** -0.5 * LOG2E)).astype(BF16)
    gt_ref[0] = _sigmoid(_dot_nt(wgt_ref[...], h1))
    mq_ref[...] = _dot(h1, wmq_ref[...]).astype(BF16)


def _proj_b(x, g_kv, g_mix, w_kv, wqt, wgt, wmq, batch, tm):
    n, d = x.shape
    t = n // batch
    nt = t // tm
    row = lambda b, i: (b * nt + i, 0)
    outs = [
        jax.ShapeDtypeStruct((2, n, KV_WIDTH), BF16),
        jax.ShapeDtypeStruct((n, 4 * KV_WIDTH), BF16),
        jax.ShapeDtypeStruct((batch, NSA_WIDTH, t), BF16),
        jax.ShapeDtypeStruct((batch, wgt.shape[0], t), F32),
        jax.ShapeDtypeStruct((n, MEM_WIDTH), BF16),
    ]
    return pl.pallas_call(
        _proj_b_kernel,
        out_shape=outs,
        grid=(batch, nt),
        in_specs=[pl.BlockSpec((tm, d), row), _const_spec((1, d)), _const_spec((1, d)),
                  _const_spec(w_kv.shape), _const_spec(wqt.shape), _const_spec(wgt.shape),
                  _const_spec(wmq.shape)],
        out_specs=[pl.BlockSpec((2, tm, KV_WIDTH), lambda b, i: (0, b * nt + i, 0)),
                   pl.BlockSpec((tm, 4 * KV_WIDTH), row),
                   pl.BlockSpec((1, NSA_WIDTH, tm), lambda b, i: (b, 0, i)),
                   pl.BlockSpec((1, wgt.shape[0], tm), lambda b, i: (b, 0, i)),
                   pl.BlockSpec((tm, MEM_WIDTH), row)],
        compiler_params=_params(("parallel", "parallel")),
        name="proj_b",
    )(x, g_kv, g_mix, w_kv, wqt, wgt, wmq)


def _compress_kernel(a_ref, w1_ref, pe_ref, w2_ref, o_ref):
    a = a_ref[0, 0]
    w1 = w1_ref[0]
    u = _dot(a, w1)
    cpe = _dot(pe_ref[0], w1)
    nrow = a.shape[0]
    hid = []
    for g in range(NSA_GROUPS):
        c0 = slice((2 * g) * CMP_HIDDEN, (2 * g + 1) * CMP_HIDDEN)
        c1 = slice((2 * g + 1) * CMP_HIDDEN, (2 * g + 2) * CMP_HIDDEN)
        nxt = pltpu.roll(u[:, c1], nrow - 1, 0)
        hid.append(u[:, c0] + nxt + cpe[0:1, c0] + cpe[8:9, c1])
    hcat = jax.nn.gelu(jnp.concatenate(hid, axis=-1), approximate=True).astype(BF16)
    o_ref[0, 0] = _dot(hcat, w2_ref[0]).astype(BF16)


def _compress(a, w1, pe, w2):
    _, batch, nsub, width = a.shape
    return pl.pallas_call(
        _compress_kernel,
        out_shape=jax.ShapeDtypeStruct((2, batch, nsub, KV_WIDTH), BF16),
        grid=(2, batch),
        in_specs=[pl.BlockSpec((1, 1, nsub, width), lambda s, b: (s, b, 0, 0)),
                  pl.BlockSpec((1,) + w1.shape[1:], lambda s, b: (s, 0, 0)),
                  pl.BlockSpec((1,) + pe.shape[1:], lambda s, b: (s, 0, 0)),
                  pl.BlockSpec((1,) + w2.shape[1:], lambda s, b: (s, 0, 0))],
        out_specs=pl.BlockSpec((1, 1, nsub, KV_WIDTH), lambda s, b: (s, b, 0, 0)),
        compiler_params=_params(("parallel", "parallel")),
        name="compress",
    )(a, w1, pe, w2)


def _alibi_tiles(ncmp_pad):
    slopes = LOG2E * 2.0 ** (-8.0 * np.arange(1, NSA_HEADS + 1) / NSA_HEADS)
    tl = np.tile(np.arange(Q_BLOCK), NSA_HPG)[None, :]
    cmp_end = (np.arange(ncmp_pad) * CMP_STRIDE + CMP_BLOCK - 1)[:, None]
    key = np.arange(SEL_BLOCK)[:, None]
    dist_win = tl + WINDOW - np.arange(WIN_KEYS)[:, None]
    c_cmp = np.zeros((NSA_GROUPS, ncmp_pad, NSA_HPG * Q_BLOCK), np.float32)
    c_blk = np.zeros((NSA_GROUPS, SEL_BLOCK, NSA_HPG * Q_BLOCK), np.float32)
    c_win = np.zeros((NSA_GROUPS, WIN_KEYS, NSA_HPG * Q_BLOCK), np.float32)
    srow = np.zeros((NSA_GROUPS, 1, NSA_HPG * Q_BLOCK), np.float32)
    for g in range(NSA_GROUPS):
        s = np.repeat(slopes[g * NSA_HPG:(g + 1) * NSA_HPG], Q_BLOCK)[None, :]
        c_cmp[g] = -s * (tl - cmp_end)
        c_blk[g] = -s * (tl - key)
        c_win[g] = np.where((dist_win >= 0) & (dist_win < WINDOW), -s * dist_win, -np.inf)
        srow[g] = s
    d_cmp = (cmp_end - tl).astype(np.float32)
    return c_cmp, d_cmp, c_blk, c_win, srow


def _group_queries(qt_ref, g, tile=0):
    zeros = jnp.zeros((NSA_DH, Q_BLOCK), BF16)
    cols = []
    for hh in range(NSA_HPG):
        h = g * NSA_HPG + hh
        piece = qt_ref[0, h * NSA_DH:(h + 1) * NSA_DH, tile * Q_BLOCK:(tile + 1) * Q_BLOCK]
        cols.append(jnp.concatenate([piece, zeros] if g == 0 else [zeros, piece], axis=0))
    return jnp.concatenate(cols, axis=1)


def _nsa_a_kernel(qt_ref, ckv_ref, ccmp_ref, dcmp_ref, srow_ref, ovt_ref, oc_ref, sel_ref, cnt_ref, imp_ref):
    first_tile = pl.program_id(1) * SEL_TILES
    nblk = sel_ref.shape[3]
    jidx = lax.broadcasted_iota(jnp.int32, (nblk, Q_BLOCK), 0)
    tl = lax.broadcasted_iota(jnp.int32, (nblk, Q_BLOCK), 1)
    units = [(k, g) for k in range(SEL_TILES) for g in range(NSA_GROUPS)]
    qbs = [first_tile + k for k in range(SEL_TILES)]
    bases = [(qb * Q_BLOCK).astype(F32) for qb in qbs]
    curs = [2 * qb + jnp.where(tl >= SEL_BLOCK, 1, 0) for qb in qbs]
    n_chunks = (qbs[-1] * (Q_BLOCK // CMP_STRIDE) + (Q_BLOCK - CMP_BLOCK) // CMP_STRIDE) // CMP_CHUNK + 1
    qts = {(k, g): _group_queries(qt_ref, g, k) for k, g in units}
    shifts = {(k, g): srow_ref[g] * bases[k] for k, g in units}

    def cmp_branch(n_rows):
        kc = ckv_ref[0, 0, 0:n_rows, :]
        vc = ckv_ref[1, 0, 0:n_rows, :]
        ov = ovt_ref[:, 0:n_rows]
        lane_grp = lax.broadcasted_iota(jnp.int32, vc.shape, 1) // NSA_DH
        raws = {u: _dot(kc, qts[u]) for u in units}
        for k, g in units:
            visible = dcmp_ref[0:n_rows, :] <= bases[k]
            s = raws[k, g] + jnp.where(visible, ccmp_ref[g, 0:n_rows, :] - shifts[k, g], NEG_INF)
            m = jnp.max(s, axis=0, keepdims=True)
            e = jnp.exp2((s - jnp.where(m == NEG_INF, 0.0, m)).astype(BF16))
            pv = _dot_tn(jnp.where(lane_grp == g, vc, jnp.ones_like(vc)), e)
            inv = 1.0 / jnp.maximum(pv[(1 - g) * NSA_DH:(1 - g) * NSA_DH + 1], 1e-30)
            oc = pv[g * NSA_DH:(g + 1) * NSA_DH] * inv
            imp = _dot(ov, e) * inv
            for hh in range(NSA_HPG):
                h = g * NSA_HPG + hh
                oc_ref[0, h * NSA_DH:(h + 1) * NSA_DH, k * Q_BLOCK:(k + 1) * Q_BLOCK] = \
                    oc[:, hh * Q_BLOCK:(hh + 1) * Q_BLOCK]
            imp_g = imp[:, 0:Q_BLOCK]
            for hh in range(1, NSA_HPG):
                imp_g = imp_g + imp[:, hh * Q_BLOCK:(hh + 1) * Q_BLOCK]
            imp_ref[k, g] = imp_g

    for nc in range(1, ckv_ref.shape[2] // CMP_CHUNK + 1):
        pl.when(n_chunks == nc)(functools.partial(cmp_branch, nc * CMP_CHUNK))

    def emit(k, g, sel):
        sel_ref[0, g, k] = sel
        cnt_ref[0, g, k] = _dot_nt(jnp.ones((8, Q_BLOCK), BF16), sel.astype(BF16))

    @pl.when(first_tile < RANK_FIRST_TILE)
    def _():
        for k, g in units:
            emit(k, g, jnp.where(jidx <= curs[k], 1.0, 0.0))

    @pl.when(first_tile >= RANK_FIRST_TILE)
    def _():
        jf = jidx.astype(F32)
        v0 = {(k, g): jnp.where((jidx >= 1) & (jidx <= curs[k] - 2), imp_ref[k, g], NEG_INF) for k, g in units}
        v = dict(v0)
        for _ in range(SEL_TOPN - 3):
            for u in units:
                mx = jnp.max(v[u], axis=0, keepdims=True)
                first = jnp.min(jnp.where(v[u] == mx, jf, float(nblk)), axis=0, keepdims=True)
                v[u] = jnp.where(jf == first, NEG_INF, v[u])
        for k, g in units:
            always = (jidx == 0) | (jidx == curs[k]) | (jidx == curs[k] - 1)
            emit(k, g, jnp.where((v[k, g] != v0[k, g]) | always, 1.0, 0.0))


def _nsa_a(qt, ckv, c_cmp, d_cmp, srow, ovt):
    batch, _, t = qt.shape
    nqb = t // Q_BLOCK
    ns = t // SEL_BLOCK
    ncmp = ckv.shape[2]
    outs = [jax.ShapeDtypeStruct((batch, NSA_WIDTH, t), F32),
            jax.ShapeDtypeStruct((batch, NSA_GROUPS, nqb, ns, Q_BLOCK), F32),
            jax.ShapeDtypeStruct((batch, NSA_GROUPS, nqb, 8, ns), F32)]
    assert nqb % SEL_TILES == 0 and RANK_FIRST_TILE % SEL_TILES == 0
    return pl.pallas_call(
        _nsa_a_kernel,
        out_shape=outs,
        grid=(batch, nqb // SEL_TILES),
        in_specs=[pl.BlockSpec((1, NSA_WIDTH, SEL_TILES * Q_BLOCK), lambda b, i: (b, 0, i)),
                  pl.BlockSpec((2, 1, ncmp, KV_WIDTH), lambda b, i: (0, b, 0, 0)),
                  _const_spec(c_cmp.shape), _const_spec(d_cmp.shape), _const_spec(srow.shape),
                  _const_spec(ovt.shape)],
        out_specs=[pl.BlockSpec((1, NSA_WIDTH, SEL_TILES * Q_BLOCK), lambda b, i: (b, 0, i)),
                   pl.BlockSpec((1, NSA_GROUPS, SEL_TILES, ns, Q_BLOCK), lambda b, i: (b, 0, i, 0, 0)),
                   pl.BlockSpec((1, NSA_GROUPS, SEL_TILES, 8, ns), lambda b, i: (b, 0, i, 0, 0))],
        scratch_shapes=[pltpu.VMEM((SEL_TILES, NSA_GROUPS, ns, Q_BLOCK), F32)],
        compiler_params=_params(("parallel", "parallel")),
        name="nsa_select",
    )(qt, ckv, c_cmp, d_cmp, srow, ovt)


def _block_softmax_step(carry, raw, consts, shifts, v_rows, g):
    m, l, acc = carry
    width = raw.shape[1]
    fold = lambda x: x.reshape(SEL_BLOCK // 8, 8, width)
    us = [raw[i * SEL_BLOCK:(i + 1) * SEL_BLOCK] + c for i, c in enumerate(consts)]
    top8 = None
    for u, sh in zip(us, shifts):
        t = jnp.max(fold(u), axis=0) + sh
        top8 = t if top8 is None else jnp.maximum(top8, t)
    m_new = jnp.maximum(m, jnp.max(top8, axis=0, keepdims=True))
    m_safe = jnp.where(m_new == NEG_INF, 0.0, m_new)
    alpha = jnp.where(m == NEG_INF, 0.0, jnp.exp2(m - m_safe))
    ps = [jnp.exp2((u - (m_safe - sh)).astype(BF16)) for u, sh in zip(us, shifts)]
    lane_grp = lax.broadcasted_iota(jnp.int32, v_rows.shape, 1) // NSA_DH
    v_aug = jnp.where(lane_grp == g, v_rows, jnp.ones_like(v_rows))
    pv = _dot_tn(v_aug, jnp.concatenate(ps, axis=0))
    og = 1 - g
    return m_new, alpha * l + pv[og * NSA_DH:og * NSA_DH + 1], alpha * acc + pv[g * NSA_DH:(g + 1) * NSA_DH]


def _nsa_b_kernel(idx_ref, qt_ref, ks_ref, vs_ref, kw_ref, vw_ref, sel_ref, oc_ref, gt_ref, cblk_ref, cwin_ref,
                  srow_ref, o_ref, *, idx_words):
    b = pl.program_id(0)
    qb = pl.program_id(1)
    base = qb * Q_BLOCK
    width = NSA_HPG * Q_BLOCK
    groups = range(NSA_GROUPS)
    qts = [_group_queries(qt_ref, g) for g in groups]
    srows = [srow_ref[g] for g in groups]
    entries = [((b * NSA_GROUPS + g) * pl.num_programs(1) + qb) * idx_words for g in groups]
    n_act = [idx_ref[entries[g]] for g in groups]
    init = (jnp.full((1, width), NEG_INF, F32), jnp.zeros((1, width), F32), jnp.zeros((NSA_DH, width), F32))
    zero_row = jnp.zeros((1, width), F32)

    def gate_row(g, j, live):
        picked = sel_ref[0, g, 0, pl.ds(j, 1), :]
        gate = jnp.where((picked > 0.0) & live, 0.0, NEG_INF)
        return jnp.concatenate([gate] * NSA_HPG, axis=1)

    def stage(it):
        out = []
        for g in groups:
            words = [idx_ref[entries[g] + 1 + it * SLC_WORDS + w] for w in range(SLC_WORDS)]
            kts, vts, shifts = [], [], []
            for i in range(SLC_CHUNK):
                live = it * SLC_CHUNK + i < n_act[g]
                j = jnp.where(live, lax.shift_right_logical(words[i // 4], 8 * (i % 4)) & 0xFF, 0)
                k0 = pl.multiple_of(j * SEL_BLOCK, SEL_BLOCK)
                kts.append(ks_ref[pl.ds(k0, SEL_BLOCK), :])
                vts.append(vs_ref[pl.ds(k0, SEL_BLOCK), :])
                shifts.append(gate_row(g, j, live) - srows[g] * (base - k0).astype(F32))
            out.append((_dot(jnp.concatenate(kts, axis=0), qts[g]), tuple(shifts), jnp.concatenate(vts, axis=0)))
        return tuple(out)

    def body(it, states):
        staged = stage(it)
        return tuple(_block_softmax_step(states[g], staged[g][0], [cblk_ref[g]] * SLC_CHUNK, staged[g][1],
                                         staged[g][2], g) for g in groups)

    n_it = (jnp.maximum(n_act[0], n_act[1]) + SLC_CHUNK - 1) // SLC_CHUNK
    slc = list(lax.fori_loop(0, n_it, body, (init,) * NSA_GROUPS))

    diag = range(WINDOW // SEL_BLOCK, WIN_KEYS // SEL_BLOCK)
    r0 = pl.multiple_of(base, Q_BLOCK)
    n_win = WIN_KEYS // SEL_BLOCK
    win = [init] * NSA_GROUPS
    steps = []
    for g in groups:
        shifts = [gate_row(g, 2 * qb + i, True) for i in range(len(diag))]
        steps.append((slc, g, ks_ref[pl.ds(r0, Q_BLOCK), :], vs_ref[pl.ds(r0, Q_BLOCK), :], diag, shifts))
    for c in range(WIN_KEYS // WIN_CHUNK):
        blocks = range(c * n_win // 2, (c + 1) * n_win // 2)
        kws, vws, offs = [], [], []
        for r in blocks:
            j = 2 * qb - WINDOW // SEL_BLOCK + r
            k0 = pl.multiple_of(jnp.maximum(j, 0) * SEL_BLOCK, SEL_BLOCK)
            kws.append(kw_ref[pl.ds(k0, SEL_BLOCK), :])
            vws.append(vw_ref[pl.ds(k0, SEL_BLOCK), :])
            offs.append(zero_row + jnp.where(j >= 0, 0.0, NEG_INF))
        kw = jnp.concatenate(kws, axis=0)
        vw = jnp.concatenate(vws, axis=0)
        for g in groups:
            steps.append((win, g, kw, vw, blocks, offs))
    raws = [_dot(keys, qts[g]) for _, g, keys, _, _, _ in steps]
    for raw, (state, g, _, values, blocks, shifts) in zip(raws, steps):
        consts = [cwin_ref[g, r * SEL_BLOCK:(r + 1) * SEL_BLOCK, :] for r in blocks]
        state[g] = _block_softmax_step(state[g], raw, consts, shifts, values, g)

    heads_out = []
    for g in groups:
        o_s = slc[g][2] * (1.0 / jnp.maximum(slc[g][1], 1e-30))
        o_w = win[g][2] * (1.0 / jnp.maximum(win[g][1], 1e-30))
        for hh in range(NSA_HPG):
            h = g * NSA_HPG + hh
            cs = slice(hh * Q_BLOCK, (hh + 1) * Q_BLOCK)
            gates = gt_ref[0, 3 * h:3 * h + 3, :]
            heads_out.append(gates[0:1] * oc_ref[0, h * NSA_DH:(h + 1) * NSA_DH, :]
                             + gates[1:2] * o_s[:, cs] + gates[2:3] * o_w[:, cs])
    o_ref[...] = jnp.concatenate(heads_out, axis=0).T.astype(BF16)


def _nsa_b(idx, qt, kvs, sel, oc, gt, c_blk, c_win, srow):
    batch, _, t = qt.shape
    nqb = t // Q_BLOCK
    ns = t // SEL_BLOCK
    kv_spec = lambda c: pl.BlockSpec((t, KV_WIDTH), lambda b, i, idx_ref: (b, c))
    grid_spec = pltpu.PrefetchScalarGridSpec(
        num_scalar_prefetch=1,
        grid=(batch, nqb),
        in_specs=[pl.BlockSpec((1, NSA_WIDTH, Q_BLOCK), lambda b, i, r: (b, 0, i)),
                  kv_spec(0), kv_spec(1), kv_spec(2), kv_spec(3),
                  pl.BlockSpec((1, NSA_GROUPS, 1, ns, Q_BLOCK), lambda b, i, r: (b, 0, i, 0, 0)),
                  pl.BlockSpec((1, NSA_WIDTH, Q_BLOCK), lambda b, i, r: (b, 0, i)),
                  pl.BlockSpec((1, gt.shape[1], Q_BLOCK), lambda b, i, r: (b, 0, i)),
                  pl.BlockSpec(c_blk.shape, lambda b, i, r: (0, 0, 0)),
                  pl.BlockSpec(c_win.shape, lambda b, i, r: (0, 0, 0)),
                  pl.BlockSpec(srow.shape, lambda b, i, r: (0, 0, 0))],
        out_specs=pl.BlockSpec((Q_BLOCK, NSA_WIDTH), lambda b, i, r: (b * nqb + i, 0)),
    )
    return pl.pallas_call(
        functools.partial(_nsa_b_kernel, idx_words=1 + _slc_steps(ns) * SLC_WORDS),
        out_shape=jax.ShapeDtypeStruct((batch * t, NSA_WIDTH), BF16),
        grid_spec=grid_spec,
        compiler_params=_params(("parallel", "parallel")),
        name="nsa_attend",
    )(idx, qt, kvs, kvs, kvs, kvs, sel, oc, gt, c_blk, c_win, srow)


def _pad_cols(w, width):
    return jnp.pad(w, ((0, 0), (0, width - w.shape[1])))


def _compress_weights(w1, pe, w2):
    nsub = CMP_STRIDE
    w1r = w1.reshape(2, nsub, NSA_DH, CMP_HIDDEN)
    same_group = jnp.eye(NSA_GROUPS, dtype=F32)
    big = same_group[None, :, None, :, None, None] * w1r.transpose(1, 2, 0, 3)[:, None, :, None, :, :]
    big = big.reshape(nsub * KV_WIDTH, NSA_GROUPS * 2 * CMP_HIDDEN)
    per = pe.reshape(2, nsub, 1, NSA_DH)
    pe2 = jnp.broadcast_to(per, (2, nsub, NSA_GROUPS, NSA_DH)).reshape(2, 1, nsub * KV_WIDTH)
    pe2 = jnp.broadcast_to(pe2, (2, 8, nsub * KV_WIDTH)).reshape(16, nsub * KV_WIDTH)
    w2bd = same_group[:, None, :, None] * w2[None, :, None, :]
    return big.astype(BF16), pe2.astype(BF16), w2bd.reshape(NSA_GROUPS * CMP_HIDDEN, KV_WIDTH).astype(BF16)


def _selection_overlap_t(ncmp_pad, ns):
    cs = np.arange(ncmp_pad) * CMP_STRIDE
    ss = np.arange(ns) * SEL_BLOCK
    ov = np.minimum(cs[:, None] + CMP_BLOCK, ss[None, :] + SEL_BLOCK) - np.maximum(cs[:, None], ss[None, :])
    return (np.clip(ov, 0, None).astype(np.float32) / CMP_BLOCK).T


def _slc_steps(ns):
    return -(-ns // SLC_CHUNK)


def _active_blocks(cnt):
    ns = cnt.shape[-1]
    before_tile = jnp.arange(ns)[None, :] < (Q_BLOCK // SEL_BLOCK) * jnp.arange(cnt.shape[2])[:, None]
    flags = (cnt[:, :, :, 0, :] > 0.0) & before_tile
    order = jnp.argsort(jnp.where(flags, 0, 1).astype(jnp.int32), axis=-1, stable=True).astype(jnp.int32)
    lead = order.shape[:-1]
    steps = _slc_steps(ns)
    order = jnp.pad(order, [(0, 0)] * len(lead) + [(0, steps * SLC_CHUNK - ns)]).reshape(lead + (steps, SLC_CHUNK))
    order = jnp.pad(order, [(0, 0)] * (len(lead) + 1) + [(0, 4 * SLC_WORDS - SLC_CHUNK)])
    packed = jnp.sum(order.reshape(lead + (steps * SLC_WORDS, 4)) << (8 * jnp.arange(4, dtype=jnp.int32)), axis=-1)
    n_act = jnp.sum(flags, axis=-1, dtype=jnp.int32)
    return jnp.concatenate([n_act[..., None], packed], axis=-1).reshape(-1)


def kernel(x, mem, g_mix, g_ffn, g_mem, w_mem_kv, w_up, conv_w, conv_b, w_down,
           a_w_in, a_w_alpha, a_b_alpha, a_g_head, a_w_out,
           g_kv, w_kv, pe_k, pe_v, w_ck1, w_ck2, w_cv1, w_cv2,
           b_w_in, b_w_out, g_final):
    batch, t, d = x.shape
    n = batch * t
    m = mem.shape[1]
    tm = min(1024, t)
    xf = x.reshape(n, d)
    memf = mem.reshape(batch * m, d)
    row = lambda v: v.reshape(1, -1)

    mkv0 = _rms_proj(memf, row(g_mem[0]), w_mem_kv[0].astype(BF16), m, BF16).reshape(batch, m, 2 * MEM_WIDTH)
    wa = a_w_in[0]
    c_alr = 2 * GLA_QK + 2 * GLA_V
    w_a = jnp.concatenate([wa[:, :c_alr], wa[:, c_alr + GLA_RANK:], _pad_cols(wa[:, c_alr:c_alr + GLA_RANK], LANE)],
                          axis=1).astype(BF16)
    w_alpha = jnp.pad(a_w_alpha[0], ((0, LANE - GLA_RANK), (0, 0))).astype(BF16)
    q, k, gl, v, r, mq = _inproj_a(xf, row(g_mix[0]), w_a, w_alpha, row(a_b_alpha[0]), tm)
    o = _gla(q, k, gl, v, r, row(a_g_head[0]), batch, tm)
    xf = _attn_out(o, mq, mkv0, a_w_out[0].astype(BF16), xf, batch, tm)
    xf = _ffn(xf, row(g_ffn[0]), w_up[0].astype(BF16), conv_w[0], row(conv_b[0]), w_down[0].astype(BF16),
              row(g_final), batch, tm, final_norm=False)

    wb = b_w_in[0]
    wqt = wb[:, :NSA_WIDTH].T.astype(BF16)
    n_gate = 3 * NSA_HEADS
    wgt = jnp.pad(wb[:, NSA_WIDTH:NSA_WIDTH + n_gate].T, ((0, 32 - n_gate), (0, 0))).astype(BF16)
    wmq = wb[:, NSA_WIDTH + n_gate:].astype(BF16)
    ckv_in, kvs, qt, gt, mq1 = _proj_b(xf, row(g_kv), row(g_mix[1]), w_kv.astype(BF16), wqt, wgt, wmq, batch, tm)
    nsub = t // CMP_STRIDE
    w1k, pek, w2k = _compress_weights(w_ck1, pe_k, w_ck2)
    w1v, pev, w2v = _compress_weights(w_cv1, pe_v, w_cv2)
    ckv = _compress(ckv_in.reshape(2, batch, nsub, CMP_STRIDE * KV_WIDTH),
                    jnp.stack([w1k, w1v]), jnp.stack([pek, pev]), jnp.stack([w2k, w2v]))

    ns = t // SEL_BLOCK
    c_cmp, d_cmp, c_blk, c_win, srow = _alibi_tiles(nsub)
    ovt = jnp.asarray(_selection_overlap_t(nsub, ns), BF16)
    oc, sel, cnt = _nsa_a(qt, ckv, jnp.asarray(c_cmp), jnp.asarray(d_cmp), jnp.asarray(srow), ovt)
    idx = _active_blocks(cnt)
    o1 = _nsa_b(idx, qt, kvs, sel, oc, gt, jnp.asarray(c_blk), jnp.asarray(c_win), jnp.asarray(srow))
    mkv1 = _rms_proj(memf, row(g_mem[1]), w_mem_kv[1].astype(BF16), m, BF16).reshape(batch, m, 2 * MEM_WIDTH)
    xf = _attn_out(o1, mq1, mkv1, b_w_out[0].astype(BF16), xf, batch, tm)
    xf = _ffn(xf, row(g_ffn[1]), w_up[1].astype(BF16), conv_w[1], row(conv_b[1]), w_down[1].astype(BF16),
              row(g_final), batch, tm, final_norm=True)
    return xf.reshape(batch, t, d)
```

```python
import functools

import numpy as np
import jax
import jax.numpy as jnp
from jax import lax
from jax.experimental import pallas as pl
from jax.experimental.pallas import tpu as pltpu

F32 = jnp.float32
BF16 = jnp.bfloat16
EPS = 1e-6
NEG_INF = float("-inf")
LOG2E = 1.4426950408889634

V7X_VMEM_BYTES = 64 * 1024 * 1024
VMEM_LIMIT = V7X_VMEM_BYTES - 8 * 1024 * 1024

D_MODEL = 1024
MEM_HEADS = 4
MEM_DH = 128
MEM_WIDTH = MEM_HEADS * MEM_DH
GLA_HEADS = 4
GLA_DK = 64
GLA_DV = 128
GLA_QK = GLA_HEADS * GLA_DK
GLA_V = GLA_HEADS * GLA_DV
GLA_RANK = 16
GLA_TAU = 16.0
GLA_CHUNK = 64
GLA_LEVELS = 6
NSA_HEADS = 8
NSA_GROUPS = 2
NSA_HPG = NSA_HEADS // NSA_GROUPS
NSA_DH = 64
NSA_WIDTH = NSA_HEADS * NSA_DH
KV_WIDTH = NSA_GROUPS * NSA_DH
CMP_BLOCK = 32
CMP_STRIDE = 16
CMP_HIDDEN = 256
SEL_BLOCK = 64
SEL_TOPN = 16
WINDOW = 512
Q_BLOCK = 128
WIN_KEYS = WINDOW + Q_BLOCK
WIN_CHUNK = WIN_KEYS // 2
CMP_CHUNK = 128
SLC_CHUNK = 6
SLC_WORDS = -(-SLC_CHUNK // 4)
RANK_FIRST_TILE = SEL_TOPN * SEL_BLOCK // Q_BLOCK
SEL_TILES = 2
ATT_TILES = 2
FFN_DIM = 2816
FFN_CHUNK = 256
LANE = 128


def _dot(a, b):
    return jnp.dot(a, b, preferred_element_type=F32)


def _dot_nt(a, b):
    return lax.dot_general(a, b, (((1,), (1,)), ((), ())), preferred_element_type=F32)


def _dot_tn(a, b):
    return lax.dot_general(a, b, (((0,), (0,)), ((), ())), preferred_element_type=F32)


def _params(sem):
    return pltpu.CompilerParams(dimension_semantics=sem, vmem_limit_bytes=VMEM_LIMIT)


def _const_spec(shape):
    n = len(shape)
    return pl.BlockSpec(shape, lambda *_: (0,) * n)


def _normalize(x):
    return x * lax.rsqrt(jnp.mean(x * x, axis=-1, keepdims=True) + EPS)


def _sigmoid(x):
    return 1.0 / (1.0 + jnp.exp(-x))


def _rms_proj_kernel(x_ref, g_ref, w_ref, o_ref):
    h = (_normalize(x_ref[...]) * g_ref[...]).astype(BF16)
    o_ref[...] = _dot(h, w_ref[...]).astype(o_ref.dtype)


def _rms_proj(x, g, w, tm, out_dtype):
    n, d = x.shape
    p = w.shape[1]
    return pl.pallas_call(
        _rms_proj_kernel,
        out_shape=jax.ShapeDtypeStruct((n, p), out_dtype),
        grid=(n // tm,),
        in_specs=[pl.BlockSpec((tm, d), lambda i: (i, 0)), _const_spec((1, d)), _const_spec((d, p))],
        out_specs=pl.BlockSpec((tm, p), lambda i: (i, 0)),
        compiler_params=_params(("parallel",)),
        name="rms_proj",
    )(x, g, w)


def _inproj_a_kernel(x_ref, g_ref, w_ref, wa_ref, ba_ref, q_ref, k_ref, gl_ref, v_ref, r_ref, mq_ref):
    h = (_normalize(x_ref[...]) * g_ref[...]).astype(BF16)
    c = 0
    q_ref[...] = _dot(h, w_ref[:, c:c + GLA_QK]) * (GLA_DK ** -0.5)
    c += GLA_QK
    k_ref[...] = _dot(h, w_ref[:, c:c + GLA_QK])
    c += GLA_QK
    v_ref[...] = _dot(h, w_ref[:, c:c + GLA_V]).astype(BF16)
    c += GLA_V
    r_ref[...] = _dot(h, w_ref[:, c:c + GLA_V])
    c += GLA_V
    mq_ref[...] = _dot(h, w_ref[:, c:c + MEM_WIDTH]).astype(BF16)
    c += MEM_WIDTH
    alr = _dot(h, w_ref[:, c:c + LANE]).astype(BF16)
    z = _dot(alr, wa_ref[...]) + ba_ref[...]
    log_sig = jnp.minimum(z, 0.0) - jnp.log1p(jnp.exp(-jnp.abs(z)))
    gl_ref[...] = log_sig * (1.0 / GLA_TAU)


def _inproj_a(x, g, w, wa, ba, tm):
    n, d = x.shape
    row = lambda i: (i, 0)
    outs = [
        jax.ShapeDtypeStruct((n, GLA_QK), F32), jax.ShapeDtypeStruct((n, GLA_QK), F32),
        jax.ShapeDtypeStruct((n, GLA_QK), F32), jax.ShapeDtypeStruct((n, GLA_V), BF16),
        jax.ShapeDtypeStruct((n, GLA_V), F32), jax.ShapeDtypeStruct((n, MEM_WIDTH), BF16),
    ]
    return pl.pallas_call(
        _inproj_a_kernel,
        out_shape=outs,
        grid=(n // tm,),
        in_specs=[pl.BlockSpec((tm, d), row), _const_spec((1, d)), _const_spec(w.shape),
                  _const_spec(wa.shape), _const_spec(ba.shape)],
        out_specs=[pl.BlockSpec((tm, s.shape[1]), row) for s in outs],
        compiler_params=_params(("parallel",)),
        name="inproj_a",
    )(x, g, w, wa, ba)


def _gla_constants():
    c = GLA_CHUNK
    w = np.zeros((8 * c, c), np.float32)
    masks = np.zeros((GLA_LEVELS + 1, c, c), np.float32)
    masks[0] = np.eye(c)
    for l in range(1, GLA_LEVELS + 1):
        blk, half = 2 ** l, 2 ** (l - 1)
        for i in range(c):
            mid = (i // blk) * blk + half - 1
            if i % blk >= half:
                w[(l - 1) * c + i, mid + 1:i + 1] = 1.0
            else:
                w[(l - 1) * c + i, i + 1:mid + 1] = 1.0
        for t in range(c):
            for s in range(c):
                if t // blk == s // blk and t % blk >= half and s % blk < half:
                    masks[l, t, s] = 1.0
    for i in range(c):
        w[6 * c + i, :i + 1] = 1.0
        w[7 * c + i, i + 1:] = 1.0
    return np.concatenate([w, w], axis=1), np.tile(masks, (1, 1, GLA_HEADS))


def _gla_kernel(q_ref, k_ref, gl_ref, v_ref, r_ref, gh_ref, wcat_ref, msk_ref, o_ref, st_ref, *, n_chunks):
    c = GLA_CHUNK

    @pl.when(pl.program_id(1) == 0)
    def _():
        st_ref[...] = jnp.zeros_like(st_ref)

    lane_qk = lax.broadcasted_iota(jnp.int32, (1, GLA_QK), 1) // GLA_DK
    lane_v = lax.broadcasted_iota(jnp.int32, (1, GLA_V), 1) // GLA_DV
    row_qk = lax.broadcasted_iota(jnp.int32, (GLA_QK, 1), 0) // GLA_DK
    ones = jnp.ones((2 * c, LANE), BF16)
    wcat = wcat_ref[...]

    def stack_heads(x, lane_head):
        return jnp.concatenate([jnp.where(lane_head == h, x, jnp.zeros_like(x)) for h in range(GLA_HEADS)], axis=0)

    chunks = [slice(ci * c, (ci + 1) * c) for ci in range(n_chunks)]
    es, decs = [], []
    for rows in chunks:
        g = gl_ref[rows, :] * LOG2E
        g_hi = g.astype(BF16)
        g_split = jnp.concatenate([g_hi, (g - g_hi.astype(F32)).astype(BF16)], axis=0)
        es.append(jnp.exp2(_dot(wcat, g_split)))
        decs.append(jnp.exp2(_dot_tn(g_split, ones)))

    o_intras, q_ins, upds = [], [], []
    for rows, e in zip(chunks, es):
        q = q_ref[rows, :]
        k = k_ref[rows, :]
        v = v_ref[rows, :]
        attn = jnp.zeros((c, GLA_HEADS * c), F32)
        for l in range(GLA_LEVELS + 1):
            if l == 0:
                ql, kl = q, k
            else:
                el = e[(l - 1) * c:l * c]
                ql, kl = q * el, k * el
            attn = attn + _dot_nt(ql.astype(BF16), stack_heads(kl, lane_qk).astype(BF16)) * msk_ref[l]
        o_intras.append(_dot(attn.astype(BF16), stack_heads(v, lane_v)))
        q_ins.append((q * e[6 * c:7 * c]).astype(BF16))
        kk = stack_heads(k * e[7 * c:8 * c], lane_qk).astype(BF16)
        v_rows = jnp.concatenate([v[:, h * GLA_DV:(h + 1) * GLA_DV] for h in range(GLA_HEADS)], axis=0)
        upds.append(_dot_tn(kk, v_rows))

    st = st_ref[...]
    outs = []
    for o_intra, q_in, dec, upd in zip(o_intras, q_ins, decs, upds):
        st_bd = jnp.concatenate([jnp.where(row_qk == h, st, 0.0).astype(BF16) for h in range(GLA_HEADS)], axis=1)
        outs.append(o_intra + _dot(q_in, st_bd))
        st = dec * st + upd
    st_ref[...] = st

    for rows, o in zip(chunks, outs):
        for h in range(GLA_HEADS):
            cs = slice(h * GLA_DV, (h + 1) * GLA_DV)
            on = _normalize(o[:, cs]) * gh_ref[...]
            r = r_ref[rows, cs]
            o_ref[rows, cs] = (on * (r * _sigmoid(r))).astype(BF16)


def _gla(q, k, gl, v, r, g_head, batch, ct):
    n = q.shape[0]
    t = n // batch
    nt = t // ct
    wcat, masks = _gla_constants()
    row = lambda b, i: (b * nt + i, 0)
    return pl.pallas_call(
        functools.partial(_gla_kernel, n_chunks=ct // GLA_CHUNK),
        out_shape=jax.ShapeDtypeStruct((n, GLA_V), BF16),
        grid=(batch, nt),
        in_specs=[pl.BlockSpec((ct, GLA_QK), row), pl.BlockSpec((ct, GLA_QK), row),
                  pl.BlockSpec((ct, GLA_QK), row), pl.BlockSpec((ct, GLA_V), row),
                  pl.BlockSpec((ct, GLA_V), row), _const_spec((1, GLA_DV)),
                  _const_spec(wcat.shape), _const_spec(masks.shape)],
        out_specs=pl.BlockSpec((ct, GLA_V), row),
        scratch_shapes=[pltpu.VMEM((GLA_QK, GLA_DV), F32)],
        compiler_params=_params(("parallel", "arbitrary")),
        name="gla",
    )(q, k, gl, v, r, g_head, jnp.asarray(wcat, BF16), jnp.asarray(masks, F32))


def _attn_out_kernel(o_ref, mq_ref, mk_ref, mv_ref, w_ref, x_ref, out_ref):
    heads = [slice(h * MEM_DH, (h + 1) * MEM_DH) for h in range(MEM_HEADS)]
    scores = [_dot_nt(mq_ref[:, cs], mk_ref[0, :, cs]) * (MEM_DH ** -0.5 * LOG2E) for cs in heads]
    parts = [o_ref[...]]
    for s, cs in zip(scores, heads):
        e = jnp.exp2(s - jnp.max(s, axis=-1, keepdims=True))
        p = e / jnp.sum(e, axis=-1, keepdims=True)
        parts.append(_dot(p.astype(BF16), mv_ref[0, :, cs]).astype(BF16))
    cat = jnp.concatenate(parts, axis=-1)
    out_ref[...] = x_ref[...] + _dot(cat, w_ref[...])


def _attn_out(o, mq, mkv, w_out, x, batch, tm):
    n, d = x.shape
    t = n // batch
    nt = t // tm
    m = mkv.shape[1]
    row = lambda b, i: (b * nt + i, 0)
    return pl.pallas_call(
        _attn_out_kernel,
        out_shape=jax.ShapeDtypeStruct((n, d), F32),
        grid=(batch, nt),
        in_specs=[pl.BlockSpec((tm, o.shape[1]), row), pl.BlockSpec((tm, MEM_WIDTH), row),
                  pl.BlockSpec((1, m, MEM_WIDTH), lambda b, i: (b, 0, 0)),
                  pl.BlockSpec((1, m, MEM_WIDTH), lambda b, i: (b, 0, 1)),
                  _const_spec(w_out.shape), pl.BlockSpec((tm, d), row)],
        out_specs=pl.BlockSpec((tm, d), row),
        compiler_params=_params(("parallel", "parallel")),
        name="attn_out",
    )(o, mq, mkv, mkv, w_out, x)


def _ffn_kernel(x_ref, g_ref, wup_ref, cw_ref, cb_ref, wd_ref, gf_ref, out_ref, act_ref, tail_ref, *,
                final_norm):
    @pl.when(pl.program_id(1) == 0)
    def _():
        tail_ref[...] = jnp.zeros_like(tail_ref)

    x = x_ref[...]
    tm = x.shape[0]
    h = (_normalize(x) * g_ref[...]).astype(BF16)
    rid = lax.broadcasted_iota(jnp.int32, (8, FFN_CHUNK), 0)
    for j in range(FFN_DIM // FFN_CHUNK):
        cs = slice(j * FFN_CHUNK, (j + 1) * FFN_CHUNK)
        a = _dot(h, wup_ref[:, cs])
        b = _dot(h, wup_ref[:, FFN_DIM + j * FFN_CHUNK:FFN_DIM + (j + 1) * FFN_CHUNK])
        tail = tail_ref[j]
        r1 = pltpu.roll(a, 1, 0)
        r2 = pltpu.roll(a, 2, 0)
        top1 = jnp.where(rid == 0, tail[7:8], r1[:8])
        top2 = jnp.where(rid == 0, tail[6:7], jnp.where(rid == 1, tail[7:8], r2[:8]))
        a1 = jnp.concatenate([top1, r1[8:]], axis=0)
        a2 = jnp.concatenate([top2, r2[8:]], axis=0)
        tail_ref[j] = a[tm - 8:]
        ac = a2 * cw_ref[0:1, cs] + a1 * cw_ref[1:2, cs] + a * cw_ref[2:3, cs] + cb_ref[:, cs]
        act_ref[:, cs] = (ac * _sigmoid(ac) * b).astype(BF16)
    y = x + _dot(act_ref[...], wd_ref[...])
    if final_norm:
        y = _normalize(y) * gf_ref[...]
    out_ref[...] = y


def _ffn(x, g, w_up, conv_w, conv_b, w_down, g_final, batch, tm, final_norm):
    n, d = x.shape
    nt = n // batch // tm
    row = lambda b, i: (b * nt + i, 0)
    once = pl.Buffered(1)
    return pl.pallas_call(
        functools.partial(_ffn_kernel, final_norm=final_norm),
        out_shape=jax.ShapeDtypeStruct((n, d), F32),
        grid=(batch, nt),
        in_specs=[pl.BlockSpec((tm, d), row), _const_spec((1, d)),
                  pl.BlockSpec(w_up.shape, lambda b, i: (0, 0), pipeline_mode=once),
                  _const_spec(conv_w.shape), _const_spec(conv_b.shape),
                  pl.BlockSpec(w_down.shape, lambda b, i: (0, 0), pipeline_mode=once),
                  _const_spec((1, d))],
        out_specs=pl.BlockSpec((tm, d), row),
        scratch_shapes=[pltpu.VMEM((tm, FFN_DIM), BF16),
                        pltpu.VMEM((FFN_DIM // FFN_CHUNK, 8, FFN_CHUNK), F32)],
        compiler_params=_params(("parallel", "arbitrary")),
        name="conv_ffn",
    )(x, g, w_up, conv_w, conv_b, w_down, g_final)


def _proj_b_kernel(x_ref, gkv_ref, gmix_ref, wkv_ref, wqt_ref, wgt_ref, wmq_ref,
                   cmp_ref, kvs_ref, qt_ref, gt_ref, mq_ref):
    xn = _normalize(x_ref[...])
    hkv = (xn * gkv_ref[...]).astype(BF16)
    h1 = (xn * gmix_ref[...]).astype(BF16)
    kv_cmp = _dot(hkv, wkv_ref[:, 0:2 * KV_WIDTH]).astype(BF16)
    cmp_ref[0] = kv_cmp[:, 0:KV_WIDTH]
    cmp_ref[1] = kv_cmp[:, KV_WIDTH:2 * KV_WIDTH]
    kvs_ref[...] = _dot(hkv, wkv_ref[:, 2 * KV_WIDTH:]).astype(BF16)
    qt_ref[0] = (_dot_nt(wqt_ref[...], h1) * (NSA_DH ** -0.5 * LOG2E)).astype(BF16)
    gt_ref[0] = _sigmoid(_dot_nt(wgt_ref[...], h1))
    mq_ref[...] = _dot(h1, wmq_ref[...]).astype(BF16)


def _proj_b(x, g_kv, g_mix, w_kv, wqt, wgt, wmq, batch, tm):
    n, d = x.shape
    t = n // batch
    nt = t // tm
    row = lambda b, i: (b * nt + i, 0)
    outs = [
        jax.ShapeDtypeStruct((2, n, KV_WIDTH), BF16),
        jax.ShapeDtypeStruct((n, 4 * KV_WIDTH), BF16),
        jax.ShapeDtypeStruct((batch, NSA_WIDTH, t), BF16),
        jax.ShapeDtypeStruct((batch, wgt.shape[0], t), F32),
        jax.ShapeDtypeStruct((n, MEM_WIDTH), BF16),
    ]
    return pl.pallas_call(
        _proj_b_kernel,
        out_shape=outs,
        grid=(batch, nt),
        in_specs=[pl.BlockSpec((tm, d), row), _const_spec((1, d)), _const_spec((1, d)),
                  _const_spec(w_kv.shape), _const_spec(wqt.shape), _const_spec(wgt.shape),
                  _const_spec(wmq.shape)],
        out_specs=[pl.BlockSpec((2, tm, KV_WIDTH), lambda b, i: (0, b * nt + i, 0)),
                   pl.BlockSpec((tm, 4 * KV_WIDTH), row),
                   pl.BlockSpec((1, NSA_WIDTH, tm), lambda b, i: (b, 0, i)),
                   pl.BlockSpec((1, wgt.shape[0], tm), lambda b, i: (b, 0, i)),
                   pl.BlockSpec((tm, MEM_WIDTH), row)],
        compiler_params=_params(("parallel", "parallel")),
        name="proj_b",
    )(x, g_kv, g_mix, w_kv, wqt, wgt, wmq)


def _compress_kernel(a_ref, w1_ref, pe_ref, w2_ref, o_ref):
    a = a_ref[0, 0]
    w1 = w1_ref[0]
    u = _dot(a, w1)
    cpe = _dot(pe_ref[0], w1)
    nrow = a.shape[0]
    hid = []
    for g in range(NSA_GROUPS):
        c0 = slice((2 * g) * CMP_HIDDEN, (2 * g + 1) * CMP_HIDDEN)
        c1 = slice((2 * g + 1) * CMP_HIDDEN, (2 * g + 2) * CMP_HIDDEN)
        nxt = pltpu.roll(u[:, c1], nrow - 1, 0)
        hid.append(u[:, c0] + nxt + cpe[0:1, c0] + cpe[8:9, c1])
    hcat = jax.nn.gelu(jnp.concatenate(hid, axis=-1), approximate=True).astype(BF16)
    o_ref[0, 0] = _dot(hcat, w2_ref[0]).astype(BF16)


def _compress(a, w1, pe, w2):
    _, batch, nsub, width = a.shape
    return pl.pallas_call(
        _compress_kernel,
        out_shape=jax.ShapeDtypeStruct((2, batch, nsub, KV_WIDTH), BF16),
        grid=(2, batch),
        in_specs=[pl.BlockSpec((1, 1, nsub, width), lambda s, b: (s, b, 0, 0)),
                  pl.BlockSpec((1,) + w1.shape[1:], lambda s, b: (s, 0, 0)),
                  pl.BlockSpec((1,) + pe.shape[1:], lambda s, b: (s, 0, 0)),
                  pl.BlockSpec((1,) + w2.shape[1:], lambda s, b: (s, 0, 0))],
        out_specs=pl.BlockSpec((1, 1, nsub, KV_WIDTH), lambda s, b: (s, b, 0, 0)),
        compiler_params=_params(("parallel", "parallel")),
        name="compress",
    )(a, w1, pe, w2)


def _alibi_tiles(ncmp_pad):
    slopes = LOG2E * 2.0 ** (-8.0 * np.arange(1, NSA_HEADS + 1) / NSA_HEADS)
    tl = np.tile(np.arange(Q_BLOCK), NSA_HPG)[None, :]
    cmp_end = (np.arange(ncmp_pad) * CMP_STRIDE + CMP_BLOCK - 1)[:, None]
    key = np.arange(SEL_BLOCK)[:, None]
    dist_win = tl + WINDOW - np.arange(WIN_KEYS)[:, None]
    c_cmp = np.zeros((NSA_GROUPS, ncmp_pad, NSA_HPG * Q_BLOCK), np.float32)
    c_blk = np.zeros((NSA_GROUPS, SEL_BLOCK, NSA_HPG * Q_BLOCK), np.float32)
    c_win = np.zeros((NSA_GROUPS, WIN_KEYS, NSA_HPG * Q_BLOCK), np.float32)
    srow = np.zeros((NSA_GROUPS, 1, NSA_HPG * Q_BLOCK), np.float32)
    for g in range(NSA_GROUPS):
        s = np.repeat(slopes[g * NSA_HPG:(g + 1) * NSA_HPG], Q_BLOCK)[None, :]
        c_cmp[g] = -s * (tl - cmp_end)
        c_blk[g] = -s * (tl - key)
        c_win[g] = np.where((dist_win >= 0) & (dist_win < WINDOW), -s * dist_win, -np.inf)
        srow[g] = s
    d_cmp = (cmp_end - tl).astype(np.float32)
    return c_cmp, d_cmp, c_blk, c_win, srow


def _group_queries(qt_ref, g, tile=0):
    zeros = jnp.zeros((NSA_DH, Q_BLOCK), BF16)
    cols = []
    for hh in range(NSA_HPG):
        h = g * NSA_HPG + hh
        piece = qt_ref[0, h * NSA_DH:(h + 1) * NSA_DH, tile * Q_BLOCK:(tile + 1) * Q_BLOCK]
        cols.append(jnp.concatenate([piece, zeros] if g == 0 else [zeros, piece], axis=0))
    return jnp.concatenate(cols, axis=1)


def _nsa_a_kernel(qt_ref, ckv_ref, ccmp_ref, dcmp_ref, srow_ref, ovt_ref, oc_ref, sel_ref, cnt_ref, imp_ref):
    first_tile = pl.program_id(1) * SEL_TILES
    nblk = sel_ref.shape[3]
    jidx = lax.broadcasted_iota(jnp.int32, (nblk, Q_BLOCK), 0)
    tl = lax.broadcasted_iota(jnp.int32, (nblk, Q_BLOCK), 1)
    units = [(k, g) for k in range(SEL_TILES) for g in range(NSA_GROUPS)]
    qbs = [first_tile + k for k in range(SEL_TILES)]
    bases = [(qb * Q_BLOCK).astype(F32) for qb in qbs]
    curs = [2 * qb + jnp.where(tl >= SEL_BLOCK, 1, 0) for qb in qbs]
    n_chunks = (qbs[-1] * (Q_BLOCK // CMP_STRIDE) + (Q_BLOCK - CMP_BLOCK) // CMP_STRIDE) // CMP_CHUNK + 1
    qts = {(k, g): _group_queries(qt_ref, g, k) for k, g in units}
    shifts = {(k, g): srow_ref[g] * bases[k] for k, g in units}

    def cmp_branch(n_rows):
        kc = ckv_ref[0, 0, 0:n_rows, :]
        vc = ckv_ref[1, 0, 0:n_rows, :]
        ov = ovt_ref[:, 0:n_rows]
        lane_grp = lax.broadcasted_iota(jnp.int32, vc.shape, 1) // NSA_DH
        raws = {u: _dot(kc, qts[u]) for u in units}
        for k, g in units:
            visible = dcmp_ref[0:n_rows, :] <= bases[k]
            s = raws[k, g] + jnp.where(visible, ccmp_ref[g, 0:n_rows, :] - shifts[k, g], NEG_INF)
            m = jnp.max(s, axis=0, keepdims=True)
            e = jnp.exp2((s - jnp.where(m == NEG_INF, 0.0, m)).astype(BF16))
            pv = _dot_tn(jnp.where(lane_grp == g, vc, jnp.ones_like(vc)), e)
            inv = 1.0 / jnp.maximum(pv[(1 - g) * NSA_DH:(1 - g) * NSA_DH + 1], 1e-30)
            oc = pv[g * NSA_DH:(g + 1) * NSA_DH] * inv
            imp = _dot(ov, e) * inv
            for hh in range(NSA_HPG):
                h = g * NSA_HPG + hh
                oc_ref[0, h * NSA_DH:(h + 1) * NSA_DH, k * Q_BLOCK:(k + 1) * Q_BLOCK] = \
                    oc[:, hh * Q_BLOCK:(hh + 1) * Q_BLOCK]
            imp_g = imp[:, 0:Q_BLOCK]
            for hh in range(1, NSA_HPG):
                imp_g = imp_g + imp[:, hh * Q_BLOCK:(hh + 1) * Q_BLOCK]
            imp_ref[k, g] = imp_g

    for nc in range(1, ckv_ref.shape[2] // CMP_CHUNK + 1):
        pl.when(n_chunks == nc)(functools.partial(cmp_branch, nc * CMP_CHUNK))

    def emit(k, g, sel):
        sel_ref[0, g, k] = sel
        cnt_ref[0, g, k] = _dot_nt(jnp.ones((8, Q_BLOCK), BF16), sel.astype(BF16))

    @pl.when(first_tile < RANK_FIRST_TILE)
    def _():
        for k, g in units:
            emit(k, g, jnp.where(jidx <= curs[k], 1.0, 0.0))

    @pl.when(first_tile >= RANK_FIRST_TILE)
    def _():
        jf = jidx.astype(F32)
        v0 = {(k, g): jnp.where((jidx >= 1) & (jidx <= curs[k] - 2), imp_ref[k, g], NEG_INF) for k, g in units}
        v = dict(v0)
        for _ in range(SEL_TOPN - 3):
            for u in units:
                mx = jnp.max(v[u], axis=0, keepdims=True)
                first = jnp.min(jnp.where(v[u] == mx, jf, float(nblk)), axis=0, keepdims=True)
                v[u] = jnp.where(jf == first, NEG_INF, v[u])
        for k, g in units:
            always = (jidx == 0) | (jidx == curs[k]) | (jidx == curs[k] - 1)
            emit(k, g, jnp.where((v[k, g] != v0[k, g]) | always, 1.0, 0.0))


def _nsa_a(qt, ckv, c_cmp, d_cmp, srow, ovt):
    batch, _, t = qt.shape
    nqb = t // Q_BLOCK
    ns = t // SEL_BLOCK
    ncmp = ckv.shape[2]
    outs = [jax.ShapeDtypeStruct((batch, NSA_WIDTH, t), F32),
            jax.ShapeDtypeStruct((batch, NSA_GROUPS, nqb, ns, Q_BLOCK), F32),
            jax.ShapeDtypeStruct((batch, NSA_GROUPS, nqb, 8, ns), F32)]
    assert nqb % SEL_TILES == 0 and RANK_FIRST_TILE % SEL_TILES == 0
    return pl.pallas_call(
        _nsa_a_kernel,
        out_shape=outs,
        grid=(batch, nqb // SEL_TILES),
        in_specs=[pl.BlockSpec((1, NSA_WIDTH, SEL_TILES * Q_BLOCK), lambda b, i: (b, 0, i)),
                  pl.BlockSpec((2, 1, ncmp, KV_WIDTH), lambda b, i: (0, b, 0, 0)),
                  _const_spec(c_cmp.shape), _const_spec(d_cmp.shape), _const_spec(srow.shape),
                  _const_spec(ovt.shape)],
        out_specs=[pl.BlockSpec((1, NSA_WIDTH, SEL_TILES * Q_BLOCK), lambda b, i: (b, 0, i)),
                   pl.BlockSpec((1, NSA_GROUPS, SEL_TILES, ns, Q_BLOCK), lambda b, i: (b, 0, i, 0, 0)),
                   pl.BlockSpec((1, NSA_GROUPS, SEL_TILES, 8, ns), lambda b, i: (b, 0, i, 0, 0))],
        scratch_shapes=[pltpu.VMEM((SEL_TILES, NSA_GROUPS, ns, Q_BLOCK), F32)],
        compiler_params=_params(("parallel", "parallel")),
        name="nsa_select",
    )(qt, ckv, c_cmp, d_cmp, srow, ovt)


def _block_softmax_step(carry, raw, consts, shifts, v_rows, g):
    m, l, acc = carry
    width = raw.shape[1]
    fold = lambda x: x.reshape(SEL_BLOCK // 8, 8, width)
    us = [raw[i * SEL_BLOCK:(i + 1) * SEL_BLOCK] + c for i, c in enumerate(consts)]
    top8 = None
    for u, sh in zip(us, shifts):
        t = jnp.max(fold(u), axis=0) + sh
        top8 = t if top8 is None else jnp.maximum(top8, t)
    m_new = jnp.maximum(m, jnp.max(top8, axis=0, keepdims=True))
    m_safe = jnp.where(m_new == NEG_INF, 0.0, m_new)
    alpha = jnp.where(m == NEG_INF, 0.0, jnp.exp2(m - m_safe))
    ps = [jnp.exp2((u - (m_safe - sh)).astype(BF16)) for u, sh in zip(us, shifts)]
    lane_grp = lax.broadcasted_iota(jnp.int32, v_rows.shape, 1) // NSA_DH
    v_aug = jnp.where(lane_grp == g, v_rows, jnp.ones_like(v_rows))
    pv = _dot_tn(v_aug, jnp.concatenate(ps, axis=0))
    og = 1 - g
    return m_new, alpha * l + pv[og * NSA_DH:og * NSA_DH + 1], alpha * acc + pv[g * NSA_DH:(g + 1) * NSA_DH]


def _nsa_b_kernel(idx_ref, qt_ref, ks_ref, vs_ref, kw_ref, vw_ref, sel_ref, oc_ref, gt_ref, cblk_ref, cwin_ref,
                  srow_ref, o_ref, *, idx_words):
    b = pl.program_id(0)
    first_tile = pl.program_id(1) * ATT_TILES
    width = NSA_HPG * Q_BLOCK
    units = [(k, g) for k in range(ATT_TILES) for g in range(NSA_GROUPS)]
    qbs = [first_tile + k for k in range(ATT_TILES)]
    bases = [qb * Q_BLOCK for qb in qbs]
    qts = {(k, g): _group_queries(qt_ref, g, k) for k, g in units}
    srows = [srow_ref[g] for g in range(NSA_GROUPS)]
    n_tiles = pl.num_programs(1) * ATT_TILES
    entries = {(k, g): ((b * NSA_GROUPS + g) * n_tiles + qbs[k]) * idx_words for k, g in units}
    n_act = {u: idx_ref[entries[u]] for u in units}
    init = (jnp.full((1, width), NEG_INF, F32), jnp.zeros((1, width), F32), jnp.zeros((NSA_DH, width), F32))
    zero_row = jnp.zeros((1, width), F32)

    def gate_row(k, g, j, live):
        picked = sel_ref[0, g, k, pl.ds(j, 1), :]
        gate = jnp.where((picked > 0.0) & live, 0.0, NEG_INF)
        return jnp.concatenate([gate] * NSA_HPG, axis=1)

    def stage(it):
        out = {}
        for k, g in units:
            words = [idx_ref[entries[k, g] + 1 + it * SLC_WORDS + w] for w in range(SLC_WORDS)]
            kts, vts, shifts = [], [], []
            for i in range(SLC_CHUNK):
                live = it * SLC_CHUNK + i < n_act[k, g]
                j = jnp.where(live, lax.shift_right_logical(words[i // 4], 8 * (i % 4)) & 0xFF, 0)
                k0 = pl.multiple_of(j * SEL_BLOCK, SEL_BLOCK)
                kts.append(ks_ref[pl.ds(k0, SEL_BLOCK), :])
                vts.append(vs_ref[pl.ds(k0, SEL_BLOCK), :])
                shifts.append(gate_row(k, g, j, live) - srows[g] * (bases[k] - k0).astype(F32))
            out[k, g] = (_dot(jnp.concatenate(kts, axis=0), qts[k, g]), shifts, jnp.concatenate(vts, axis=0))
        return out

    def body(it, states):
        staged = stage(it)
        return tuple(_block_softmax_step(state, staged[u][0], [cblk_ref[u[1]]] * SLC_CHUNK, staged[u][1],
                                         staged[u][2], u[1]) for u, state in zip(units, states))

    longest = functools.reduce(jnp.maximum, [n_act[u] for u in units])
    far = lax.fori_loop(0, (longest + SLC_CHUNK - 1) // SLC_CHUNK, body, (init,) * len(units))
    slc = dict(zip(units, far))

    diag = range(WINDOW // SEL_BLOCK, WIN_KEYS // SEL_BLOCK)
    n_win = WIN_KEYS // SEL_BLOCK
    win = {u: init for u in units}
    steps = []
    for k, g in units:
        r0 = pl.multiple_of(bases[k], Q_BLOCK)
        shifts = [gate_row(k, g, 2 * qbs[k] + i, True) for i in range(len(diag))]
        steps.append((slc, (k, g), ks_ref[pl.ds(r0, Q_BLOCK), :], vs_ref[pl.ds(r0, Q_BLOCK), :], diag, shifts))
    for k in range(ATT_TILES):
        for c in range(WIN_KEYS // WIN_CHUNK):
            blocks = range(c * n_win // 2, (c + 1) * n_win // 2)
            kws, vws, offs = [], [], []
            for r in blocks:
                j = 2 * qbs[k] - WINDOW // SEL_BLOCK + r
                k0 = pl.multiple_of(jnp.maximum(j, 0) * SEL_BLOCK, SEL_BLOCK)
                kws.append(kw_ref[pl.ds(k0, SEL_BLOCK), :])
                vws.append(vw_ref[pl.ds(k0, SEL_BLOCK), :])
                offs.append(zero_row + jnp.where(j >= 0, 0.0, NEG_INF))
            kw = jnp.concatenate(kws, axis=0)
            vw = jnp.concatenate(vws, axis=0)
            for g in range(NSA_GROUPS):
                steps.append((win, (k, g), kw, vw, blocks, offs))
    raws = [_dot(keys, qts[u]) for _, u, keys, _, _, _ in steps]
    for raw, (states, u, _, values, blocks, shifts) in zip(raws, steps):
        consts = [cwin_ref[u[1], r * SEL_BLOCK:(r + 1) * SEL_BLOCK, :] for r in blocks]
        states[u] = _block_softmax_step(states[u], raw, consts, shifts, values, u[1])

    for k in range(ATT_TILES):
        cols = slice(k * Q_BLOCK, (k + 1) * Q_BLOCK)
        heads_out = []
        for g in range(NSA_GROUPS):
            o_s = slc[k, g][2] * (1.0 / jnp.maximum(slc[k, g][1], 1e-30))
            o_w = win[k, g][2] * (1.0 / jnp.maximum(win[k, g][1], 1e-30))
            for hh in range(NSA_HPG):
                h = g * NSA_HPG + hh
                cs = slice(hh * Q_BLOCK, (hh + 1) * Q_BLOCK)
                gates = gt_ref[0, 3 * h:3 * h + 3, cols]
                heads_out.append(gates[0:1] * oc_ref[0, h * NSA_DH:(h + 1) * NSA_DH, cols]
                                 + gates[1:2] * o_s[:, cs] + gates[2:3] * o_w[:, cs])
        o_ref[cols, :] = jnp.concatenate(heads_out, axis=0).T.astype(BF16)


def _nsa_b(idx, qt, kvs, sel, oc, gt, c_blk, c_win, srow):
    batch, _, t = qt.shape
    nqb = t // Q_BLOCK
    ns = t // SEL_BLOCK
    kv_spec = lambda c: pl.BlockSpec((t, KV_WIDTH), lambda b, i, idx_ref: (b, c))
    grid_spec = pltpu.PrefetchScalarGridSpec(
        num_scalar_prefetch=1,
        grid=(batch, nqb // ATT_TILES),
        in_specs=[pl.BlockSpec((1, NSA_WIDTH, ATT_TILES * Q_BLOCK), lambda b, i, r: (b, 0, i)),
                  kv_spec(0), kv_spec(1), kv_spec(2), kv_spec(3),
                  pl.BlockSpec((1, NSA_GROUPS, ATT_TILES, ns, Q_BLOCK), lambda b, i, r: (b, 0, i, 0, 0)),
                  pl.BlockSpec((1, NSA_WIDTH, ATT_TILES * Q_BLOCK), lambda b, i, r: (b, 0, i)),
                  pl.BlockSpec((1, gt.shape[1], ATT_TILES * Q_BLOCK), lambda b, i, r: (b, 0, i)),
                  pl.BlockSpec(c_blk.shape, lambda b, i, r: (0, 0, 0)),
                  pl.BlockSpec(c_win.shape, lambda b, i, r: (0, 0, 0)),
                  pl.BlockSpec(srow.shape, lambda b, i, r: (0, 0, 0))],
        out_specs=pl.BlockSpec((ATT_TILES * Q_BLOCK, NSA_WIDTH), lambda b, i, r: (b * (nqb // ATT_TILES) + i, 0)),
    )
    assert nqb % ATT_TILES == 0
    return pl.pallas_call(
        functools.partial(_nsa_b_kernel, idx_words=1 + _slc_steps(ns) * SLC_WORDS),
        out_shape=jax.ShapeDtypeStruct((batch * t, NSA_WIDTH), BF16),
        grid_spec=grid_spec,
        compiler_params=_params(("parallel", "parallel")),
        name="nsa_attend",
    )(idx, qt, kvs, kvs, kvs, kvs, sel, oc, gt, c_blk, c_win, srow)


def _pad_cols(w, width):
    return jnp.pad(w, ((0, 0), (0, width - w.shape[1])))


def _compress_weights(w1, pe, w2):
    nsub = CMP_STRIDE
    w1r = w1.reshape(2, nsub, NSA_DH, CMP_HIDDEN)
    same_group = jnp.eye(NSA_GROUPS, dtype=F32)
    big = same_group[None, :, None, :, None, None] * w1r.transpose(1, 2, 0, 3)[:, None, :, None, :, :]
    big = big.reshape(nsub * KV_WIDTH, NSA_GROUPS * 2 * CMP_HIDDEN)
    per = pe.reshape(2, nsub, 1, NSA_DH)
    pe2 = jnp.broadcast_to(per, (2, nsub, NSA_GROUPS, NSA_DH)).reshape(2, 1, nsub * KV_WIDTH)
    pe2 = jnp.broadcast_to(pe2, (2, 8, nsub * KV_WIDTH)).reshape(16, nsub * KV_WIDTH)
    w2bd = same_group[:, None, :, None] * w2[None, :, None, :]
    return big.astype(BF16), pe2.astype(BF16), w2bd.reshape(NSA_GROUPS * CMP_HIDDEN, KV_WIDTH).astype(BF16)


def _selection_overlap_t(ncmp_pad, ns):
    cs = np.arange(ncmp_pad) * CMP_STRIDE
    ss = np.arange(ns) * SEL_BLOCK
    ov = np.minimum(cs[:, None] + CMP_BLOCK, ss[None, :] + SEL_BLOCK) - np.maximum(cs[:, None], ss[None, :])
    return (np.clip(ov, 0, None).astype(np.float32) / CMP_BLOCK).T


def _slc_steps(ns):
    return -(-ns // SLC_CHUNK)


def _active_blocks(cnt):
    ns = cnt.shape[-1]
    before_tile = jnp.arange(ns)[None, :] < (Q_BLOCK // SEL_BLOCK) * jnp.arange(cnt.shape[2])[:, None]
    flags = (cnt[:, :, :, 0, :] > 0.0) & before_tile
    order = jnp.argsort(jnp.where(flags, 0, 1).astype(jnp.int32), axis=-1, stable=True).astype(jnp.int32)
    n_act = jnp.sum(flags, axis=-1, dtype=jnp.int32)
    lead = order.shape[:-1]
    steps = _slc_steps(ns)
    order = jnp.pad(order, [(0, 0)] * len(lead) + [(0, steps * SLC_CHUNK - ns)]).reshape(lead + (steps, SLC_CHUNK))
    order = jnp.pad(order, [(0, 0)] * (len(lead) + 1) + [(0, 4 * SLC_WORDS - SLC_CHUNK)])
    packed = jnp.sum(order.reshape(lead + (steps * SLC_WORDS, 4)) << (8 * jnp.arange(4, dtype=jnp.int32)), axis=-1)
    return jnp.concatenate([n_act[..., None], packed], axis=-1).reshape(-1)


def kernel(x, mem, g_mix, g_ffn, g_mem, w_mem_kv, w_up, conv_w, conv_b, w_down,
           a_w_in, a_w_alpha, a_b_alpha, a_g_head, a_w_out,
           g_kv, w_kv, pe_k, pe_v, w_ck1, w_ck2, w_cv1, w_cv2,
           b_w_in, b_w_out, g_final):
    batch, t, d = x.shape
    n = batch * t
    m = mem.shape[1]
    tm = min(1024, t)
    xf = x.reshape(n, d)
    memf = mem.reshape(batch * m, d)
    row = lambda v: v.reshape(1, -1)

    mkv0 = _rms_proj(memf, row(g_mem[0]), w_mem_kv[0].astype(BF16), m, BF16).reshape(batch, m, 2 * MEM_WIDTH)
    wa = a_w_in[0]
    c_alr = 2 * GLA_QK + 2 * GLA_V
    w_a = jnp.concatenate([wa[:, :c_alr], wa[:, c_alr + GLA_RANK:], _pad_cols(wa[:, c_alr:c_alr + GLA_RANK], LANE)],
                          axis=1).astype(BF16)
    w_alpha = jnp.pad(a_w_alpha[0], ((0, LANE - GLA_RANK), (0, 0))).astype(BF16)
    q, k, gl, v, r, mq = _inproj_a(xf, row(g_mix[0]), w_a, w_alpha, row(a_b_alpha[0]), tm)
    o = _gla(q, k, gl, v, r, row(a_g_head[0]), batch, tm)
    xf = _attn_out(o, mq, mkv0, a_w_out[0].astype(BF16), xf, batch, tm)
    xf = _ffn(xf, row(g_ffn[0]), w_up[0].astype(BF16), conv_w[0], row(conv_b[0]), w_down[0].astype(BF16),
              row(g_final), batch, tm, final_norm=False)

    wb = b_w_in[0]
    wqt = wb[:, :NSA_WIDTH].T.astype(BF16)
    n_gate = 3 * NSA_HEADS
    wgt = jnp.pad(wb[:, NSA_WIDTH:NSA_WIDTH + n_gate].T, ((0, 32 - n_gate), (0, 0))).astype(BF16)
    wmq = wb[:, NSA_WIDTH + n_gate:].astype(BF16)
    ckv_in, kvs, qt, gt, mq1 = _proj_b(xf, row(g_kv), row(g_mix[1]), w_kv.astype(BF16), wqt, wgt, wmq, batch, tm)
    nsub = t // CMP_STRIDE
    w1k, pek, w2k = _compress_weights(w_ck1, pe_k, w_ck2)
    w1v, pev, w2v = _compress_weights(w_cv1, pe_v, w_cv2)
    ckv = _compress(ckv_in.reshape(2, batch, nsub, CMP_STRIDE * KV_WIDTH),
                    jnp.stack([w1k, w1v]), jnp.stack([pek, pev]), jnp.stack([w2k, w2v]))

    ns = t // SEL_BLOCK
    c_cmp, d_cmp, c_blk, c_win, srow = _alibi_tiles(nsub)
    ovt = jnp.asarray(_selection_overlap_t(nsub, ns), BF16)
    oc, sel, cnt = _nsa_a(qt, ckv, jnp.asarray(c_cmp), jnp.asarray(d_cmp), jnp.asarray(srow), ovt)
    idx = _active_blocks(cnt)
    o1 = _nsa_b(idx, qt, kvs, sel, oc, gt, jnp.asarray(c_blk), jnp.asarray(c_win), jnp.asarray(srow))
    mkv1 = _rms_proj(memf, row(g_mem[1]), w_mem_kv[1].astype(BF16), m, BF16).reshape(batch, m, 2 * MEM_WIDTH)
    xf = _attn_out(o1, mq1, mkv1, b_w_out[0].astype(BF16), xf, batch, tm)
    xf = _ffn(xf, row(g_ffn[1]), w_up[1].astype(BF16), conv_w[1], row(conv_b[1]), w_down[1].astype(BF16),
              row(g_final), batch, tm, final_norm=True)
    return xf.reshape(batch, t, d)
```

```python
import functools

import numpy as np
import jax
import jax.numpy as jnp
from jax import lax
from jax.experimental import pallas as pl
from jax.experimental.pallas import tpu as pltpu

F32 = jnp.float32
BF16 = jnp.bfloat16
EPS = 1e-6
NEG_INF = float("-inf")
LOG2E = 1.4426950408889634

V7X_VMEM_BYTES = 64 * 1024 * 1024
VMEM_LIMIT = V7X_VMEM_BYTES - 8 * 1024 * 1024

D_MODEL = 1024
MEM_HEADS = 4
MEM_DH = 128
MEM_WIDTH = MEM_HEADS * MEM_DH
GLA_HEADS = 4
GLA_DK = 64
GLA_DV = 128
GLA_QK = GLA_HEADS * GLA_DK
GLA_V = GLA_HEADS * GLA_DV
GLA_RANK = 16
GLA_TAU = 16.0
GLA_CHUNK = 64
GLA_LEVELS = 6
NSA_HEADS = 8
NSA_GROUPS = 2
NSA_HPG = NSA_HEADS // NSA_GROUPS
NSA_DH = 64
NSA_WIDTH = NSA_HEADS * NSA_DH
KV_WIDTH = NSA_GROUPS * NSA_DH
CMP_BLOCK = 32
CMP_STRIDE = 16
CMP_HIDDEN = 256
SEL_BLOCK = 64
SEL_TOPN = 16
WINDOW = 512
Q_BLOCK = 128
WIN_KEYS = WINDOW + Q_BLOCK
WIN_CHUNK = WIN_KEYS // 2
CMP_CHUNK = 128
SLC_CHUNK = 6
SLC_WORDS = -(-SLC_CHUNK // 4)
RANK_FIRST_TILE = SEL_TOPN * SEL_BLOCK // Q_BLOCK
SEL_TILES = 4
ATT_TILES = 4
FFN_DIM = 2816
FFN_CHUNK = 256
LANE = 128


def _dot(a, b):
    return jnp.dot(a, b, preferred_element_type=F32)


def _dot_nt(a, b):
    return lax.dot_general(a, b, (((1,), (1,)), ((), ())), preferred_element_type=F32)


def _dot_tn(a, b):
    return lax.dot_general(a, b, (((0,), (0,)), ((), ())), preferred_element_type=F32)


def _params(sem):
    return pltpu.CompilerParams(dimension_semantics=sem, vmem_limit_bytes=VMEM_LIMIT)


def _const_spec(shape):
    n = len(shape)
    return pl.BlockSpec(shape, lambda *_: (0,) * n)


def _normalize(x):
    return x * lax.rsqrt(jnp.mean(x * x, axis=-1, keepdims=True) + EPS)


def _sigmoid(x):
    return 1.0 / (1.0 + jnp.exp(-x))


def _rms_proj_kernel(x_ref, g_ref, w_ref, o_ref):
    h = (_normalize(x_ref[...]) * g_ref[...]).astype(BF16)
    o_ref[...] = _dot(h, w_ref[...]).astype(o_ref.dtype)


def _rms_proj(x, g, w, tm, out_dtype):
    n, d = x.shape
    p = w.shape[1]
    return pl.pallas_call(
        _rms_proj_kernel,
        out_shape=jax.ShapeDtypeStruct((n, p), out_dtype),
        grid=(n // tm,),
        in_specs=[pl.BlockSpec((tm, d), lambda i: (i, 0)), _const_spec((1, d)), _const_spec((d, p))],
        out_specs=pl.BlockSpec((tm, p), lambda i: (i, 0)),
        compiler_params=_params(("parallel",)),
        name="rms_proj",
    )(x, g, w)


def _inproj_a_kernel(x_ref, g_ref, w_ref, wa_ref, ba_ref, q_ref, k_ref, gl_ref, v_ref, r_ref, mq_ref):
    h = (_normalize(x_ref[...]) * g_ref[...]).astype(BF16)
    c = 0
    q_ref[...] = _dot(h, w_ref[:, c:c + GLA_QK]) * (GLA_DK ** -0.5)
    c += GLA_QK
    k_ref[...] = _dot(h, w_ref[:, c:c + GLA_QK])
    c += GLA_QK
    v_ref[...] = _dot(h, w_ref[:, c:c + GLA_V]).astype(BF16)
    c += GLA_V
    r_ref[...] = _dot(h, w_ref[:, c:c + GLA_V])
    c += GLA_V
    mq_ref[...] = _dot(h, w_ref[:, c:c + MEM_WIDTH]).astype(BF16)
    c += MEM_WIDTH
    alr = _dot(h, w_ref[:, c:c + LANE]).astype(BF16)
    z = _dot(alr, wa_ref[...]) + ba_ref[...]
    log_sig = jnp.minimum(z, 0.0) - jnp.log1p(jnp.exp(-jnp.abs(z)))
    gl_ref[...] = log_sig * (1.0 / GLA_TAU)


def _inproj_a(x, g, w, wa, ba, tm):
    n, d = x.shape
    row = lambda i: (i, 0)
    outs = [
        jax.ShapeDtypeStruct((n, GLA_QK), F32), jax.ShapeDtypeStruct((n, GLA_QK), F32),
        jax.ShapeDtypeStruct((n, GLA_QK), F32), jax.ShapeDtypeStruct((n, GLA_V), BF16),
        jax.ShapeDtypeStruct((n, GLA_V), F32), jax.ShapeDtypeStruct((n, MEM_WIDTH), BF16),
    ]
    return pl.pallas_call(
        _inproj_a_kernel,
        out_shape=outs,
        grid=(n // tm,),
        in_specs=[pl.BlockSpec((tm, d), row), _const_spec((1, d)), _const_spec(w.shape),
                  _const_spec(wa.shape), _const_spec(ba.shape)],
        out_specs=[pl.BlockSpec((tm, s.shape[1]), row) for s in outs],
        compiler_params=_params(("parallel",)),
        name="inproj_a",
    )(x, g, w, wa, ba)


def _gla_constants():
    c = GLA_CHUNK
    w = np.zeros((8 * c, c), np.float32)
    masks = np.zeros((GLA_LEVELS + 1, c, c), np.float32)
    masks[0] = np.eye(c)
    for l in range(1, GLA_LEVELS + 1):
        blk, half = 2 ** l, 2 ** (l - 1)
        for i in range(c):
            mid = (i // blk) * blk + half - 1
            if i % blk >= half:
                w[(l - 1) * c + i, mid + 1:i + 1] = 1.0
            else:
                w[(l - 1) * c + i, i + 1:mid + 1] = 1.0
        for t in range(c):
            for s in range(c):
                if t // blk == s // blk and t % blk >= half and s % blk < half:
                    masks[l, t, s] = 1.0
    for i in range(c):
        w[6 * c + i, :i + 1] = 1.0
        w[7 * c + i, i + 1:] = 1.0
    return np.concatenate([w, w], axis=1), np.tile(masks, (1, 1, GLA_HEADS))


def _gla_kernel(q_ref, k_ref, gl_ref, v_ref, r_ref, gh_ref, wcat_ref, msk_ref, o_ref, st_ref, *, n_chunks):
    c = GLA_CHUNK

    @pl.when(pl.program_id(1) == 0)
    def _():
        st_ref[...] = jnp.zeros_like(st_ref)

    lane_qk = lax.broadcasted_iota(jnp.int32, (1, GLA_QK), 1) // GLA_DK
    lane_v = lax.broadcasted_iota(jnp.int32, (1, GLA_V), 1) // GLA_DV
    row_qk = lax.broadcasted_iota(jnp.int32, (GLA_QK, 1), 0) // GLA_DK
    ones = jnp.ones((2 * c, LANE), BF16)
    wcat = wcat_ref[...]

    def stack_heads(x, lane_head):
        return jnp.concatenate([jnp.where(lane_head == h, x, jnp.zeros_like(x)) for h in range(GLA_HEADS)], axis=0)

    chunks = [slice(ci * c, (ci + 1) * c) for ci in range(n_chunks)]
    es, decs = [], []
    for rows in chunks:
        g = gl_ref[rows, :] * LOG2E
        g_hi = g.astype(BF16)
        g_split = jnp.concatenate([g_hi, (g - g_hi.astype(F32)).astype(BF16)], axis=0)
        es.append(jnp.exp2(_dot(wcat, g_split)))
        decs.append(jnp.exp2(_dot_tn(g_split, ones)))

    o_intras, q_ins, upds = [], [], []
    for rows, e in zip(chunks, es):
        q = q_ref[rows, :]
        k = k_ref[rows, :]
        v = v_ref[rows, :]
        attn = jnp.zeros((c, GLA_HEADS * c), F32)
        for l in range(GLA_LEVELS + 1):
            if l == 0:
                ql, kl = q, k
            else:
                el = e[(l - 1) * c:l * c]
                ql, kl = q * el, k * el
            attn = attn + _dot_nt(ql.astype(BF16), stack_heads(kl, lane_qk).astype(BF16)) * msk_ref[l]
        o_intras.append(_dot(attn.astype(BF16), stack_heads(v, lane_v)))
        q_ins.append((q * e[6 * c:7 * c]).astype(BF16))
        kk = stack_heads(k * e[7 * c:8 * c], lane_qk).astype(BF16)
        v_rows = jnp.concatenate([v[:, h * GLA_DV:(h + 1) * GLA_DV] for h in range(GLA_HEADS)], axis=0)
        upds.append(_dot_tn(kk, v_rows))

    st = st_ref[...]
    outs = []
    for o_intra, q_in, dec, upd in zip(o_intras, q_ins, decs, upds):
        st_bd = jnp.concatenate([jnp.where(row_qk == h, st, 0.0).astype(BF16) for h in range(GLA_HEADS)], axis=1)
        outs.append(o_intra + _dot(q_in, st_bd))
        st = dec * st + upd
    st_ref[...] = st

    for rows, o in zip(chunks, outs):
        for h in range(GLA_HEADS):
            cs = slice(h * GLA_DV, (h + 1) * GLA_DV)
            on = _normalize(o[:, cs]) * gh_ref[...]
            r = r_ref[rows, cs]
            o_ref[rows, cs] = (on * (r * _sigmoid(r))).astype(BF16)


def _gla(q, k, gl, v, r, g_head, batch, ct):
    n = q.shape[0]
    t = n // batch
    nt = t // ct
    wcat, masks = _gla_constants()
    row = lambda b, i: (b * nt + i, 0)
    return pl.pallas_call(
        functools.partial(_gla_kernel, n_chunks=ct // GLA_CHUNK),
        out_shape=jax.ShapeDtypeStruct((n, GLA_V), BF16),
        grid=(batch, nt),
        in_specs=[pl.BlockSpec((ct, GLA_QK), row), pl.BlockSpec((ct, GLA_QK), row),
                  pl.BlockSpec((ct, GLA_QK), row), pl.BlockSpec((ct, GLA_V), row),
                  pl.BlockSpec((ct, GLA_V), row), _const_spec((1, GLA_DV)),
                  _const_spec(wcat.shape), _const_spec(masks.shape)],
        out_specs=pl.BlockSpec((ct, GLA_V), row),
        scratch_shapes=[pltpu.VMEM((GLA_QK, GLA_DV), F32)],
        compiler_params=_params(("parallel", "arbitrary")),
        name="gla",
    )(q, k, gl, v, r, g_head, jnp.asarray(wcat, BF16), jnp.asarray(masks, F32))


def _attn_out_kernel(o_ref, mq_ref, mk_ref, mv_ref, w_ref, x_ref, out_ref):
    heads = [slice(h * MEM_DH, (h + 1) * MEM_DH) for h in range(MEM_HEADS)]
    scores = [_dot_nt(mq_ref[:, cs], mk_ref[0, :, cs]) * (MEM_DH ** -0.5 * LOG2E) for cs in heads]
    parts = [o_ref[...]]
    for s, cs in zip(scores, heads):
        e = jnp.exp2(s - jnp.max(s, axis=-1, keepdims=True))
        p = e / jnp.sum(e, axis=-1, keepdims=True)
        parts.append(_dot(p.astype(BF16), mv_ref[0, :, cs]).astype(BF16))
    cat = jnp.concatenate(parts, axis=-1)
    out_ref[...] = x_ref[...] + _dot(cat, w_ref[...])


def _attn_out(o, mq, mkv, w_out, x, batch, tm):
    n, d = x.shape
    t = n // batch
    nt = t // tm
    m = mkv.shape[1]
    row = lambda b, i: (b * nt + i, 0)
    return pl.pallas_call(
        _attn_out_kernel,
        out_shape=jax.ShapeDtypeStruct((n, d), F32),
        grid=(batch, nt),
        in_specs=[pl.BlockSpec((tm, o.shape[1]), row), pl.BlockSpec((tm, MEM_WIDTH), row),
                  pl.BlockSpec((1, m, MEM_WIDTH), lambda b, i: (b, 0, 0)),
                  pl.BlockSpec((1, m, MEM_WIDTH), lambda b, i: (b, 0, 1)),
                  _const_spec(w_out.shape), pl.BlockSpec((tm, d), row)],
        out_specs=pl.BlockSpec((tm, d), row),
        compiler_params=_params(("parallel", "parallel")),
        name="attn_out",
    )(o, mq, mkv, mkv, w_out, x)


def _ffn_kernel(x_ref, g_ref, wup_ref, cw_ref, cb_ref, wd_ref, gf_ref, out_ref, act_ref, tail_ref, *,
                final_norm):
    @pl.when(pl.program_id(1) == 0)
    def _():
        tail_ref[...] = jnp.zeros_like(tail_ref)

    x = x_ref[...]
    tm = x.shape[0]
    h = (_normalize(x) * g_ref[...]).astype(BF16)
    rid = lax.broadcasted_iota(jnp.int32, (8, FFN_CHUNK), 0)
    for j in range(FFN_DIM // FFN_CHUNK):
        cs = slice(j * FFN_CHUNK, (j + 1) * FFN_CHUNK)
        a = _dot(h, wup_ref[:, cs])
        b = _dot(h, wup_ref[:, FFN_DIM + j * FFN_CHUNK:FFN_DIM + (j + 1) * FFN_CHUNK])
        tail = tail_ref[j]
        r1 = pltpu.roll(a, 1, 0)
        r2 = pltpu.roll(a, 2, 0)
        top1 = jnp.where(rid == 0, tail[7:8], r1[:8])
        top2 = jnp.where(rid == 0, tail[6:7], jnp.where(rid == 1, tail[7:8], r2[:8]))
        a1 = jnp.concatenate([top1, r1[8:]], axis=0)
        a2 = jnp.concatenate([top2, r2[8:]], axis=0)
        tail_ref[j] = a[tm - 8:]
        ac = a2 * cw_ref[0:1, cs] + a1 * cw_ref[1:2, cs] + a * cw_ref[2:3, cs] + cb_ref[:, cs]
        act_ref[:, cs] = (ac * _sigmoid(ac) * b).astype(BF16)
    y = x + _dot(act_ref[...], wd_ref[...])
    if final_norm:
        y = _normalize(y) * gf_ref[...]
    out_ref[...] = y


def _ffn(x, g, w_up, conv_w, conv_b, w_down, g_final, batch, tm, final_norm):
    n, d = x.shape
    nt = n // batch // tm
    row = lambda b, i: (b * nt + i, 0)
    once = pl.Buffered(1)
    return pl.pallas_call(
        functools.partial(_ffn_kernel, final_norm=final_norm),
        out_shape=jax.ShapeDtypeStruct((n, d), F32),
        grid=(batch, nt),
        in_specs=[pl.BlockSpec((tm, d), row), _const_spec((1, d)),
                  pl.BlockSpec(w_up.shape, lambda b, i: (0, 0), pipeline_mode=once),
                  _const_spec(conv_w.shape), _const_spec(conv_b.shape),
                  pl.BlockSpec(w_down.shape, lambda b, i: (0, 0), pipeline_mode=once),
                  _const_spec((1, d))],
        out_specs=pl.BlockSpec((tm, d), row),
        scratch_shapes=[pltpu.VMEM((tm, FFN_DIM), BF16),
                        pltpu.VMEM((FFN_DIM // FFN_CHUNK, 8, FFN_CHUNK), F32)],
        compiler_params=_params(("parallel", "arbitrary")),
        name="conv_ffn",
    )(x, g, w_up, conv_w, conv_b, w_down, g_final)


def _proj_b_kernel(x_ref, gkv_ref, gmix_ref, wkv_ref, wqt_ref, wgt_ref, wmq_ref,
                   cmp_ref, kvs_ref, qt_ref, gt_ref, mq_ref):
    xn = _normalize(x_ref[...])
    hkv = (xn * gkv_ref[...]).astype(BF16)
    h1 = (xn * gmix_ref[...]).astype(BF16)
    kv_cmp = _dot(hkv, wkv_ref[:, 0:2 * KV_WIDTH]).astype(BF16)
    cmp_ref[0] = kv_cmp[:, 0:KV_WIDTH]
    cmp_ref[1] = kv_cmp[:, KV_WIDTH:2 * KV_WIDTH]
    kvs_ref[...] = _dot(hkv, wkv_ref[:, 2 * KV_WIDTH:]).astype(BF16)
    qt_ref[0] = (_dot_nt(wqt_ref[...], h1) * (NSA_DH ** -0.5 * LOG2E)).astype(BF16)
    gt_ref[0] = _sigmoid(_dot_nt(wgt_ref[...], h1))
    mq_ref[...] = _dot(h1, wmq_ref[...]).astype(BF16)


def _proj_b(x, g_kv, g_mix, w_kv, wqt, wgt, wmq, batch, tm):
    n, d = x.shape
    t = n // batch
    nt = t // tm
    row = lambda b, i: (b * nt + i, 0)
    outs = [
        jax.ShapeDtypeStruct((2, n, KV_WIDTH), BF16),
        jax.ShapeDtypeStruct((n, 4 * KV_WIDTH), BF16),
        jax.ShapeDtypeStruct((batch, NSA_WIDTH, t), BF16),
        jax.ShapeDtypeStruct((batch, wgt.shape[0], t), F32),
        jax.ShapeDtypeStruct((n, MEM_WIDTH), BF16),
    ]
    return pl.pallas_call(
        _proj_b_kernel,
        out_shape=outs,
        grid=(batch, nt),
        in_specs=[pl.BlockSpec((tm, d), row), _const_spec((1, d)), _const_spec((1, d)),
                  _const_spec(w_kv.shape), _const_spec(wqt.shape), _const_spec(wgt.shape),
                  _const_spec(wmq.shape)],
        out_specs=[pl.BlockSpec((2, tm, KV_WIDTH), lambda b, i: (0, b * nt + i, 0)),
                   pl.BlockSpec((tm, 4 * KV_WIDTH), row),
                   pl.BlockSpec((1, NSA_WIDTH, tm), lambda b, i: (b, 0, i)),
                   pl.BlockSpec((1, wgt.shape[0], tm), lambda b, i: (b, 0, i)),
                   pl.BlockSpec((tm, MEM_WIDTH), row)],
        compiler_params=_params(("parallel", "parallel")),
        name="proj_b",
    )(x, g_kv, g_mix, w_kv, wqt, wgt, wmq)


def _compress_kernel(a_ref, w1_ref, pe_ref, w2_ref, o_ref):
    a = a_ref[0, 0]
    w1 = w1_ref[0]
    u = _dot(a, w1)
    cpe = _dot(pe_ref[0], w1)
    nrow = a.shape[0]
    hid = []
    for g in range(NSA_GROUPS):
        c0 = slice((2 * g) * CMP_HIDDEN, (2 * g + 1) * CMP_HIDDEN)
        c1 = slice((2 * g + 1) * CMP_HIDDEN, (2 * g + 2) * CMP_HIDDEN)
        nxt = pltpu.roll(u[:, c1], nrow - 1, 0)
        hid.append(u[:, c0] + nxt + cpe[0:1, c0] + cpe[8:9, c1])
    hcat = jax.nn.gelu(jnp.concatenate(hid, axis=-1), approximate=True).astype(BF16)
    o_ref[0, 0] = _dot(hcat, w2_ref[0]).astype(BF16)


def _compress(a, w1, pe, w2):
    _, batch, nsub, width = a.shape
    return pl.pallas_call(
        _compress_kernel,
        out_shape=jax.ShapeDtypeStruct((2, batch, nsub, KV_WIDTH), BF16),
        grid=(2, batch),
        in_specs=[pl.BlockSpec((1, 1, nsub, width), lambda s, b: (s, b, 0, 0)),
                  pl.BlockSpec((1,) + w1.shape[1:], lambda s, b: (s, 0, 0)),
                  pl.BlockSpec((1,) + pe.shape[1:], lambda s, b: (s, 0, 0)),
                  pl.BlockSpec((1,) + w2.shape[1:], lambda s, b: (s, 0, 0))],
        out_specs=pl.BlockSpec((1, 1, nsub, KV_WIDTH), lambda s, b: (s, b, 0, 0)),
        compiler_params=_params(("parallel", "parallel")),
        name="compress",
    )(a, w1, pe, w2)


def _alibi_tiles(ncmp_pad):
    slopes = LOG2E * 2.0 ** (-8.0 * np.arange(1, NSA_HEADS + 1) / NSA_HEADS)
    tl = np.tile(np.arange(Q_BLOCK), NSA_HPG)[None, :]
    cmp_end = (np.arange(ncmp_pad) * CMP_STRIDE + CMP_BLOCK - 1)[:, None]
    key = np.arange(SEL_BLOCK)[:, None]
    dist_win = tl + WINDOW - np.arange(WIN_KEYS)[:, None]
    c_cmp = np.zeros((NSA_GROUPS, ncmp_pad, NSA_HPG * Q_BLOCK), np.float32)
    c_blk = np.zeros((NSA_GROUPS, SEL_BLOCK, NSA_HPG * Q_BLOCK), np.float32)
    c_win = np.zeros((NSA_GROUPS, WIN_KEYS, NSA_HPG * Q_BLOCK), np.float32)
    srow = np.zeros((NSA_GROUPS, 1, NSA_HPG * Q_BLOCK), np.float32)
    for g in range(NSA_GROUPS):
        s = np.repeat(slopes[g * NSA_HPG:(g + 1) * NSA_HPG], Q_BLOCK)[None, :]
        c_cmp[g] = -s * (tl - cmp_end)
        c_blk[g] = -s * (tl - key)
        c_win[g] = np.where((dist_win >= 0) & (dist_win < WINDOW), -s * dist_win, -np.inf)
        srow[g] = s
    d_cmp = (cmp_end - tl).astype(np.float32)
    return c_cmp, d_cmp, c_blk, c_win, srow


def _group_queries(qt_ref, g, tile=0):
    zeros = jnp.zeros((NSA_DH, Q_BLOCK), BF16)
    cols = []
    for hh in range(NSA_HPG):
        h = g * NSA_HPG + hh
        piece = qt_ref[0, h * NSA_DH:(h + 1) * NSA_DH, tile * Q_BLOCK:(tile + 1) * Q_BLOCK]
        cols.append(jnp.concatenate([piece, zeros] if g == 0 else [zeros, piece], axis=0))
    return jnp.concatenate(cols, axis=1)


def _nsa_a_kernel(qt_ref, ckv_ref, ccmp_ref, dcmp_ref, srow_ref, ovt_ref, oc_ref, sel_ref, cnt_ref, imp_ref):
    first_tile = pl.program_id(1) * SEL_TILES
    nblk = sel_ref.shape[3]
    jidx = lax.broadcasted_iota(jnp.int32, (nblk, Q_BLOCK), 0)
    tl = lax.broadcasted_iota(jnp.int32, (nblk, Q_BLOCK), 1)
    units = [(k, g) for k in range(SEL_TILES) for g in range(NSA_GROUPS)]
    qbs = [first_tile + k for k in range(SEL_TILES)]
    bases = [(qb * Q_BLOCK).astype(F32) for qb in qbs]
    curs = [2 * qb + jnp.where(tl >= SEL_BLOCK, 1, 0) for qb in qbs]
    n_chunks = (qbs[-1] * (Q_BLOCK // CMP_STRIDE) + (Q_BLOCK - CMP_BLOCK) // CMP_STRIDE) // CMP_CHUNK + 1
    qts = {(k, g): _group_queries(qt_ref, g, k) for k, g in units}
    shifts = {(k, g): srow_ref[g] * bases[k] for k, g in units}

    def cmp_branch(n_rows):
        kc = ckv_ref[0, 0, 0:n_rows, :]
        vc = ckv_ref[1, 0, 0:n_rows, :]
        ov = ovt_ref[:, 0:n_rows]
        lane_grp = lax.broadcasted_iota(jnp.int32, vc.shape, 1) // NSA_DH
        raws = {u: _dot(kc, qts[u]) for u in units}
        for k, g in units:
            visible = dcmp_ref[0:n_rows, :] <= bases[k]
            s = raws[k, g] + jnp.where(visible, ccmp_ref[g, 0:n_rows, :] - shifts[k, g], NEG_INF)
            m = jnp.max(s, axis=0, keepdims=True)
            e = jnp.exp2((s - jnp.where(m == NEG_INF, 0.0, m)).astype(BF16))
            pv = _dot_tn(jnp.where(lane_grp == g, vc, jnp.ones_like(vc)), e)
            inv = 1.0 / jnp.maximum(pv[(1 - g) * NSA_DH:(1 - g) * NSA_DH + 1], 1e-30)
            oc = pv[g * NSA_DH:(g + 1) * NSA_DH] * inv
            imp = _dot(ov, e) * inv
            for hh in range(NSA_HPG):
                h = g * NSA_HPG + hh
                oc_ref[0, h * NSA_DH:(h + 1) * NSA_DH, k * Q_BLOCK:(k + 1) * Q_BLOCK] = \
                    oc[:, hh * Q_BLOCK:(hh + 1) * Q_BLOCK]
            imp_g = imp[:, 0:Q_BLOCK]
            for hh in range(1, NSA_HPG):
                imp_g = imp_g + imp[:, hh * Q_BLOCK:(hh + 1) * Q_BLOCK]
            imp_ref[k, g] = imp_g

    for nc in range(1, ckv_ref.shape[2] // CMP_CHUNK + 1):
        pl.when(n_chunks == nc)(functools.partial(cmp_branch, nc * CMP_CHUNK))

    def emit(k, g, sel):
        sel_ref[0, g, k] = sel
        cnt_ref[0, g, k] = _dot_nt(jnp.ones((8, Q_BLOCK), BF16), sel.astype(BF16))

    @pl.when(first_tile < RANK_FIRST_TILE)
    def _():
        for k, g in units:
            emit(k, g, jnp.where(jidx <= curs[k], 1.0, 0.0))

    def rank_branch(rows):
        jr = lax.broadcasted_iota(jnp.int32, (rows, Q_BLOCK), 0)
        jf = jr.astype(F32)
        cur_r = [2 * qb + jnp.where(lax.broadcasted_iota(jnp.int32, (rows, Q_BLOCK), 1) >= SEL_BLOCK, 1, 0)
                 for qb in qbs]
        v0 = {(k, g): jnp.where((jr >= 1) & (jr <= cur_r[k] - 2), imp_ref[k, g, 0:rows, :], NEG_INF)
              for k, g in units}
        v = dict(v0)
        for _ in range(SEL_TOPN - 3):
            for u in units:
                mx = jnp.max(v[u], axis=0, keepdims=True)
                first = jnp.min(jnp.where(v[u] == mx, jf, float(nblk)), axis=0, keepdims=True)
                v[u] = jnp.where(jf == first, NEG_INF, v[u])
        for k, g in units:
            cur = cur_r[k]
            taken = (v[k, g] != v0[k, g]) | (jr == 0) | (jr == cur) | (jr == cur - 1)
            sel = jnp.where(taken, 1.0, 0.0)
            if rows < nblk:
                sel = jnp.concatenate([sel, jnp.zeros((nblk - rows, Q_BLOCK), F32)], axis=0)
            emit(k, g, sel)

    @pl.when(first_tile >= RANK_FIRST_TILE)
    def _():
        blocks_per_variant = CMP_CHUNK * CMP_STRIDE // SEL_BLOCK
        for nc in range(1, ckv_ref.shape[2] // CMP_CHUNK + 1):
            pl.when(n_chunks == nc)(functools.partial(rank_branch, min(nblk, nc * blocks_per_variant)))


def _nsa_a(qt, ckv, c_cmp, d_cmp, srow, ovt):
    batch, _, t = qt.shape
    nqb = t // Q_BLOCK
    ns = t // SEL_BLOCK
    ncmp = ckv.shape[2]
    outs = [jax.ShapeDtypeStruct((batch, NSA_WIDTH, t), F32),
            jax.ShapeDtypeStruct((batch, NSA_GROUPS, nqb, ns, Q_BLOCK), F32),
            jax.ShapeDtypeStruct((batch, NSA_GROUPS, nqb, 8, ns), F32)]
    assert nqb % SEL_TILES == 0 and RANK_FIRST_TILE % SEL_TILES == 0
    return pl.pallas_call(
        _nsa_a_kernel,
        out_shape=outs,
        grid=(batch, nqb // SEL_TILES),
        in_specs=[pl.BlockSpec((1, NSA_WIDTH, SEL_TILES * Q_BLOCK), lambda b, i: (b, 0, i)),
                  pl.BlockSpec((2, 1, ncmp, KV_WIDTH), lambda b, i: (0, b, 0, 0)),
                  _const_spec(c_cmp.shape), _const_spec(d_cmp.shape), _const_spec(srow.shape),
                  _const_spec(ovt.shape)],
        out_specs=[pl.BlockSpec((1, NSA_WIDTH, SEL_TILES * Q_BLOCK), lambda b, i: (b, 0, i)),
                   pl.BlockSpec((1, NSA_GROUPS, SEL_TILES, ns, Q_BLOCK), lambda b, i: (b, 0, i, 0, 0)),
                   pl.BlockSpec((1, NSA_GROUPS, SEL_TILES, 8, ns), lambda b, i: (b, 0, i, 0, 0))],
        scratch_shapes=[pltpu.VMEM((SEL_TILES, NSA_GROUPS, ns, Q_BLOCK), F32)],
        compiler_params=_params(("parallel", "parallel")),
        name="nsa_select",
    )(qt, ckv, c_cmp, d_cmp, srow, ovt)


def _block_softmax_step(carry, raw, consts, shifts, v_rows, g):
    m, l, acc = carry
    width = raw.shape[1]
    fold = lambda x: x.reshape(SEL_BLOCK // 8, 8, width)
    us = [raw[i * SEL_BLOCK:(i + 1) * SEL_BLOCK] + c for i, c in enumerate(consts)]
    top8 = None
    for u, sh in zip(us, shifts):
        t = jnp.max(fold(u), axis=0) + sh
        top8 = t if top8 is None else jnp.maximum(top8, t)
    m_new = jnp.maximum(m, jnp.max(top8, axis=0, keepdims=True))
    m_safe = jnp.where(m_new == NEG_INF, 0.0, m_new)
    alpha = jnp.where(m == NEG_INF, 0.0, jnp.exp2(m - m_safe))
    ps = [jnp.exp2((u - (m_safe - sh)).astype(BF16)) for u, sh in zip(us, shifts)]
    lane_grp = lax.broadcasted_iota(jnp.int32, v_rows.shape, 1) // NSA_DH
    v_aug = jnp.where(lane_grp == g, v_rows, jnp.ones_like(v_rows))
    pv = _dot_tn(v_aug, jnp.concatenate(ps, axis=0))
    og = 1 - g
    return m_new, alpha * l + pv[og * NSA_DH:og * NSA_DH + 1], alpha * acc + pv[g * NSA_DH:(g + 1) * NSA_DH]


def _nsa_b_kernel(idx_ref, qt_ref, ks_ref, vs_ref, kw_ref, vw_ref, sel_ref, oc_ref, gt_ref, cblk_ref, cwin_ref,
                  srow_ref, o_ref, *, idx_words):
    b = pl.program_id(0)
    first_tile = pl.program_id(1) * ATT_TILES
    width = NSA_HPG * Q_BLOCK
    units = [(k, g) for k in range(ATT_TILES) for g in range(NSA_GROUPS)]
    qbs = [first_tile + k for k in range(ATT_TILES)]
    bases = [qb * Q_BLOCK for qb in qbs]
    qts = {(k, g): _group_queries(qt_ref, g, k) for k, g in units}
    srows = [srow_ref[g] for g in range(NSA_GROUPS)]
    n_tiles = pl.num_programs(1) * ATT_TILES
    entries = {(k, g): ((b * NSA_GROUPS + g) * n_tiles + qbs[k]) * idx_words for k, g in units}
    n_act = {u: idx_ref[entries[u]] for u in units}
    init = (jnp.full((1, width), NEG_INF, F32), jnp.zeros((1, width), F32), jnp.zeros((NSA_DH, width), F32))
    zero_row = jnp.zeros((1, width), F32)

    def gate_row(k, g, j, live):
        picked = sel_ref[0, g, k, pl.ds(j, 1), :]
        gate = jnp.where((picked > 0.0) & live, 0.0, NEG_INF)
        return jnp.concatenate([gate] * NSA_HPG, axis=1)

    def stage(it):
        out = {}
        for k, g in units:
            words = [idx_ref[entries[k, g] + 1 + it * SLC_WORDS + w] for w in range(SLC_WORDS)]
            kts, vts, shifts = [], [], []
            for i in range(SLC_CHUNK):
                live = it * SLC_CHUNK + i < n_act[k, g]
                j = jnp.where(live, lax.shift_right_logical(words[i // 4], 8 * (i % 4)) & 0xFF, 0)
                k0 = pl.multiple_of(j * SEL_BLOCK, SEL_BLOCK)
                kts.append(ks_ref[pl.ds(k0, SEL_BLOCK), :])
                vts.append(vs_ref[pl.ds(k0, SEL_BLOCK), :])
                shifts.append(gate_row(k, g, j, live) - srows[g] * (bases[k] - k0).astype(F32))
            out[k, g] = (_dot(jnp.concatenate(kts, axis=0), qts[k, g]), shifts, jnp.concatenate(vts, axis=0))
        return out

    def body(it, states):
        staged = stage(it)
        return tuple(_block_softmax_step(state, staged[u][0], [cblk_ref[u[1]]] * SLC_CHUNK, staged[u][1],
                                         staged[u][2], u[1]) for u, state in zip(units, states))

    longest = functools.reduce(jnp.maximum, [n_act[u] for u in units])
    far = lax.fori_loop(0, (longest + SLC_CHUNK - 1) // SLC_CHUNK, body, (init,) * len(units))
    slc = dict(zip(units, far))

    diag = range(WINDOW // SEL_BLOCK, WIN_KEYS // SEL_BLOCK)
    n_win = WIN_KEYS // SEL_BLOCK
    win = {u: init for u in units}
    steps = []
    for k, g in units:
        r0 = pl.multiple_of(bases[k], Q_BLOCK)
        shifts = [gate_row(k, g, 2 * qbs[k] + i, True) for i in range(len(diag))]
        steps.append((slc, (k, g), ks_ref[pl.ds(r0, Q_BLOCK), :], vs_ref[pl.ds(r0, Q_BLOCK), :], diag, shifts))
    for k in range(ATT_TILES):
        for c in range(WIN_KEYS // WIN_CHUNK):
            blocks = range(c * n_win // 2, (c + 1) * n_win // 2)
            kws, vws, offs = [], [], []
            for r in blocks:
                j = 2 * qbs[k] - WINDOW // SEL_BLOCK + r
                k0 = pl.multiple_of(jnp.maximum(j, 0) * SEL_BLOCK, SEL_BLOCK)
                kws.append(kw_ref[pl.ds(k0, SEL_BLOCK), :])
                vws.append(vw_ref[pl.ds(k0, SEL_BLOCK), :])
                offs.append(zero_row + jnp.where(j >= 0, 0.0, NEG_INF))
            kw = jnp.concatenate(kws, axis=0)
            vw = jnp.concatenate(vws, axis=0)
            for g in range(NSA_GROUPS):
                steps.append((win, (k, g), kw, vw, blocks, offs))
    raws = [_dot(keys, qts[u]) for _, u, keys, _, _, _ in steps]
    for raw, (states, u, _, values, blocks, shifts) in zip(raws, steps):
        consts = [cwin_ref[u[1], r * SEL_BLOCK:(r + 1) * SEL_BLOCK, :] for r in blocks]
        states[u] = _block_softmax_step(states[u], raw, consts, shifts, values, u[1])

    for k in range(ATT_TILES):
        cols = slice(k * Q_BLOCK, (k + 1) * Q_BLOCK)
        heads_out = []
        for g in range(NSA_GROUPS):
            o_s = slc[k, g][2] * (1.0 / jnp.maximum(slc[k, g][1], 1e-30))
            o_w = win[k, g][2] * (1.0 / jnp.maximum(win[k, g][1], 1e-30))
            for hh in range(NSA_HPG):
                h = g * NSA_HPG + hh
                cs = slice(hh * Q_BLOCK, (hh + 1) * Q_BLOCK)
                gates = gt_ref[0, 3 * h:3 * h + 3, cols]
                heads_out.append(gates[0:1] * oc_ref[0, h * NSA_DH:(h + 1) * NSA_DH, cols]
                                 + gates[1:2] * o_s[:, cs] + gates[2:3] * o_w[:, cs])
        o_ref[cols, :] = jnp.concatenate(heads_out, axis=0).astype(BF16).T


def _nsa_b(idx, qt, kvs, sel, oc, gt, c_blk, c_win, srow):
    batch, _, t = qt.shape
    nqb = t // Q_BLOCK
    ns = t // SEL_BLOCK
    kv_spec = lambda c: pl.BlockSpec((t, KV_WIDTH), lambda b, i, idx_ref: (b, c))
    grid_spec = pltpu.PrefetchScalarGridSpec(
        num_scalar_prefetch=1,
        grid=(batch, nqb // ATT_TILES),
        in_specs=[pl.BlockSpec((1, NSA_WIDTH, ATT_TILES * Q_BLOCK), lambda b, i, r: (b, 0, i)),
                  kv_spec(0), kv_spec(1), kv_spec(2), kv_spec(3),
                  pl.BlockSpec((1, NSA_GROUPS, ATT_TILES, ns, Q_BLOCK), lambda b, i, r: (b, 0, i, 0, 0)),
                  pl.BlockSpec((1, NSA_WIDTH, ATT_TILES * Q_BLOCK), lambda b, i, r: (b, 0, i)),
                  pl.BlockSpec((1, gt.shape[1], ATT_TILES * Q_BLOCK), lambda b, i, r: (b, 0, i)),
                  pl.BlockSpec(c_blk.shape, lambda b, i, r: (0, 0, 0)),
                  pl.BlockSpec(c_win.shape, lambda b, i, r: (0, 0, 0)),
                  pl.BlockSpec(srow.shape, lambda b, i, r: (0, 0, 0))],
        out_specs=pl.BlockSpec((ATT_TILES * Q_BLOCK, NSA_WIDTH), lambda b, i, r: (b * (nqb // ATT_TILES) + i, 0)),
    )
    assert nqb % ATT_TILES == 0
    return pl.pallas_call(
        functools.partial(_nsa_b_kernel, idx_words=1 + _slc_steps(ns) * SLC_WORDS),
        out_shape=jax.ShapeDtypeStruct((batch * t, NSA_WIDTH), BF16),
        grid_spec=grid_spec,
        compiler_params=_params(("parallel", "parallel")),
        name="nsa_attend",
    )(idx, qt, kvs, kvs, kvs, kvs, sel, oc, gt, c_blk, c_win, srow)


def _pad_cols(w, width):
    return jnp.pad(w, ((0, 0), (0, width - w.shape[1])))


def _compress_weights(w1, pe, w2):
    nsub = CMP_STRIDE
    w1r = w1.reshape(2, nsub, NSA_DH, CMP_HIDDEN)
    same_group = jnp.eye(NSA_GROUPS, dtype=F32)
    big = same_group[None, :, None, :, None, None] * w1r.transpose(1, 2, 0, 3)[:, None, :, None, :, :]
    big = big.reshape(nsub * KV_WIDTH, NSA_GROUPS * 2 * CMP_HIDDEN)
    per = pe.reshape(2, nsub, 1, NSA_DH)
    pe2 = jnp.broadcast_to(per, (2, nsub, NSA_GROUPS, NSA_DH)).reshape(2, 1, nsub * KV_WIDTH)
    pe2 = jnp.broadcast_to(pe2, (2, 8, nsub * KV_WIDTH)).reshape(16, nsub * KV_WIDTH)
    w2bd = same_group[:, None, :, None] * w2[None, :, None, :]
    return big.astype(BF16), pe2.astype(BF16), w2bd.reshape(NSA_GROUPS * CMP_HIDDEN, KV_WIDTH).astype(BF16)


def _selection_overlap_t(ncmp_pad, ns):
    cs = np.arange(ncmp_pad) * CMP_STRIDE
    ss = np.arange(ns) * SEL_BLOCK
    ov = np.minimum(cs[:, None] + CMP_BLOCK, ss[None, :] + SEL_BLOCK) - np.maximum(cs[:, None], ss[None, :])
    return (np.clip(ov, 0, None).astype(np.float32) / CMP_BLOCK).T


def _slc_steps(ns):
    return -(-ns // SLC_CHUNK)


def _active_blocks(cnt):
    ns = cnt.shape[-1]
    before_tile = jnp.arange(ns)[None, :] < (Q_BLOCK // SEL_BLOCK) * jnp.arange(cnt.shape[2])[:, None]
    flags = (cnt[:, :, :, 0, :] > 0.0) & before_tile
    order = jnp.argsort(jnp.where(flags, 0, 1).astype(jnp.int32), axis=-1, stable=True).astype(jnp.int32)
    n_act = jnp.sum(flags, axis=-1, dtype=jnp.int32)
    lead = order.shape[:-1]
    steps = _slc_steps(ns)
    order = jnp.pad(order, [(0, 0)] * len(lead) + [(0, steps * SLC_CHUNK - ns)]).reshape(lead + (steps, SLC_CHUNK))
    order = jnp.pad(order, [(0, 0)] * (len(lead) + 1) + [(0, 4 * SLC_WORDS - SLC_CHUNK)])
    packed = jnp.sum(order.reshape(lead + (steps * SLC_WORDS, 4)) << (8 * jnp.arange(4, dtype=jnp.int32)), axis=-1)
    return jnp.concatenate([n_act[..., None], packed], axis=-1).reshape(-1)


def kernel(x, mem, g_mix, g_ffn, g_mem, w_mem_kv, w_up, conv_w, conv_b, w_down,
           a_w_in, a_w_alpha, a_b_alpha, a_g_head, a_w_out,
           g_kv, w_kv, pe_k, pe_v, w_ck1, w_ck2, w_cv1, w_cv2,
           b_w_in, b_w_out, g_final):
    batch, t, d = x.shape
    n = batch * t
    m = mem.shape[1]
    tm = min(1024, t)
    xf = x.reshape(n, d)
    memf = mem.reshape(batch * m, d)
    row = lambda v: v.reshape(1, -1)

    mkv0 = _rms_proj(memf, row(g_mem[0]), w_mem_kv[0].astype(BF16), m, BF16).reshape(batch, m, 2 * MEM_WIDTH)
    wa = a_w_in[0]
    c_alr = 2 * GLA_QK + 2 * GLA_V
    w_a = jnp.concatenate([wa[:, :c_alr], wa[:, c_alr + GLA_RANK:], _pad_cols(wa[:, c_alr:c_alr + GLA_RANK], LANE)],
                          axis=1).astype(BF16)
    w_alpha = jnp.pad(a_w_alpha[0], ((0, LANE - GLA_RANK), (0, 0))).astype(BF16)
    q, k, gl, v, r, mq = _inproj_a(xf, row(g_mix[0]), w_a, w_alpha, row(a_b_alpha[0]), tm)
    o = _gla(q, k, gl, v, r, row(a_g_head[0]), batch, tm)
    xf = _attn_out(o, mq, mkv0, a_w_out[0].astype(BF16), xf, batch, tm)
    xf = _ffn(xf, row(g_ffn[0]), w_up[0].astype(BF16), conv_w[0], row(conv_b[0]), w_down[0].astype(BF16),
              row(g_final), batch, tm, final_norm=False)

    wb = b_w_in[0]
    wqt = wb[:, :NSA_WIDTH].T.astype(BF16)
    n_gate = 3 * NSA_HEADS
    wgt = jnp.pad(wb[:, NSA_WIDTH:NSA_WIDTH + n_gate].T, ((0, 32 - n_gate), (0, 0))).astype(BF16)
    wmq = wb[:, NSA_WIDTH + n_gate:].astype(BF16)
    ckv_in, kvs, qt, gt, mq1 = _proj_b(xf, row(g_kv), row(g_mix[1]), w_kv.astype(BF16), wqt, wgt, wmq, batch, tm)
    nsub = t // CMP_STRIDE
    w1k, pek, w2k = _compress_weights(w_ck1, pe_k, w_ck2)
    w1v, pev, w2v = _compress_weights(w_cv1, pe_v, w_cv2)
    ckv = _compress(ckv_in.reshape(2, batch, nsub, CMP_STRIDE * KV_WIDTH),
                    jnp.stack([w1k, w1v]), jnp.stack([pek, pev]), jnp.stack([w2k, w2v]))

    ns = t // SEL_BLOCK
    c_cmp, d_cmp, c_blk, c_win, srow = _alibi_tiles(nsub)
    ovt = jnp.asarray(_selection_overlap_t(nsub, ns), BF16)
    oc, sel, cnt = _nsa_a(qt, ckv, jnp.asarray(c_cmp), jnp.asarray(d_cmp), jnp.asarray(srow), ovt)
    idx = _active_blocks(cnt)
    o1 = _nsa_b(idx, qt, kvs, sel, oc, gt, jnp.asarray(c_blk), jnp.asarray(c_win), jnp.asarray(srow))
    mkv1 = _rms_proj(memf, row(g_mem[1]), w_mem_kv[1].astype(BF16), m, BF16).reshape(batch, m, 2 * MEM_WIDTH)
    xf = _attn_out(o1, mq1, mkv1, b_w_out[0].astype(BF16), xf, batch, tm)
    xf = _ffn(xf, row(g_ffn[1]), w_up[1].astype(BF16), conv_w[1], row(conv_b[1]), w_down[1].astype(BF16),
              row(g_final), batch, tm, final_norm=True)
    return xf.reshape(batch, t, d)
```

```python
import functools

import numpy as np
import jax
import jax.numpy as jnp
from jax import lax
from jax.experimental import pallas as pl
from jax.experimental.pallas import tpu as pltpu

F32 = jnp.float32
BF16 = jnp.bfloat16
EPS = 1e-6
NEG_INF = float("-inf")
LOG2E = 1.4426950408889634

V7X_VMEM_BYTES = 64 * 1024 * 1024
VMEM_LIMIT = V7X_VMEM_BYTES - 8 * 1024 * 1024

MEM_HEADS = 4
MEM_DH = 128
MEM_WIDTH = MEM_HEADS * MEM_DH
GLA_HEADS = 4
GLA_DK = 64
GLA_DV = 128
GLA_QK = GLA_HEADS * GLA_DK
GLA_V = GLA_HEADS * GLA_DV
GLA_RANK = 16
GLA_TAU = 16.0
GLA_CHUNK = 64
GLA_LEVELS = 6
NSA_HEADS = 8
NSA_GROUPS = 2
NSA_HPG = NSA_HEADS // NSA_GROUPS
NSA_DH = 64
NSA_WIDTH = NSA_HEADS * NSA_DH
KV_WIDTH = NSA_GROUPS * NSA_DH
CMP_BLOCK = 32
CMP_STRIDE = 16
CMP_HIDDEN = 256
SEL_BLOCK = 64
SEL_TOPN = 16
WINDOW = 512
Q_BLOCK = 128
WIN_KEYS = WINDOW + Q_BLOCK
WIN_CHUNK = WIN_KEYS
CMP_CHUNK = 128
SLC_CHUNK = 6
SLC_WORDS = -(-SLC_CHUNK // 4)
RANK_FIRST_TILE = SEL_TOPN * SEL_BLOCK // Q_BLOCK
SEL_TILES = 4
ATT_TILES = 4
FFN_DIM = 2816
FFN_CHUNK = 256
LANE = 128


def _dot(a, b):
    return jnp.dot(a, b, preferred_element_type=F32)


def _dot_nt(a, b):
    return lax.dot_general(a, b, (((1,), (1,)), ((), ())), preferred_element_type=F32)


def _dot_tn(a, b):
    return lax.dot_general(a, b, (((0,), (0,)), ((), ())), preferred_element_type=F32)


def _params(sem):
    return pltpu.CompilerParams(dimension_semantics=sem, vmem_limit_bytes=VMEM_LIMIT)


def _const_spec(shape):
    n = len(shape)
    return pl.BlockSpec(shape, lambda *_: (0,) * n)


def _normalize(x):
    return x * lax.rsqrt(jnp.mean(x * x, axis=-1, keepdims=True) + EPS)


def _sigmoid(x):
    return 1.0 / (1.0 + jnp.exp(-x))


def _rms_proj_kernel(x_ref, g_ref, w_ref, o_ref):
    h = (_normalize(x_ref[...]) * g_ref[...]).astype(BF16)
    o_ref[...] = _dot(h, w_ref[...]).astype(o_ref.dtype)


def _rms_proj(x, g, w, tm, out_dtype):
    n, d = x.shape
    p = w.shape[1]
    return pl.pallas_call(
        _rms_proj_kernel,
        out_shape=jax.ShapeDtypeStruct((n, p), out_dtype),
        grid=(n // tm,),
        in_specs=[pl.BlockSpec((tm, d), lambda i: (i, 0)), _const_spec((1, d)), _const_spec((d, p))],
        out_specs=pl.BlockSpec((tm, p), lambda i: (i, 0)),
        compiler_params=_params(("parallel",)),
        name="rms_proj",
    )(x, g, w)


def _inproj_a_kernel(x_ref, g_ref, w_ref, wa_ref, ba_ref, q_ref, k_ref, gl_ref, v_ref, r_ref, mq_ref):
    h = (_normalize(x_ref[...]) * g_ref[...]).astype(BF16)
    c = 0
    q_ref[...] = _dot(h, w_ref[:, c:c + GLA_QK]) * (GLA_DK ** -0.5)
    c += GLA_QK
    k_ref[...] = _dot(h, w_ref[:, c:c + GLA_QK])
    c += GLA_QK
    v_ref[...] = _dot(h, w_ref[:, c:c + GLA_V]).astype(BF16)
    c += GLA_V
    r_ref[...] = _dot(h, w_ref[:, c:c + GLA_V])
    c += GLA_V
    mq_ref[...] = _dot(h, w_ref[:, c:c + MEM_WIDTH]).astype(BF16)
    c += MEM_WIDTH
    alr = _dot(h, w_ref[:, c:c + LANE]).astype(BF16)
    z = _dot(alr, wa_ref[...]) + ba_ref[...]
    log_sig = jnp.minimum(z, 0.0) - jnp.log1p(jnp.exp(-jnp.abs(z)))
    gl_ref[...] = log_sig * (1.0 / GLA_TAU)


def _inproj_a(x, g, w, wa, ba, tm):
    n, d = x.shape
    row = lambda i: (i, 0)
    outs = [
        jax.ShapeDtypeStruct((n, GLA_QK), F32), jax.ShapeDtypeStruct((n, GLA_QK), F32),
        jax.ShapeDtypeStruct((n, GLA_QK), F32), jax.ShapeDtypeStruct((n, GLA_V), BF16),
        jax.ShapeDtypeStruct((n, GLA_V), F32), jax.ShapeDtypeStruct((n, MEM_WIDTH), BF16),
    ]
    return pl.pallas_call(
        _inproj_a_kernel,
        out_shape=outs,
        grid=(n // tm,),
        in_specs=[pl.BlockSpec((tm, d), row), _const_spec((1, d)), _const_spec(w.shape),
                  _const_spec(wa.shape), _const_spec(ba.shape)],
        out_specs=[pl.BlockSpec((tm, s.shape[1]), row) for s in outs],
        compiler_params=_params(("parallel",)),
        name="inproj_a",
    )(x, g, w, wa, ba)


def _gla_constants():
    c = GLA_CHUNK
    w = np.zeros(((GLA_LEVELS + 2) * c, c), np.float32)
    masks = np.zeros((GLA_LEVELS + 1, c, c), np.float32)
    masks[0] = np.eye(c)
    for l in range(1, GLA_LEVELS + 1):
        blk, half = 2 ** l, 2 ** (l - 1)
        for i in range(c):
            mid = (i // blk) * blk + half - 1
            if i % blk >= half:
                w[(l - 1) * c + i, mid + 1:i + 1] = 1.0
            else:
                w[(l - 1) * c + i, i + 1:mid + 1] = 1.0
        for t in range(c):
            for s in range(c):
                if t // blk == s // blk and t % blk >= half and s % blk < half:
                    masks[l, t, s] = 1.0
    for i in range(c):
        w[GLA_LEVELS * c + i, :i + 1] = 1.0
        w[(GLA_LEVELS + 1) * c + i, i + 1:] = 1.0
    return np.concatenate([w, w], axis=1), np.tile(masks, (1, 1, GLA_HEADS))


def _gla_kernel(q_ref, k_ref, gl_ref, v_ref, r_ref, gh_ref, wcat_ref, msk_ref, o_ref, st_ref, *, n_chunks):
    c = GLA_CHUNK

    @pl.when(pl.program_id(1) == 0)
    def _():
        st_ref[...] = jnp.zeros_like(st_ref)

    lane_qk = lax.broadcasted_iota(jnp.int32, (1, GLA_QK), 1) // GLA_DK
    lane_v = lax.broadcasted_iota(jnp.int32, (1, GLA_V), 1) // GLA_DV
    row_qk = lax.broadcasted_iota(jnp.int32, (GLA_QK, 1), 0) // GLA_DK
    ones = jnp.ones((2 * c, LANE), BF16)
    wcat = wcat_ref[...]

    def stack_heads(x, lane_head):
        return jnp.concatenate([jnp.where(lane_head == h, x, jnp.zeros_like(x)) for h in range(GLA_HEADS)], axis=0)

    chunks = [slice(ci * c, (ci + 1) * c) for ci in range(n_chunks)]
    es, decs = [], []
    for rows in chunks:
        g = gl_ref[rows, :] * LOG2E
        g_hi = g.astype(BF16)
        g_split = jnp.concatenate([g_hi, (g - g_hi.astype(F32)).astype(BF16)], axis=0)
        es.append(jnp.exp2(_dot(wcat, g_split)))
        decs.append(jnp.exp2(_dot_tn(g_split, ones)))

    o_intras, q_ins, upds = [], [], []
    for rows, e in zip(chunks, es):
        q = q_ref[rows, :]
        k = k_ref[rows, :]
        v = v_ref[rows, :]
        attn = jnp.zeros((c, GLA_HEADS * c), F32)
        for l in range(GLA_LEVELS + 1):
            if l == 0:
                ql, kl = q, k
            else:
                el = e[(l - 1) * c:l * c]
                ql, kl = q * el, k * el
            attn = attn + _dot_nt(ql.astype(BF16), stack_heads(kl, lane_qk).astype(BF16)) * msk_ref[l]
        o_intras.append(_dot(attn.astype(BF16), stack_heads(v, lane_v)))
        q_ins.append((q * e[GLA_LEVELS * c:(GLA_LEVELS + 1) * c]).astype(BF16))
        kk = stack_heads(k * e[(GLA_LEVELS + 1) * c:(GLA_LEVELS + 2) * c], lane_qk).astype(BF16)
        v_rows = jnp.concatenate([v[:, h * GLA_DV:(h + 1) * GLA_DV] for h in range(GLA_HEADS)], axis=0)
        upds.append(_dot_tn(kk, v_rows))

    st = st_ref[...]
    outs = []
    for o_intra, q_in, dec, upd in zip(o_intras, q_ins, decs, upds):
        st_bd = jnp.concatenate([jnp.where(row_qk == h, st, 0.0).astype(BF16) for h in range(GLA_HEADS)], axis=1)
        outs.append(o_intra + _dot(q_in, st_bd))
        st = dec * st + upd
    st_ref[...] = st

    for rows, o in zip(chunks, outs):
        for h in range(GLA_HEADS):
            cs = slice(h * GLA_DV, (h + 1) * GLA_DV)
            on = _normalize(o[:, cs]) * gh_ref[...]
            r = r_ref[rows, cs]
            o_ref[rows, cs] = (on * (r * _sigmoid(r))).astype(BF16)


def _gla(q, k, gl, v, r, g_head, batch, ct):
    n = q.shape[0]
    t = n // batch
    nt = t // ct
    wcat, masks = _gla_constants()
    row = lambda b, i: (b * nt + i, 0)
    return pl.pallas_call(
        functools.partial(_gla_kernel, n_chunks=ct // GLA_CHUNK),
        out_shape=jax.ShapeDtypeStruct((n, GLA_V), BF16),
        grid=(batch, nt),
        in_specs=[pl.BlockSpec((ct, GLA_QK), row), pl.BlockSpec((ct, GLA_QK), row),
                  pl.BlockSpec((ct, GLA_QK), row), pl.BlockSpec((ct, GLA_V), row),
                  pl.BlockSpec((ct, GLA_V), row), _const_spec((1, GLA_DV)),
                  _const_spec(wcat.shape), _const_spec(masks.shape)],
        out_specs=pl.BlockSpec((ct, GLA_V), row),
        scratch_shapes=[pltpu.VMEM((GLA_QK, GLA_DV), F32)],
        compiler_params=_params(("parallel", "arbitrary")),
        name="gla",
    )(q, k, gl, v, r, g_head, jnp.asarray(wcat, BF16), jnp.asarray(masks, F32))


def _attn_out_kernel(o_ref, mq_ref, mk_ref, mv_ref, w_ref, x_ref, out_ref):
    heads = [slice(h * MEM_DH, (h + 1) * MEM_DH) for h in range(MEM_HEADS)]
    scores = [_dot_nt(mq_ref[:, cs], mk_ref[0, :, cs]) * (MEM_DH ** -0.5 * LOG2E) for cs in heads]
    parts = [o_ref[...]]
    for s, cs in zip(scores, heads):
        e = jnp.exp2(s - jnp.max(s, axis=-1, keepdims=True))
        p = e / jnp.sum(e, axis=-1, keepdims=True)
        parts.append(_dot(p.astype(BF16), mv_ref[0, :, cs]).astype(BF16))
    cat = jnp.concatenate(parts, axis=-1)
    out_ref[...] = x_ref[...] + _dot(cat, w_ref[...])


def _attn_out(o, mq, mkv, w_out, x, batch, tm):
    n, d = x.shape
    t = n // batch
    nt = t // tm
    m = mkv.shape[1]
    row = lambda b, i: (b * nt + i, 0)
    return pl.pallas_call(
        _attn_out_kernel,
        out_shape=jax.ShapeDtypeStruct((n, d), F32),
        grid=(batch, nt),
        in_specs=[pl.BlockSpec((tm, o.shape[1]), row), pl.BlockSpec((tm, MEM_WIDTH), row),
                  pl.BlockSpec((1, m, MEM_WIDTH), lambda b, i: (b, 0, 0)),
                  pl.BlockSpec((1, m, MEM_WIDTH), lambda b, i: (b, 0, 1)),
                  _const_spec(w_out.shape), pl.BlockSpec((tm, d), row)],
        out_specs=pl.BlockSpec((tm, d), row),
        compiler_params=_params(("parallel", "parallel")),
        name="attn_out",
    )(o, mq, mkv, mkv, w_out, x)


def _ffn_kernel(x_ref, g_ref, wup_ref, cw_ref, cb_ref, wd_ref, gf_ref, out_ref, act_ref, tail_ref, *,
                final_norm):
    @pl.when(pl.program_id(1) == 0)
    def _():
        tail_ref[...] = jnp.zeros_like(tail_ref)

    x = x_ref[...]
    tm = x.shape[0]
    h = (_normalize(x) * g_ref[...]).astype(BF16)
    rid = lax.broadcasted_iota(jnp.int32, (8, FFN_CHUNK), 0)
    for j in range(FFN_DIM // FFN_CHUNK):
        cs = slice(j * FFN_CHUNK, (j + 1) * FFN_CHUNK)
        a = _dot(h, wup_ref[:, cs])
        b = _dot(h, wup_ref[:, FFN_DIM + j * FFN_CHUNK:FFN_DIM + (j + 1) * FFN_CHUNK])
        tail = tail_ref[j]
        r1 = pltpu.roll(a, 1, 0)
        r2 = pltpu.roll(a, 2, 0)
        top1 = jnp.where(rid == 0, tail[7:8], r1[:8])
        top2 = jnp.where(rid == 0, tail[6:7], jnp.where(rid == 1, tail[7:8], r2[:8]))
        a1 = jnp.concatenate([top1, r1[8:]], axis=0)
        a2 = jnp.concatenate([top2, r2[8:]], axis=0)
        tail_ref[j] = a[tm - 8:]
        ac = a2 * cw_ref[0:1, cs] + a1 * cw_ref[1:2, cs] + a * cw_ref[2:3, cs] + cb_ref[:, cs]
        act_ref[:, cs] = (ac * _sigmoid(ac) * b).astype(BF16)
    y = x + _dot(act_ref[...], wd_ref[...])
    if final_norm:
        y = _normalize(y) * gf_ref[...]
    out_ref[...] = y


def _ffn(x, g, w_up, conv_w, conv_b, w_down, g_final, batch, tm, final_norm):
    n, d = x.shape
    nt = n // batch // tm
    row = lambda b, i: (b * nt + i, 0)
    once = pl.Buffered(1)
    return pl.pallas_call(
        functools.partial(_ffn_kernel, final_norm=final_norm),
        out_shape=jax.ShapeDtypeStruct((n, d), F32),
        grid=(batch, nt),
        in_specs=[pl.BlockSpec((tm, d), row), _const_spec((1, d)),
                  pl.BlockSpec(w_up.shape, lambda b, i: (0, 0), pipeline_mode=once),
                  _const_spec(conv_w.shape), _const_spec(conv_b.shape),
                  pl.BlockSpec(w_down.shape, lambda b, i: (0, 0), pipeline_mode=once),
                  _const_spec((1, d))],
        out_specs=pl.BlockSpec((tm, d), row),
        scratch_shapes=[pltpu.VMEM((tm, FFN_DIM), BF16),
                        pltpu.VMEM((FFN_DIM // FFN_CHUNK, 8, FFN_CHUNK), F32)],
        compiler_params=_params(("parallel", "arbitrary")),
        name="conv_ffn",
    )(x, g, w_up, conv_w, conv_b, w_down, g_final)


def _proj_b_kernel(x_ref, gkv_ref, gmix_ref, wkv_ref, wqt_ref, wgt_ref, wmq_ref,
                   cmp_ref, kvs_ref, qt_ref, gt_ref, mq_ref):
    xn = _normalize(x_ref[...])
    hkv = (xn * gkv_ref[...]).astype(BF16)
    h1 = (xn * gmix_ref[...]).astype(BF16)
    kv_cmp = _dot(hkv, wkv_ref[:, 0:2 * KV_WIDTH]).astype(BF16)
    cmp_ref[0] = kv_cmp[:, 0:KV_WIDTH]
    cmp_ref[1] = kv_cmp[:, KV_WIDTH:2 * KV_WIDTH]
    kvs_ref[...] = _dot(hkv, wkv_ref[:, 2 * KV_WIDTH:]).astype(BF16)
    qt_ref[0] = (_dot_nt(wqt_ref[...], h1) * (NSA_DH ** -0.5 * LOG2E)).astype(BF16)
    gt_ref[0] = _sigmoid(_dot_nt(wgt_ref[...], h1))
    mq_ref[...] = _dot(h1, wmq_ref[...]).astype(BF16)


def _proj_b(x, g_kv, g_mix, w_kv, wqt, wgt, wmq, batch, tm):
    n, d = x.shape
    t = n // batch
    nt = t // tm
    row = lambda b, i: (b * nt + i, 0)
    outs = [
        jax.ShapeDtypeStruct((2, n, KV_WIDTH), BF16),
        jax.ShapeDtypeStruct((n, 4 * KV_WIDTH), BF16),
        jax.ShapeDtypeStruct((batch, NSA_WIDTH, t), BF16),
        jax.ShapeDtypeStruct((batch, wgt.shape[0], t), F32),
        jax.ShapeDtypeStruct((n, MEM_WIDTH), BF16),
    ]
    return pl.pallas_call(
        _proj_b_kernel,
        out_shape=outs,
        grid=(batch, nt),
        in_specs=[pl.BlockSpec((tm, d), row), _const_spec((1, d)), _const_spec((1, d)),
                  _const_spec(w_kv.shape), _const_spec(wqt.shape), _const_spec(wgt.shape),
                  _const_spec(wmq.shape)],
        out_specs=[pl.BlockSpec((2, tm, KV_WIDTH), lambda b, i: (0, b * nt + i, 0)),
                   pl.BlockSpec((tm, 4 * KV_WIDTH), row),
                   pl.BlockSpec((1, NSA_WIDTH, tm), lambda b, i: (b, 0, i)),
                   pl.BlockSpec((1, wgt.shape[0], tm), lambda b, i: (b, 0, i)),
                   pl.BlockSpec((tm, MEM_WIDTH), row)],
        compiler_params=_params(("parallel", "parallel")),
        name="proj_b",
    )(x, g_kv, g_mix, w_kv, wqt, wgt, wmq)


def _compress_kernel(a_ref, w1_ref, pe_ref, w2_ref, o_ref):
    a = a_ref[0, 0]
    w1 = w1_ref[0]
    u = _dot(a, w1)
    cpe = _dot(pe_ref[0], w1)
    nrow = a.shape[0]
    hid = []
    for g in range(NSA_GROUPS):
        c0 = slice((2 * g) * CMP_HIDDEN, (2 * g + 1) * CMP_HIDDEN)
        c1 = slice((2 * g + 1) * CMP_HIDDEN, (2 * g + 2) * CMP_HIDDEN)
        nxt = pltpu.roll(u[:, c1], nrow - 1, 0)
        hid.append(u[:, c0] + nxt + cpe[0:1, c0] + cpe[8:9, c1])
    hcat = jax.nn.gelu(jnp.concatenate(hid, axis=-1), approximate=True).astype(BF16)
    o_ref[0, 0] = _dot(hcat, w2_ref[0]).astype(BF16)


def _compress(a, w1, pe, w2):
    _, batch, nsub, width = a.shape
    return pl.pallas_call(
        _compress_kernel,
        out_shape=jax.ShapeDtypeStruct((2, batch, nsub, KV_WIDTH), BF16),
        grid=(2, batch),
        in_specs=[pl.BlockSpec((1, 1, nsub, width), lambda s, b: (s, b, 0, 0)),
                  pl.BlockSpec((1,) + w1.shape[1:], lambda s, b: (s, 0, 0)),
                  pl.BlockSpec((1,) + pe.shape[1:], lambda s, b: (s, 0, 0)),
                  pl.BlockSpec((1,) + w2.shape[1:], lambda s, b: (s, 0, 0))],
        out_specs=pl.BlockSpec((1, 1, nsub, KV_WIDTH), lambda s, b: (s, b, 0, 0)),
        compiler_params=_params(("parallel", "parallel")),
        name="compress",
    )(a, w1, pe, w2)


def _alibi_tiles(ncmp_pad):
    slopes = LOG2E * 2.0 ** (-8.0 * np.arange(1, NSA_HEADS + 1) / NSA_HEADS)
    tl = np.tile(np.arange(Q_BLOCK), NSA_HPG)[None, :]
    cmp_end = (np.arange(ncmp_pad) * CMP_STRIDE + CMP_BLOCK - 1)[:, None]
    key = np.arange(SEL_BLOCK)[:, None]
    dist_win = tl + WINDOW - np.arange(WIN_KEYS)[:, None]
    c_cmp = np.zeros((NSA_GROUPS, ncmp_pad, NSA_HPG * Q_BLOCK), np.float32)
    c_blk = np.zeros((NSA_GROUPS, SEL_BLOCK, NSA_HPG * Q_BLOCK), np.float32)
    c_win = np.zeros((NSA_GROUPS, WIN_KEYS, NSA_HPG * Q_BLOCK), np.float32)
    srow = np.zeros((NSA_GROUPS, 1, NSA_HPG * Q_BLOCK), np.float32)
    for g in range(NSA_GROUPS):
        s = np.repeat(slopes[g * NSA_HPG:(g + 1) * NSA_HPG], Q_BLOCK)[None, :]
        c_cmp[g] = -s * (tl - cmp_end)
        c_blk[g] = -s * (tl - key)
        c_win[g] = np.where((dist_win >= 0) & (dist_win < WINDOW), -s * dist_win, -np.inf)
        srow[g] = s
    d_cmp = (cmp_end - tl).astype(np.float32)
    return c_cmp, d_cmp, c_blk, c_win, srow


def _group_queries(qt_ref, g, tile=0):
    zeros = jnp.zeros((NSA_DH, Q_BLOCK), BF16)
    cols = []
    for hh in range(NSA_HPG):
        h = g * NSA_HPG + hh
        piece = qt_ref[0, h * NSA_DH:(h + 1) * NSA_DH, tile * Q_BLOCK:(tile + 1) * Q_BLOCK]
        cols.append(jnp.concatenate([piece, zeros] if g == 0 else [zeros, piece], axis=0))
    return jnp.concatenate(cols, axis=1)


def _nsa_a_kernel(qt_ref, ckv_ref, ccmp_ref, dcmp_ref, srow_ref, ovt_ref, oc_ref, sel_ref, cnt_ref, imp_ref):
    first_tile = pl.program_id(1) * SEL_TILES
    nblk = sel_ref.shape[3]
    jidx = lax.broadcasted_iota(jnp.int32, (nblk, Q_BLOCK), 0)
    tl = lax.broadcasted_iota(jnp.int32, (nblk, Q_BLOCK), 1)
    units = [(k, g) for k in range(SEL_TILES) for g in range(NSA_GROUPS)]
    qbs = [first_tile + k for k in range(SEL_TILES)]
    bases = [(qb * Q_BLOCK).astype(F32) for qb in qbs]
    curs = [2 * qb + jnp.where(tl >= SEL_BLOCK, 1, 0) for qb in qbs]
    n_chunks = (qbs[-1] * (Q_BLOCK // CMP_STRIDE) + (Q_BLOCK - CMP_BLOCK) // CMP_STRIDE) // CMP_CHUNK + 1
    qts = {(k, g): _group_queries(qt_ref, g, k) for k, g in units}
    shifts = {(k, g): srow_ref[g] * bases[k] for k, g in units}

    def cmp_branch(n_rows):
        kc = ckv_ref[0, 0, 0:n_rows, :]
        vc = ckv_ref[1, 0, 0:n_rows, :]
        ov = ovt_ref[:, 0:n_rows]
        lane_grp = lax.broadcasted_iota(jnp.int32, vc.shape, 1) // NSA_DH
        raws = {u: _dot(kc, qts[u]) for u in units}
        for k, g in units:
            visible = dcmp_ref[0:n_rows, :] <= bases[k]
            s = raws[k, g] + jnp.where(visible, ccmp_ref[g, 0:n_rows, :] - shifts[k, g], NEG_INF)
            m = jnp.max(s, axis=0, keepdims=True)
            e = jnp.exp2((s - jnp.where(m == NEG_INF, 0.0, m)).astype(BF16))
            pv = _dot_tn(jnp.where(lane_grp == g, vc, jnp.ones_like(vc)), e)
            inv = 1.0 / jnp.maximum(pv[(1 - g) * NSA_DH:(1 - g) * NSA_DH + 1], 1e-30)
            oc = pv[g * NSA_DH:(g + 1) * NSA_DH] * inv
            imp = _dot(ov, e) * inv
            for hh in range(NSA_HPG):
                h = g * NSA_HPG + hh
                oc_ref[0, h * NSA_DH:(h + 1) * NSA_DH, k * Q_BLOCK:(k + 1) * Q_BLOCK] = \
                    oc[:, hh * Q_BLOCK:(hh + 1) * Q_BLOCK]
            imp_g = imp[:, 0:Q_BLOCK]
            for hh in range(1, NSA_HPG):
                imp_g = imp_g + imp[:, hh * Q_BLOCK:(hh + 1) * Q_BLOCK]
            imp_ref[k, g] = imp_g

    for nc in range(1, ckv_ref.shape[2] // CMP_CHUNK + 1):
        pl.when(n_chunks == nc)(functools.partial(cmp_branch, nc * CMP_CHUNK))

    def emit(k, g, sel):
        sel_ref[0, g, k] = sel
        cnt_ref[0, g, k] = _dot_nt(jnp.ones((8, Q_BLOCK), BF16), sel.astype(BF16))

    @pl.when(first_tile < RANK_FIRST_TILE)
    def _():
        for k, g in units:
            emit(k, g, jnp.where(jidx <= curs[k], 1.0, 0.0))

    def rank_branch(rows):
        jr = lax.broadcasted_iota(jnp.int32, (rows, Q_BLOCK), 0)
        jf = jr.astype(F32)
        cur_r = [2 * qb + jnp.where(lax.broadcasted_iota(jnp.int32, (rows, Q_BLOCK), 1) >= SEL_BLOCK, 1, 0)
                 for qb in qbs]
        v0 = {(k, g): jnp.where((jr >= 1) & (jr <= cur_r[k] - 2), imp_ref[k, g, 0:rows, :], NEG_INF)
              for k, g in units}
        v = dict(v0)
        for _ in range(SEL_TOPN - 3):
            for u in units:
                mx = jnp.max(v[u], axis=0, keepdims=True)
                first = jnp.min(jnp.where(v[u] == mx, jf, float(nblk)), axis=0, keepdims=True)
                v[u] = jnp.where(jf == first, NEG_INF, v[u])
        for k, g in units:
            cur = cur_r[k]
            taken = (v[k, g] != v0[k, g]) | (jr == 0) | (jr == cur) | (jr == cur - 1)
            sel = jnp.where(taken, 1.0, 0.0)
            if rows < nblk:
                sel = jnp.concatenate([sel, jnp.zeros((nblk - rows, Q_BLOCK), F32)], axis=0)
            emit(k, g, sel)

    @pl.when(first_tile >= RANK_FIRST_TILE)
    def _():
        blocks_per_variant = CMP_CHUNK * CMP_STRIDE // SEL_BLOCK
        for nc in range(1, ckv_ref.shape[2] // CMP_CHUNK + 1):
            pl.when(n_chunks == nc)(functools.partial(rank_branch, min(nblk, nc * blocks_per_variant)))


def _nsa_a(qt, ckv, c_cmp, d_cmp, srow, ovt):
    batch, _, t = qt.shape
    nqb = t // Q_BLOCK
    ns = t // SEL_BLOCK
    ncmp = ckv.shape[2]
    outs = [jax.ShapeDtypeStruct((batch, NSA_WIDTH, t), F32),
            jax.ShapeDtypeStruct((batch, NSA_GROUPS, nqb, ns, Q_BLOCK), F32),
            jax.ShapeDtypeStruct((batch, NSA_GROUPS, nqb, 8, ns), F32)]
    assert nqb % SEL_TILES == 0 and RANK_FIRST_TILE % SEL_TILES == 0
    return pl.pallas_call(
        _nsa_a_kernel,
        out_shape=outs,
        grid=(batch, nqb // SEL_TILES),
        in_specs=[pl.BlockSpec((1, NSA_WIDTH, SEL_TILES * Q_BLOCK), lambda b, i: (b, 0, i)),
                  pl.BlockSpec((2, 1, ncmp, KV_WIDTH), lambda b, i: (0, b, 0, 0)),
                  _const_spec(c_cmp.shape), _const_spec(d_cmp.shape), _const_spec(srow.shape),
                  _const_spec(ovt.shape)],
        out_specs=[pl.BlockSpec((1, NSA_WIDTH, SEL_TILES * Q_BLOCK), lambda b, i: (b, 0, i)),
                   pl.BlockSpec((1, NSA_GROUPS, SEL_TILES, ns, Q_BLOCK), lambda b, i: (b, 0, i, 0, 0)),
                   pl.BlockSpec((1, NSA_GROUPS, SEL_TILES, 8, ns), lambda b, i: (b, 0, i, 0, 0))],
        scratch_shapes=[pltpu.VMEM((SEL_TILES, NSA_GROUPS, ns, Q_BLOCK), F32)],
        compiler_params=_params(("parallel", "parallel")),
        name="nsa_select",
    )(qt, ckv, c_cmp, d_cmp, srow, ovt)


def _block_softmax_step(carry, raw, consts, shifts, v_rows, g):
    m, l, acc = carry
    width = raw.shape[1]
    fold = lambda x: x.reshape(SEL_BLOCK // 8, 8, width)
    us = [raw[i * SEL_BLOCK:(i + 1) * SEL_BLOCK] + c for i, c in enumerate(consts)]
    top8 = None
    for u, sh in zip(us, shifts):
        t = jnp.max(fold(u), axis=0) + sh
        top8 = t if top8 is None else jnp.maximum(top8, t)
    m_new = jnp.maximum(m, jnp.max(top8, axis=0, keepdims=True))
    m_safe = jnp.where(m_new == NEG_INF, 0.0, m_new)
    alpha = jnp.where(m == NEG_INF, 0.0, jnp.exp2(m - m_safe))
    ps = [jnp.exp2((u - (m_safe - sh)).astype(BF16)) for u, sh in zip(us, shifts)]
    lane_grp = lax.broadcasted_iota(jnp.int32, v_rows.shape, 1) // NSA_DH
    v_aug = jnp.where(lane_grp == g, v_rows, jnp.ones_like(v_rows))
    pv = _dot_tn(v_aug, jnp.concatenate(ps, axis=0))
    og = 1 - g
    return m_new, alpha * l + pv[og * NSA_DH:og * NSA_DH + 1], alpha * acc + pv[g * NSA_DH:(g + 1) * NSA_DH]


def _nsa_b_kernel(idx_ref, qt_ref, ks_ref, vs_ref, kw_ref, vw_ref, sel_ref, oc_ref, gt_ref, cblk_ref, cwin_ref,
                  srow_ref, o_ref, *, idx_words):
    b = pl.program_id(0)
    first_tile = pl.program_id(1) * ATT_TILES
    width = NSA_HPG * Q_BLOCK
    units = [(k, g) for k in range(ATT_TILES) for g in range(NSA_GROUPS)]
    qbs = [first_tile + k for k in range(ATT_TILES)]
    bases = [qb * Q_BLOCK for qb in qbs]
    qts = {(k, g): _group_queries(qt_ref, g, k) for k, g in units}
    srows = [srow_ref[g] for g in range(NSA_GROUPS)]
    n_tiles = pl.num_programs(1) * ATT_TILES
    entries = {(k, g): ((b * NSA_GROUPS + g) * n_tiles + qbs[k]) * idx_words for k, g in units}
    n_act = {u: idx_ref[entries[u]] for u in units}
    init = (jnp.full((1, width), NEG_INF, F32), jnp.zeros((1, width), F32), jnp.zeros((NSA_DH, width), F32))
    zero_row = jnp.zeros((1, width), F32)

    def gate_row(k, g, j, live):
        picked = sel_ref[0, g, k, pl.ds(j, 1), :]
        gate = jnp.where((picked > 0.0) & live, 0.0, NEG_INF)
        return jnp.concatenate([gate] * NSA_HPG, axis=1)

    def stage(it):
        out = {}
        for k, g in units:
            words = [idx_ref[entries[k, g] + 1 + it * SLC_WORDS + w] for w in range(SLC_WORDS)]
            kts, vts, shifts = [], [], []
            for i in range(SLC_CHUNK):
                live = it * SLC_CHUNK + i < n_act[k, g]
                j = jnp.where(live, lax.shift_right_logical(words[i // 4], 8 * (i % 4)) & 0xFF, 0)
                k0 = pl.multiple_of(j * SEL_BLOCK, SEL_BLOCK)
                kts.append(ks_ref[pl.ds(k0, SEL_BLOCK), :])
                vts.append(vs_ref[pl.ds(k0, SEL_BLOCK), :])
                shifts.append(gate_row(k, g, j, live) - srows[g] * (bases[k] - k0).astype(F32))
            out[k, g] = (_dot(jnp.concatenate(kts, axis=0), qts[k, g]), shifts, jnp.concatenate(vts, axis=0))
        return out

    def body(it, states):
        staged = stage(it)
        return tuple(_block_softmax_step(state, staged[u][0], [cblk_ref[u[1]]] * SLC_CHUNK, staged[u][1],
                                         staged[u][2], u[1]) for u, state in zip(units, states))

    longest = functools.reduce(jnp.maximum, [n_act[u] for u in units])
    far = lax.fori_loop(0, (longest + SLC_CHUNK - 1) // SLC_CHUNK, body, (init,) * len(units))
    slc = dict(zip(units, far))

    diag = range(WINDOW // SEL_BLOCK, WIN_KEYS // SEL_BLOCK)
    win = {u: init for u in units}
    steps = []
    for k, g in units:
        r0 = pl.multiple_of(bases[k], Q_BLOCK)
        shifts = [gate_row(k, g, 2 * qbs[k] + i, True) for i in range(len(diag))]
        steps.append((slc, (k, g), ks_ref[pl.ds(r0, Q_BLOCK), :], vs_ref[pl.ds(r0, Q_BLOCK), :], diag, shifts))
    for k in range(ATT_TILES):
        for c in range(WIN_KEYS // WIN_CHUNK):
            blocks = range(c * (WIN_CHUNK // SEL_BLOCK), (c + 1) * (WIN_CHUNK // SEL_BLOCK))
            kws, vws, offs = [], [], []
            for r in blocks:
                j = 2 * qbs[k] - WINDOW // SEL_BLOCK + r
                k0 = pl.multiple_of(jnp.maximum(j, 0) * SEL_BLOCK, SEL_BLOCK)
                kws.append(kw_ref[pl.ds(k0, SEL_BLOCK), :])
                vws.append(vw_ref[pl.ds(k0, SEL_BLOCK), :])
                offs.append(zero_row + jnp.where(j >= 0, 0.0, NEG_INF))
            kw = jnp.concatenate(kws, axis=0)
            vw = jnp.concatenate(vws, axis=0)
            for g in range(NSA_GROUPS):
                steps.append((win, (k, g), kw, vw, blocks, offs))
    raws = [_dot(keys, qts[u]) for _, u, keys, _, _, _ in steps]
    for raw, (states, u, _, values, blocks, shifts) in zip(raws, steps):
        consts = [cwin_ref[u[1], r * SEL_BLOCK:(r + 1) * SEL_BLOCK, :] for r in blocks]
        states[u] = _block_softmax_step(states[u], raw, consts, shifts, values, u[1])

    for k in range(ATT_TILES):
        cols = slice(k * Q_BLOCK, (k + 1) * Q_BLOCK)
        heads_out = []
        for g in range(NSA_GROUPS):
            o_s = slc[k, g][2] * (1.0 / jnp.maximum(slc[k, g][1], 1e-30))
            o_w = win[k, g][2] * (1.0 / jnp.maximum(win[k, g][1], 1e-30))
            for hh in range(NSA_HPG):
                h = g * NSA_HPG + hh
                cs = slice(hh * Q_BLOCK, (hh + 1) * Q_BLOCK)
                gates = gt_ref[0, 3 * h:3 * h + 3, cols]
                heads_out.append(gates[0:1] * oc_ref[0, h * NSA_DH:(h + 1) * NSA_DH, cols]
                                 + gates[1:2] * o_s[:, cs] + gates[2:3] * o_w[:, cs])
        o_ref[cols, :] = jnp.concatenate(heads_out, axis=0).astype(BF16).T


def _nsa_b(idx, qt, kvs, sel, oc, gt, c_blk, c_win, srow):
    batch, _, t = qt.shape
    nqb = t // Q_BLOCK
    ns = t // SEL_BLOCK
    kv_spec = lambda c: pl.BlockSpec((t, KV_WIDTH), lambda b, i, idx_ref: (b, c))
    grid_spec = pltpu.PrefetchScalarGridSpec(
        num_scalar_prefetch=1,
        grid=(batch, nqb // ATT_TILES),
        in_specs=[pl.BlockSpec((1, NSA_WIDTH, ATT_TILES * Q_BLOCK), lambda b, i, r: (b, 0, i)),
                  kv_spec(0), kv_spec(1), kv_spec(2), kv_spec(3),
                  pl.BlockSpec((1, NSA_GROUPS, ATT_TILES, ns, Q_BLOCK), lambda b, i, r: (b, 0, i, 0, 0)),
                  pl.BlockSpec((1, NSA_WIDTH, ATT_TILES * Q_BLOCK), lambda b, i, r: (b, 0, i)),
                  pl.BlockSpec((1, gt.shape[1], ATT_TILES * Q_BLOCK), lambda b, i, r: (b, 0, i)),
                  pl.BlockSpec(c_blk.shape, lambda b, i, r: (0, 0, 0)),
                  pl.BlockSpec(c_win.shape, lambda b, i, r: (0, 0, 0)),
                  pl.BlockSpec(srow.shape, lambda b, i, r: (0, 0, 0))],
        out_specs=pl.BlockSpec((ATT_TILES * Q_BLOCK, NSA_WIDTH), lambda b, i, r: (b * (nqb // ATT_TILES) + i, 0)),
    )
    assert nqb % ATT_TILES == 0
    return pl.pallas_call(
        functools.partial(_nsa_b_kernel, idx_words=1 + _slc_steps(ns) * SLC_WORDS),
        out_shape=jax.ShapeDtypeStruct((batch * t, NSA_WIDTH), BF16),
        grid_spec=grid_spec,
        compiler_params=_params(("parallel", "parallel")),
        name="nsa_attend",
    )(idx, qt, kvs, kvs, kvs, kvs, sel, oc, gt, c_blk, c_win, srow)


def _pad_cols(w, width):
    return jnp.pad(w, ((0, 0), (0, width - w.shape[1])))


def _compress_weights(w1, pe, w2):
    nsub = CMP_STRIDE
    w1r = w1.reshape(2, nsub, NSA_DH, CMP_HIDDEN)
    same_group = jnp.eye(NSA_GROUPS, dtype=F32)
    big = same_group[None, :, None, :, None, None] * w1r.transpose(1, 2, 0, 3)[:, None, :, None, :, :]
    big = big.reshape(nsub * KV_WIDTH, NSA_GROUPS * 2 * CMP_HIDDEN)
    per = pe.reshape(2, nsub, 1, NSA_DH)
    pe2 = jnp.broadcast_to(per, (2, nsub, NSA_GROUPS, NSA_DH)).reshape(2, 1, nsub * KV_WIDTH)
    pe2 = jnp.broadcast_to(pe2, (2, 8, nsub * KV_WIDTH)).reshape(16, nsub * KV_WIDTH)
    w2bd = same_group[:, None, :, None] * w2[None, :, None, :]
    return big.astype(BF16), pe2.astype(BF16), w2bd.reshape(NSA_GROUPS * CMP_HIDDEN, KV_WIDTH).astype(BF16)


def _selection_overlap_t(ncmp_pad, ns):
    cs = np.arange(ncmp_pad) * CMP_STRIDE
    ss = np.arange(ns) * SEL_BLOCK
    ov = np.minimum(cs[:, None] + CMP_BLOCK, ss[None, :] + SEL_BLOCK) - np.maximum(cs[:, None], ss[None, :])
    return (np.clip(ov, 0, None).astype(np.float32) / CMP_BLOCK).T


def _slc_steps(ns):
    return -(-ns // SLC_CHUNK)


def _active_blocks(cnt):
    ns = cnt.shape[-1]
    before_tile = jnp.arange(ns)[None, :] < (Q_BLOCK // SEL_BLOCK) * jnp.arange(cnt.shape[2])[:, None]
    flags = (cnt[:, :, :, 0, :] > 0.0) & before_tile
    order = jnp.argsort(jnp.where(flags, 0, 1).astype(jnp.int32), axis=-1, stable=True).astype(jnp.int32)
    n_act = jnp.sum(flags, axis=-1, dtype=jnp.int32)
    lead = order.shape[:-1]
    steps = _slc_steps(ns)
    order = jnp.pad(order, [(0, 0)] * len(lead) + [(0, steps * SLC_CHUNK - ns)]).reshape(lead + (steps, SLC_CHUNK))
    order = jnp.pad(order, [(0, 0)] * (len(lead) + 1) + [(0, 4 * SLC_WORDS - SLC_CHUNK)])
    packed = jnp.sum(order.reshape(lead + (steps * SLC_WORDS, 4)) << (8 * jnp.arange(4, dtype=jnp.int32)), axis=-1)
    return jnp.concatenate([n_act[..., None], packed], axis=-1).reshape(-1)


def kernel(x, mem, g_mix, g_ffn, g_mem, w_mem_kv, w_up, conv_w, conv_b, w_down,
           a_w_in, a_w_alpha, a_b_alpha, a_g_head, a_w_out,
           g_kv, w_kv, pe_k, pe_v, w_ck1, w_ck2, w_cv1, w_cv2,
           b_w_in, b_w_out, g_final):
    batch, t, d = x.shape
    n = batch * t
    m = mem.shape[1]
    tm = min(1024, t)
    xf = x.reshape(n, d)
    memf = mem.reshape(batch * m, d)
    row = lambda v: v.reshape(1, -1)

    mkv0 = _rms_proj(memf, row(g_mem[0]), w_mem_kv[0].astype(BF16), m, BF16).reshape(batch, m, 2 * MEM_WIDTH)
    wa = a_w_in[0]
    c_alr = 2 * GLA_QK + 2 * GLA_V
    w_a = jnp.concatenate([wa[:, :c_alr], wa[:, c_alr + GLA_RANK:], _pad_cols(wa[:, c_alr:c_alr + GLA_RANK], LANE)],
                          axis=1).astype(BF16)
    w_alpha = jnp.pad(a_w_alpha[0], ((0, LANE - GLA_RANK), (0, 0))).astype(BF16)
    q, k, gl, v, r, mq = _inproj_a(xf, row(g_mix[0]), w_a, w_alpha, row(a_b_alpha[0]), tm)
    o = _gla(q, k, gl, v, r, row(a_g_head[0]), batch, tm)
    xf = _attn_out(o, mq, mkv0, a_w_out[0].astype(BF16), xf, batch, tm)
    xf = _ffn(xf, row(g_ffn[0]), w_up[0].astype(BF16), conv_w[0], row(conv_b[0]), w_down[0].astype(BF16),
              row(g_final), batch, tm, final_norm=False)

    wb = b_w_in[0]
    wqt = wb[:, :NSA_WIDTH].T.astype(BF16)
    n_gate = 3 * NSA_HEADS
    wgt = jnp.pad(wb[:, NSA_WIDTH:NSA_WIDTH + n_gate].T, ((0, 32 - n_gate), (0, 0))).astype(BF16)
    wmq = wb[:, NSA_WIDTH + n_gate:].astype(BF16)
    ckv_in, kvs, qt, gt, mq1 = _proj_b(xf, row(g_kv), row(g_mix[1]), w_kv.astype(BF16), wqt, wgt, wmq, batch, tm)
    nsub = t // CMP_STRIDE
    w1k, pek, w2k = _compress_weights(w_ck1, pe_k, w_ck2)
    w1v, pev, w2v = _compress_weights(w_cv1, pe_v, w_cv2)
    ckv = _compress(ckv_in.reshape(2, batch, nsub, CMP_STRIDE * KV_WIDTH),
                    jnp.stack([w1k, w1v]), jnp.stack([pek, pev]), jnp.stack([w2k, w2v]))

    ns = t // SEL_BLOCK
    c_cmp, d_cmp, c_blk, c_win, srow = _alibi_tiles(nsub)
    ovt = jnp.asarray(_selection_overlap_t(nsub, ns), BF16)
    oc, sel, cnt = _nsa_a(qt, ckv, jnp.asarray(c_cmp), jnp.asarray(d_cmp), jnp.asarray(srow), ovt)
    idx = _active_blocks(cnt)
    o1 = _nsa_b(idx, qt, kvs, sel, oc, gt, jnp.asarray(c_blk), jnp.asarray(c_win), jnp.asarray(srow))
    mkv1 = _rms_proj(memf, row(g_mem[1]), w_mem_kv[1].astype(BF16), m, BF16).reshape(batch, m, 2 * MEM_WIDTH)
    xf = _attn_out(o1, mq1, mkv1, b_w_out[0].astype(BF16), xf, batch, tm)
    xf = _ffn(xf, row(g_ffn[1]), w_up[1].astype(BF16), conv_w[1], row(conv_b[1]), w_down[1].astype(BF16),
              row(g_final), batch, tm, final_norm=True)
    return xf.reshape(batch, t, d)
```

```python
import functools

import numpy as np
import jax
import jax.numpy as jnp
from jax import lax
from jax.experimental import pallas as pl
from jax.experimental.pallas import tpu as pltpu

F32 = jnp.float32
BF16 = jnp.bfloat16
EPS = 1e-6
NEG_INF = float("-inf")
LOG2E = 1.4426950408889634

V7X_VMEM_BYTES = 64 * 1024 * 1024
VMEM_LIMIT = V7X_VMEM_BYTES - 8 * 1024 * 1024

MEM_HEADS = 4
MEM_DH = 128
MEM_WIDTH = MEM_HEADS * MEM_DH
GLA_HEADS = 4
GLA_DK = 64
GLA_DV = 128
GLA_QK = GLA_HEADS * GLA_DK
GLA_V = GLA_HEADS * GLA_DV
GLA_RANK = 16
GLA_TAU = 16.0
GLA_CHUNK = 64
GLA_LEVELS = 6
NSA_HEADS = 8
NSA_GROUPS = 2
NSA_HPG = NSA_HEADS // NSA_GROUPS
NSA_DH = 64
NSA_WIDTH = NSA_HEADS * NSA_DH
KV_WIDTH = NSA_GROUPS * NSA_DH
CMP_BLOCK = 32
CMP_STRIDE = 16
CMP_HIDDEN = 256
SEL_BLOCK = 64
SEL_TOPN = 16
WINDOW = 512
Q_BLOCK = 128
WIN_KEYS = WINDOW + Q_BLOCK
WIN_CHUNK = WIN_KEYS
CMP_CHUNK = 128
SLC_CHUNK = 6
SLC_WORDS = -(-SLC_CHUNK // 4)
RANK_FIRST_TILE = SEL_TOPN * SEL_BLOCK // Q_BLOCK
SEL_TILES = 4
ATT_TILES = 4
FFN_DIM = 2816
FFN_CHUNK = 256
LANE = 128


def _dot(a, b):
    return jnp.dot(a, b, preferred_element_type=F32)


def _dot_nt(a, b):
    return lax.dot_general(a, b, (((1,), (1,)), ((), ())), preferred_element_type=F32)


def _dot_tn(a, b):
    return lax.dot_general(a, b, (((0,), (0,)), ((), ())), preferred_element_type=F32)


def _params(sem):
    return pltpu.CompilerParams(dimension_semantics=sem, vmem_limit_bytes=VMEM_LIMIT)


def _const_spec(shape):
    n = len(shape)
    return pl.BlockSpec(shape, lambda *_: (0,) * n)


def _normalize(x):
    return x * lax.rsqrt(jnp.mean(x * x, axis=-1, keepdims=True) + EPS)


def _sigmoid(x):
    return 1.0 / (1.0 + jnp.exp(-x))


def _rms_proj_kernel(x_ref, g_ref, w_ref, o_ref):
    h = (_normalize(x_ref[...]) * g_ref[...]).astype(BF16)
    o_ref[...] = _dot(h, w_ref[...]).astype(o_ref.dtype)


def _rms_proj(x, g, w, tm, out_dtype):
    n, d = x.shape
    p = w.shape[1]
    return pl.pallas_call(
        _rms_proj_kernel,
        out_shape=jax.ShapeDtypeStruct((n, p), out_dtype),
        grid=(n // tm,),
        in_specs=[pl.BlockSpec((tm, d), lambda i: (i, 0)), _const_spec((1, d)), _const_spec((d, p))],
        out_specs=pl.BlockSpec((tm, p), lambda i: (i, 0)),
        compiler_params=_params(("parallel",)),
        name="rms_proj",
    )(x, g, w)


def _inproj_a_kernel(x_ref, g_ref, w_ref, wa_ref, ba_ref, q_ref, k_ref, gl_ref, v_ref, r_ref, mq_ref):
    h = (_normalize(x_ref[...]) * g_ref[...]).astype(BF16)
    c = 0
    q_ref[...] = _dot(h, w_ref[:, c:c + GLA_QK]) * (GLA_DK ** -0.5)
    c += GLA_QK
    k_ref[...] = _dot(h, w_ref[:, c:c + GLA_QK])
    c += GLA_QK
    v_ref[...] = _dot(h, w_ref[:, c:c + GLA_V]).astype(BF16)
    c += GLA_V
    r_ref[...] = _dot(h, w_ref[:, c:c + GLA_V])
    c += GLA_V
    mq_ref[...] = _dot(h, w_ref[:, c:c + MEM_WIDTH]).astype(BF16)
    c += MEM_WIDTH
    alr = _dot(h, w_ref[:, c:c + LANE]).astype(BF16)
    z = _dot(alr, wa_ref[...]) + ba_ref[...]
    log_sig = jnp.minimum(z, 0.0) - jnp.log1p(jnp.exp(-jnp.abs(z)))
    gl_ref[...] = log_sig * (1.0 / GLA_TAU)


def _inproj_a(x, g, w, wa, ba, tm):
    n, d = x.shape
    row = lambda i: (i, 0)
    outs = [
        jax.ShapeDtypeStruct((n, GLA_QK), F32), jax.ShapeDtypeStruct((n, GLA_QK), F32),
        jax.ShapeDtypeStruct((n, GLA_QK), F32), jax.ShapeDtypeStruct((n, GLA_V), BF16),
        jax.ShapeDtypeStruct((n, GLA_V), F32), jax.ShapeDtypeStruct((n, MEM_WIDTH), BF16),
    ]
    return pl.pallas_call(
        _inproj_a_kernel,
        out_shape=outs,
        grid=(n // tm,),
        in_specs=[pl.BlockSpec((tm, d), row), _const_spec((1, d)), _const_spec(w.shape),
                  _const_spec(wa.shape), _const_spec(ba.shape)],
        out_specs=[pl.BlockSpec((tm, s.shape[1]), row) for s in outs],
        compiler_params=_params(("parallel",)),
        name="inproj_a",
    )(x, g, w, wa, ba)


def _gla_constants():
    c = GLA_CHUNK
    w = np.zeros(((GLA_LEVELS + 2) * c, c), np.float32)
    masks = np.zeros((GLA_LEVELS + 1, c, c), np.float32)
    masks[0] = np.eye(c)
    for l in range(1, GLA_LEVELS + 1):
        blk, half = 2 ** l, 2 ** (l - 1)
        for i in range(c):
            mid = (i // blk) * blk + half - 1
            if i % blk >= half:
                w[(l - 1) * c + i, mid + 1:i + 1] = 1.0
            else:
                w[(l - 1) * c + i, i + 1:mid + 1] = 1.0
        for t in range(c):
            for s in range(c):
                if t // blk == s // blk and t % blk >= half and s % blk < half:
                    masks[l, t, s] = 1.0
    for i in range(c):
        w[GLA_LEVELS * c + i, :i + 1] = 1.0
        w[(GLA_LEVELS + 1) * c + i, i + 1:] = 1.0
    return np.concatenate([w, w], axis=1), np.tile(masks, (1, 1, GLA_HEADS))


def _gla_kernel(q_ref, k_ref, gl_ref, v_ref, r_ref, gh_ref, wcat_ref, msk_ref, o_ref, st_ref, *, n_chunks):
    c = GLA_CHUNK

    @pl.when(pl.program_id(1) == 0)
    def _():
        st_ref[...] = jnp.zeros_like(st_ref)

    lane_qk = lax.broadcasted_iota(jnp.int32, (1, GLA_QK), 1) // GLA_DK
    lane_v = lax.broadcasted_iota(jnp.int32, (1, GLA_V), 1) // GLA_DV
    row_qk = lax.broadcasted_iota(jnp.int32, (GLA_QK, 1), 0) // GLA_DK
    ones = jnp.ones((2 * c, LANE), BF16)
    wcat = wcat_ref[...]

    def stack_heads(x, lane_head):
        return jnp.concatenate([jnp.where(lane_head == h, x, jnp.zeros_like(x)) for h in range(GLA_HEADS)], axis=0)

    chunks = [slice(ci * c, (ci + 1) * c) for ci in range(n_chunks)]
    es, decs = [], []
    for rows in chunks:
        g = gl_ref[rows, :] * LOG2E
        g_hi = g.astype(BF16)
        g_split = jnp.concatenate([g_hi, (g - g_hi.astype(F32)).astype(BF16)], axis=0)
        es.append(jnp.exp2(_dot(wcat, g_split)))
        decs.append(jnp.exp2(_dot_tn(g_split, ones)))

    o_intras, q_ins, upds = [], [], []
    for rows, e in zip(chunks, es):
        q = q_ref[rows, :]
        k = k_ref[rows, :]
        v = v_ref[rows, :]
        attn = jnp.zeros((c, GLA_HEADS * c), F32)
        for l in range(GLA_LEVELS + 1):
            if l == 0:
                ql, kl = q, k
            else:
                el = e[(l - 1) * c:l * c]
                ql, kl = q * el, k * el
            attn = attn + _dot_nt(ql.astype(BF16), stack_heads(kl, lane_qk).astype(BF16)) * msk_ref[l]
        o_intras.append(_dot(attn.astype(BF16), stack_heads(v, lane_v)))
        q_ins.append((q * e[GLA_LEVELS * c:(GLA_LEVELS + 1) * c]).astype(BF16))
        kk = stack_heads(k * e[(GLA_LEVELS + 1) * c:(GLA_LEVELS + 2) * c], lane_qk).astype(BF16)
        v_rows = jnp.concatenate([v[:, h * GLA_DV:(h + 1) * GLA_DV] for h in range(GLA_HEADS)], axis=0)
        upds.append(_dot_tn(kk, v_rows))

    st = st_ref[...]
    outs = []
    for o_intra, q_in, dec, upd in zip(o_intras, q_ins, decs, upds):
        st_bd = jnp.concatenate([jnp.where(row_qk == h, st, 0.0).astype(BF16) for h in range(GLA_HEADS)], axis=1)
        outs.append(o_intra + _dot(q_in, st_bd))
        st = dec * st + upd
    st_ref[...] = st

    for rows, o in zip(chunks, outs):
        for h in range(GLA_HEADS):
            cs = slice(h * GLA_DV, (h + 1) * GLA_DV)
            on = _normalize(o[:, cs]) * gh_ref[...]
            r = r_ref[rows, cs]
            o_ref[rows, cs] = (on * (r * _sigmoid(r))).astype(BF16)


def _gla(q, k, gl, v, r, g_head, batch, ct):
    n = q.shape[0]
    t = n // batch
    nt = t // ct
    wcat, masks = _gla_constants()
    row = lambda b, i: (b * nt + i, 0)
    return pl.pallas_call(
        functools.partial(_gla_kernel, n_chunks=ct // GLA_CHUNK),
        out_shape=jax.ShapeDtypeStruct((n, GLA_V), BF16),
        grid=(batch, nt),
        in_specs=[pl.BlockSpec((ct, GLA_QK), row), pl.BlockSpec((ct, GLA_QK), row),
                  pl.BlockSpec((ct, GLA_QK), row), pl.BlockSpec((ct, GLA_V), row),
                  pl.BlockSpec((ct, GLA_V), row), _const_spec((1, GLA_DV)),
                  _const_spec(wcat.shape), _const_spec(masks.shape)],
        out_specs=pl.BlockSpec((ct, GLA_V), row),
        scratch_shapes=[pltpu.VMEM((GLA_QK, GLA_DV), F32)],
        compiler_params=_params(("parallel", "arbitrary")),
        name="gla",
    )(q, k, gl, v, r, g_head, jnp.asarray(wcat, BF16), jnp.asarray(masks, F32))


def _attn_out_kernel(o_ref, mq_ref, mk_ref, mv_ref, w_ref, x_ref, out_ref):
    heads = [slice(h * MEM_DH, (h + 1) * MEM_DH) for h in range(MEM_HEADS)]
    scores = [_dot_nt(mq_ref[:, cs], mk_ref[0, :, cs]) * (MEM_DH ** -0.5 * LOG2E) for cs in heads]
    parts = [o_ref[...]]
    for s, cs in zip(scores, heads):
        e = jnp.exp2(s - jnp.max(s, axis=-1, keepdims=True))
        p = e / jnp.sum(e, axis=-1, keepdims=True)
        parts.append(_dot(p.astype(BF16), mv_ref[0, :, cs]).astype(BF16))
    cat = jnp.concatenate(parts, axis=-1)
    out_ref[...] = x_ref[...] + _dot(cat, w_ref[...])


def _attn_out(o, mq, mkv, w_out, x, batch, tm):
    n, d = x.shape
    t = n // batch
    nt = t // tm
    m = mkv.shape[1]
    row = lambda b, i: (b * nt + i, 0)
    return pl.pallas_call(
        _attn_out_kernel,
        out_shape=jax.ShapeDtypeStruct((n, d), F32),
        grid=(batch, nt),
        in_specs=[pl.BlockSpec((tm, o.shape[1]), row), pl.BlockSpec((tm, MEM_WIDTH), row),
                  pl.BlockSpec((1, m, MEM_WIDTH), lambda b, i: (b, 0, 0)),
                  pl.BlockSpec((1, m, MEM_WIDTH), lambda b, i: (b, 0, 1)),
                  _const_spec(w_out.shape), pl.BlockSpec((tm, d), row)],
        out_specs=pl.BlockSpec((tm, d), row),
        compiler_params=_params(("parallel", "parallel")),
        name="attn_out",
    )(o, mq, mkv, mkv, w_out, x)


def _ffn_kernel(x_ref, g_ref, wup_ref, cw_ref, cb_ref, wd_ref, gf_ref, out_ref, act_ref, tail_ref, *,
                final_norm):
    @pl.when(pl.program_id(1) == 0)
    def _():
        tail_ref[...] = jnp.zeros_like(tail_ref)

    x = x_ref[...]
    tm = x.shape[0]
    h = (_normalize(x) * g_ref[...]).astype(BF16)
    rid = lax.broadcasted_iota(jnp.int32, (8, FFN_CHUNK), 0)
    for j in range(FFN_DIM // FFN_CHUNK):
        cs = slice(j * FFN_CHUNK, (j + 1) * FFN_CHUNK)
        a = _dot(h, wup_ref[:, cs])
        b = _dot(h, wup_ref[:, FFN_DIM + j * FFN_CHUNK:FFN_DIM + (j + 1) * FFN_CHUNK])
        tail = tail_ref[j]
        r1 = pltpu.roll(a, 1, 0)
        r2 = pltpu.roll(a, 2, 0)
        top1 = jnp.where(rid == 0, tail[7:8], r1[:8])
        top2 = jnp.where(rid == 0, tail[6:7], jnp.where(rid == 1, tail[7:8], r2[:8]))
        a1 = jnp.concatenate([top1, r1[8:]], axis=0)
        a2 = jnp.concatenate([top2, r2[8:]], axis=0)
        tail_ref[j] = a[tm - 8:]
        ac = a2 * cw_ref[0:1, cs] + a1 * cw_ref[1:2, cs] + a * cw_ref[2:3, cs] + cb_ref[:, cs]
        act_ref[:, cs] = (ac * _sigmoid(ac) * b).astype(BF16)
    y = x + _dot(act_ref[...], wd_ref[...])
    if final_norm:
        y = _normalize(y) * gf_ref[...]
    out_ref[...] = y


def _ffn(x, g, w_up, conv_w, conv_b, w_down, g_final, batch, tm, final_norm):
    n, d = x.shape
    nt = n // batch // tm
    row = lambda b, i: (b * nt + i, 0)
    once = pl.Buffered(1)
    return pl.pallas_call(
        functools.partial(_ffn_kernel, final_norm=final_norm),
        out_shape=jax.ShapeDtypeStruct((n, d), F32),
        grid=(batch, nt),
        in_specs=[pl.BlockSpec((tm, d), row), _const_spec((1, d)),
                  pl.BlockSpec(w_up.shape, lambda b, i: (0, 0), pipeline_mode=once),
                  _const_spec(conv_w.shape), _const_spec(conv_b.shape),
                  pl.BlockSpec(w_down.shape, lambda b, i: (0, 0), pipeline_mode=once),
                  _const_spec((1, d))],
        out_specs=pl.BlockSpec((tm, d), row),
        scratch_shapes=[pltpu.VMEM((tm, FFN_DIM), BF16),
                        pltpu.VMEM((FFN_DIM // FFN_CHUNK, 8, FFN_CHUNK), F32)],
        compiler_params=_params(("parallel", "arbitrary")),
        name="conv_ffn",
    )(x, g, w_up, conv_w, conv_b, w_down, g_final)


def _proj_b_kernel(x_ref, gkv_ref, gmix_ref, wkv_ref, wqt_ref, wgt_ref, wmq_ref,
                   cmp_ref, kvs_ref, qt_ref, gt_ref, mq_ref, regroup_ref):
    xn = _normalize(x_ref[...])
    hkv = (xn * gkv_ref[...]).astype(BF16)
    h1 = (xn * gmix_ref[...]).astype(BF16)
    kv_cmp = _dot(hkv, wkv_ref[:, 0:2 * KV_WIDTH])
    n_sub = x_ref.shape[0] // CMP_STRIDE
    for s in range(2):
        regroup_ref[s] = kv_cmp[:, s * KV_WIDTH:(s + 1) * KV_WIDTH]
        for p in range(CMP_STRIDE):
            cmp_ref[s, :, p * KV_WIDTH:(p + 1) * KV_WIDTH] = \
                regroup_ref[s, pl.ds(p, n_sub, stride=CMP_STRIDE), :].astype(BF16)
    kvs_ref[...] = _dot(hkv, wkv_ref[:, 2 * KV_WIDTH:]).astype(BF16)
    qt_ref[0] = (_dot_nt(wqt_ref[...], h1) * (NSA_DH ** -0.5 * LOG2E)).astype(BF16)
    gt_ref[0] = _sigmoid(_dot_nt(wgt_ref[...], h1))
    mq_ref[...] = _dot(h1, wmq_ref[...]).astype(BF16)


def _proj_b(x, g_kv, g_mix, w_kv, wqt, wgt, wmq, batch, tm):
    n, d = x.shape
    t = n // batch
    nt = t // tm
    row = lambda b, i: (b * nt + i, 0)
    outs = [
        jax.ShapeDtypeStruct((2, n // CMP_STRIDE, CMP_STRIDE * KV_WIDTH), BF16),
        jax.ShapeDtypeStruct((n, 4 * KV_WIDTH), BF16),
        jax.ShapeDtypeStruct((batch, NSA_WIDTH, t), BF16),
        jax.ShapeDtypeStruct((batch, wgt.shape[0], t), F32),
        jax.ShapeDtypeStruct((n, MEM_WIDTH), BF16),
    ]
    return pl.pallas_call(
        _proj_b_kernel,
        out_shape=outs,
        grid=(batch, nt),
        in_specs=[pl.BlockSpec((tm, d), row), _const_spec((1, d)), _const_spec((1, d)),
                  _const_spec(w_kv.shape), _const_spec(wqt.shape), _const_spec(wgt.shape),
                  _const_spec(wmq.shape)],
        out_specs=[pl.BlockSpec((2, tm // CMP_STRIDE, CMP_STRIDE * KV_WIDTH), lambda b, i: (0, b * nt + i, 0)),
                   pl.BlockSpec((tm, 4 * KV_WIDTH), row),
                   pl.BlockSpec((1, NSA_WIDTH, tm), lambda b, i: (b, 0, i)),
                   pl.BlockSpec((1, wgt.shape[0], tm), lambda b, i: (b, 0, i)),
                   pl.BlockSpec((tm, MEM_WIDTH), row)],
        scratch_shapes=[pltpu.VMEM((2, tm, KV_WIDTH), F32)],
        compiler_params=_params(("parallel", "parallel")),
        name="proj_b",
    )(x, g_kv, g_mix, w_kv, wqt, wgt, wmq)


def _compress_kernel(a_ref, w1_ref, pe_ref, w2_ref, o_ref):
    a = a_ref[0, 0]
    w1 = w1_ref[0]
    u = _dot(a, w1)
    cpe = _dot(pe_ref[0], w1)
    nrow = a.shape[0]
    hid = []
    for g in range(NSA_GROUPS):
        c0 = slice((2 * g) * CMP_HIDDEN, (2 * g + 1) * CMP_HIDDEN)
        c1 = slice((2 * g + 1) * CMP_HIDDEN, (2 * g + 2) * CMP_HIDDEN)
        nxt = pltpu.roll(u[:, c1], nrow - 1, 0)
        hid.append(u[:, c0] + nxt + cpe[0:1, c0] + cpe[8:9, c1])
    hcat = jax.nn.gelu(jnp.concatenate(hid, axis=-1), approximate=True).astype(BF16)
    o_ref[0, 0] = _dot(hcat, w2_ref[0]).astype(BF16)


def _compress(a, w1, pe, w2):
    _, batch, nsub, width = a.shape
    return pl.pallas_call(
        _compress_kernel,
        out_shape=jax.ShapeDtypeStruct((2, batch, nsub, KV_WIDTH), BF16),
        grid=(2, batch),
        in_specs=[pl.BlockSpec((1, 1, nsub, width), lambda s, b: (s, b, 0, 0)),
                  pl.BlockSpec((1,) + w1.shape[1:], lambda s, b: (s, 0, 0)),
                  pl.BlockSpec((1,) + pe.shape[1:], lambda s, b: (s, 0, 0)),
                  pl.BlockSpec((1,) + w2.shape[1:], lambda s, b: (s, 0, 0))],
        out_specs=pl.BlockSpec((1, 1, nsub, KV_WIDTH), lambda s, b: (s, b, 0, 0)),
        compiler_params=_params(("parallel", "parallel")),
        name="compress",
    )(a, w1, pe, w2)


def _alibi_tiles(ncmp_pad):
    slopes = LOG2E * 2.0 ** (-8.0 * np.arange(1, NSA_HEADS + 1) / NSA_HEADS)
    tl = np.tile(np.arange(Q_BLOCK), NSA_HPG)[None, :]
    cmp_end = (np.arange(ncmp_pad) * CMP_STRIDE + CMP_BLOCK - 1)[:, None]
    key = np.arange(SEL_BLOCK)[:, None]
    dist_win = tl + WINDOW - np.arange(WIN_KEYS)[:, None]
    c_cmp = np.zeros((NSA_GROUPS, ncmp_pad, NSA_HPG * Q_BLOCK), np.float32)
    c_blk = np.zeros((NSA_GROUPS, SEL_BLOCK, NSA_HPG * Q_BLOCK), np.float32)
    c_win = np.zeros((NSA_GROUPS, WIN_KEYS, NSA_HPG * Q_BLOCK), np.float32)
    srow = np.zeros((NSA_GROUPS, 1, NSA_HPG * Q_BLOCK), np.float32)
    for g in range(NSA_GROUPS):
        s = np.repeat(slopes[g * NSA_HPG:(g + 1) * NSA_HPG], Q_BLOCK)[None, :]
        c_cmp[g] = -s * (tl - cmp_end)
        c_blk[g] = -s * (tl - key)
        c_win[g] = np.where((dist_win >= 0) & (dist_win < WINDOW), -s * dist_win, -np.inf)
        srow[g] = s
    d_cmp = (cmp_end - tl).astype(np.float32)
    return c_cmp, d_cmp, c_blk, c_win, srow


def _group_queries(qt_ref, g, tile=0):
    zeros = jnp.zeros((NSA_DH, Q_BLOCK), BF16)
    cols = []
    for hh in range(NSA_HPG):
        h = g * NSA_HPG + hh
        piece = qt_ref[0, h * NSA_DH:(h + 1) * NSA_DH, tile * Q_BLOCK:(tile + 1) * Q_BLOCK]
        cols.append(jnp.concatenate([piece, zeros] if g == 0 else [zeros, piece], axis=0))
    return jnp.concatenate(cols, axis=1)


def _nsa_a_kernel(qt_ref, ckv_ref, ccmp_ref, dcmp_ref, srow_ref, ovt_ref, oc_ref, sel_ref, cnt_ref, imp_ref):
    first_tile = pl.program_id(1) * SEL_TILES
    nblk = sel_ref.shape[3]
    jidx = lax.broadcasted_iota(jnp.int32, (nblk, Q_BLOCK), 0)
    tl = lax.broadcasted_iota(jnp.int32, (nblk, Q_BLOCK), 1)
    units = [(k, g) for k in range(SEL_TILES) for g in range(NSA_GROUPS)]
    qbs = [first_tile + k for k in range(SEL_TILES)]
    bases = [(qb * Q_BLOCK).astype(F32) for qb in qbs]
    curs = [2 * qb + jnp.where(tl >= SEL_BLOCK, 1, 0) for qb in qbs]
    n_chunks = (qbs[-1] * (Q_BLOCK // CMP_STRIDE) + (Q_BLOCK - CMP_BLOCK) // CMP_STRIDE) // CMP_CHUNK + 1
    qts = {(k, g): _group_queries(qt_ref, g, k) for k, g in units}
    shifts = {(k, g): srow_ref[g] * bases[k] for k, g in units}

    def cmp_branch(n_rows):
        kc = ckv_ref[0, 0, 0:n_rows, :]
        vc = ckv_ref[1, 0, 0:n_rows, :]
        ov = ovt_ref[:, 0:n_rows]
        lane_grp = lax.broadcasted_iota(jnp.int32, vc.shape, 1) // NSA_DH
        raws = {u: _dot(kc, qts[u]) for u in units}
        for k, g in units:
            visible = dcmp_ref[0:n_rows, :] <= bases[k]
            s = raws[k, g] + jnp.where(visible, ccmp_ref[g, 0:n_rows, :] - shifts[k, g], NEG_INF)
            m = jnp.max(s, axis=0, keepdims=True)
            e = jnp.exp2((s - jnp.where(m == NEG_INF, 0.0, m)).astype(BF16))
            pv = _dot_tn(jnp.where(lane_grp == g, vc, jnp.ones_like(vc)), e)
            inv = 1.0 / jnp.maximum(pv[(1 - g) * NSA_DH:(1 - g) * NSA_DH + 1], 1e-30)
            oc = pv[g * NSA_DH:(g + 1) * NSA_DH] * inv
            imp = _dot(ov, e) * inv
            for hh in range(NSA_HPG):
                h = g * NSA_HPG + hh
                oc_ref[0, h * NSA_DH:(h + 1) * NSA_DH, k * Q_BLOCK:(k + 1) * Q_BLOCK] = \
                    oc[:, hh * Q_BLOCK:(hh + 1) * Q_BLOCK]
            imp_g = imp[:, 0:Q_BLOCK]
            for hh in range(1, NSA_HPG):
                imp_g = imp_g + imp[:, hh * Q_BLOCK:(hh + 1) * Q_BLOCK]
            imp_ref[k, g] = imp_g

    for nc in range(1, ckv_ref.shape[2] // CMP_CHUNK + 1):
        pl.when(n_chunks == nc)(functools.partial(cmp_branch, nc * CMP_CHUNK))

    def emit(k, g, sel):
        sel_ref[0, g, k] = sel
        cnt_ref[0, g, k] = _dot_nt(jnp.ones((8, Q_BLOCK), BF16), sel.astype(BF16))

    @pl.when(first_tile < RANK_FIRST_TILE)
    def _():
        for k, g in units:
            emit(k, g, jnp.where(jidx <= curs[k], 1.0, 0.0))

    def rank_branch(rows):
        jr = lax.broadcasted_iota(jnp.int32, (rows, Q_BLOCK), 0)
        jf = jr.astype(F32)
        cur_r = [2 * qb + jnp.where(lax.broadcasted_iota(jnp.int32, (rows, Q_BLOCK), 1) >= SEL_BLOCK, 1, 0)
                 for qb in qbs]
        v0 = {(k, g): jnp.where((jr >= 1) & (jr <= cur_r[k] - 2), imp_ref[k, g, 0:rows, :], NEG_INF)
              for k, g in units}
        v = dict(v0)
        for _ in range(SEL_TOPN - 3):
            for u in units:
                mx = jnp.max(v[u], axis=0, keepdims=True)
                first = jnp.min(jnp.where(v[u] == mx, jf, float(nblk)), axis=0, keepdims=True)
                v[u] = jnp.where(jf == first, NEG_INF, v[u])
        for k, g in units:
            cur = cur_r[k]
            taken = (v[k, g] != v0[k, g]) | (jr == 0) | (jr == cur) | (jr == cur - 1)
            sel = jnp.where(taken, 1.0, 0.0)
            if rows < nblk:
                sel = jnp.concatenate([sel, jnp.zeros((nblk - rows, Q_BLOCK), F32)], axis=0)
            emit(k, g, sel)

    @pl.when(first_tile >= RANK_FIRST_TILE)
    def _():
        blocks_per_variant = CMP_CHUNK * CMP_STRIDE // SEL_BLOCK
        for nc in range(1, ckv_ref.shape[2] // CMP_CHUNK + 1):
            pl.when(n_chunks == nc)(functools.partial(rank_branch, min(nblk, nc * blocks_per_variant)))


def _nsa_a(qt, ckv, c_cmp, d_cmp, srow, ovt):
    batch, _, t = qt.shape
    nqb = t // Q_BLOCK
    ns = t // SEL_BLOCK
    ncmp = ckv.shape[2]
    outs = [jax.ShapeDtypeStruct((batch, NSA_WIDTH, t), F32),
            jax.ShapeDtypeStruct((batch, NSA_GROUPS, nqb, ns, Q_BLOCK), F32),
            jax.ShapeDtypeStruct((batch, NSA_GROUPS, nqb, 8, ns), F32)]
    assert nqb % SEL_TILES == 0 and RANK_FIRST_TILE % SEL_TILES == 0
    return pl.pallas_call(
        _nsa_a_kernel,
        out_shape=outs,
        grid=(batch, nqb // SEL_TILES),
        in_specs=[pl.BlockSpec((1, NSA_WIDTH, SEL_TILES * Q_BLOCK), lambda b, i: (b, 0, i)),
                  pl.BlockSpec((2, 1, ncmp, KV_WIDTH), lambda b, i: (0, b, 0, 0)),
                  _const_spec(c_cmp.shape), _const_spec(d_cmp.shape), _const_spec(srow.shape),
                  _const_spec(ovt.shape)],
        out_specs=[pl.BlockSpec((1, NSA_WIDTH, SEL_TILES * Q_BLOCK), lambda b, i: (b, 0, i)),
                   pl.BlockSpec((1, NSA_GROUPS, SEL_TILES, ns, Q_BLOCK), lambda b, i: (b, 0, i, 0, 0)),
                   pl.BlockSpec((1, NSA_GROUPS, SEL_TILES, 8, ns), lambda b, i: (b, 0, i, 0, 0))],
        scratch_shapes=[pltpu.VMEM((SEL_TILES, NSA_GROUPS, ns, Q_BLOCK), F32)],
        compiler_params=_params(("parallel", "parallel")),
        name="nsa_select",
    )(qt, ckv, c_cmp, d_cmp, srow, ovt)


def _block_softmax_step(carry, raw, consts, shifts, v_rows, g):
    m, l, acc = carry
    width = raw.shape[1]
    fold = lambda x: x.reshape(SEL_BLOCK // 8, 8, width)
    us = [raw[i * SEL_BLOCK:(i + 1) * SEL_BLOCK] + c for i, c in enumerate(consts)]
    top8 = None
    for u, sh in zip(us, shifts):
        t = jnp.max(fold(u), axis=0) + sh
        top8 = t if top8 is None else jnp.maximum(top8, t)
    m_new = jnp.maximum(m, jnp.max(top8, axis=0, keepdims=True))
    m_safe = jnp.where(m_new == NEG_INF, 0.0, m_new)
    alpha = jnp.where(m == NEG_INF, 0.0, jnp.exp2(m - m_safe))
    ps = [jnp.exp2((u - (m_safe - sh)).astype(BF16)) for u, sh in zip(us, shifts)]
    lane_grp = lax.broadcasted_iota(jnp.int32, v_rows.shape, 1) // NSA_DH
    v_aug = jnp.where(lane_grp == g, v_rows, jnp.ones_like(v_rows))
    pv = _dot_tn(v_aug, jnp.concatenate(ps, axis=0))
    og = 1 - g
    return m_new, alpha * l + pv[og * NSA_DH:og * NSA_DH + 1], alpha * acc + pv[g * NSA_DH:(g + 1) * NSA_DH]


def _nsa_b_kernel(idx_ref, qt_ref, ks_ref, vs_ref, kw_ref, vw_ref, sel_ref, oc_ref, gt_ref, cblk_ref, cwin_ref,
                  srow_ref, o_ref, *, idx_words):
    b = pl.program_id(0)
    first_tile = pl.program_id(1) * ATT_TILES
    width = NSA_HPG * Q_BLOCK
    units = [(k, g) for k in range(ATT_TILES) for g in range(NSA_GROUPS)]
    qbs = [first_tile + k for k in range(ATT_TILES)]
    bases = [qb * Q_BLOCK for qb in qbs]
    qts = {(k, g): _group_queries(qt_ref, g, k) for k, g in units}
    srows = [srow_ref[g] for g in range(NSA_GROUPS)]
    n_tiles = pl.num_programs(1) * ATT_TILES
    entries = {(k, g): ((b * NSA_GROUPS + g) * n_tiles + qbs[k]) * idx_words for k, g in units}
    n_act = {u: idx_ref[entries[u]] for u in units}
    init = (jnp.full((1, width), NEG_INF, F32), jnp.zeros((1, width), F32), jnp.zeros((NSA_DH, width), F32))
    zero_row = jnp.zeros((1, width), F32)

    def gate_row(k, g, j, live):
        picked = sel_ref[0, g, k, pl.ds(j, 1), :]
        gate = jnp.where((picked > 0.0) & live, 0.0, NEG_INF)
        return jnp.concatenate([gate] * NSA_HPG, axis=1)

    def stage(it):
        out = {}
        for k, g in units:
            words = [idx_ref[entries[k, g] + 1 + it * SLC_WORDS + w] for w in range(SLC_WORDS)]
            kts, vts, shifts = [], [], []
            for i in range(SLC_CHUNK):
                live = it * SLC_CHUNK + i < n_act[k, g]
                j = jnp.where(live, lax.shift_right_logical(words[i // 4], 8 * (i % 4)) & 0xFF, 0)
                k0 = pl.multiple_of(j * SEL_BLOCK, SEL_BLOCK)
                kts.append(ks_ref[pl.ds(k0, SEL_BLOCK), :])
                vts.append(vs_ref[pl.ds(k0, SEL_BLOCK), :])
                shifts.append(gate_row(k, g, j, live) - srows[g] * (bases[k] - k0).astype(F32))
            out[k, g] = (_dot(jnp.concatenate(kts, axis=0), qts[k, g]), shifts, jnp.concatenate(vts, axis=0))
        return out

    def body(it, states):
        staged = stage(it)
        return tuple(_block_softmax_step(state, staged[u][0], [cblk_ref[u[1]]] * SLC_CHUNK, staged[u][1],
                                         staged[u][2], u[1]) for u, state in zip(units, states))

    longest = functools.reduce(jnp.maximum, [n_act[u] for u in units])
    far = lax.fori_loop(0, (longest + SLC_CHUNK - 1) // SLC_CHUNK, body, (init,) * len(units))
    slc = dict(zip(units, far))

    diag = range(WINDOW // SEL_BLOCK, WIN_KEYS // SEL_BLOCK)
    win = {u: init for u in units}
    steps = []
    for k, g in units:
        r0 = pl.multiple_of(bases[k], Q_BLOCK)
        shifts = [gate_row(k, g, 2 * qbs[k] + i, True) for i in range(len(diag))]
        steps.append((slc, (k, g), ks_ref[pl.ds(r0, Q_BLOCK), :], vs_ref[pl.ds(r0, Q_BLOCK), :], diag, shifts))
    for k in range(ATT_TILES):
        for c in range(WIN_KEYS // WIN_CHUNK):
            blocks = range(c * (WIN_CHUNK // SEL_BLOCK), (c + 1) * (WIN_CHUNK // SEL_BLOCK))
            kws, vws, offs = [], [], []
            for r in blocks:
                j = 2 * qbs[k] - WINDOW // SEL_BLOCK + r
                k0 = pl.multiple_of(jnp.maximum(j, 0) * SEL_BLOCK, SEL_BLOCK)
                kws.append(kw_ref[pl.ds(k0, SEL_BLOCK), :])
                vws.append(vw_ref[pl.ds(k0, SEL_BLOCK), :])
                offs.append(zero_row + jnp.where(j >= 0, 0.0, NEG_INF))
            kw = jnp.concatenate(kws, axis=0)
            vw = jnp.concatenate(vws, axis=0)
            for g in range(NSA_GROUPS):
                steps.append((win, (k, g), kw, vw, blocks, offs))
    raws = [_dot(keys, qts[u]) for _, u, keys, _, _, _ in steps]
    for raw, (states, u, _, values, blocks, shifts) in zip(raws, steps):
        consts = [cwin_ref[u[1], r * SEL_BLOCK:(r + 1) * SEL_BLOCK, :] for r in blocks]
        states[u] = _block_softmax_step(states[u], raw, consts, shifts, values, u[1])

    for k in range(ATT_TILES):
        cols = slice(k * Q_BLOCK, (k + 1) * Q_BLOCK)
        heads_out = []
        for g in range(NSA_GROUPS):
            o_s = slc[k, g][2] * (1.0 / jnp.maximum(slc[k, g][1], 1e-30))
            o_w = win[k, g][2] * (1.0 / jnp.maximum(win[k, g][1], 1e-30))
            for hh in range(NSA_HPG):
                h = g * NSA_HPG + hh
                cs = slice(hh * Q_BLOCK, (hh + 1) * Q_BLOCK)
                gates = gt_ref[0, 3 * h:3 * h + 3, cols]
                heads_out.append(gates[0:1] * oc_ref[0, h * NSA_DH:(h + 1) * NSA_DH, cols]
                                 + gates[1:2] * o_s[:, cs] + gates[2:3] * o_w[:, cs])
        o_ref[cols, :] = jnp.concatenate(heads_out, axis=0).astype(BF16).T


def _nsa_b(idx, qt, kvs, sel, oc, gt, c_blk, c_win, srow):
    batch, _, t = qt.shape
    nqb = t // Q_BLOCK
    ns = t // SEL_BLOCK
    kv_spec = lambda c: pl.BlockSpec((t, KV_WIDTH), lambda b, i, idx_ref: (b, c))
    grid_spec = pltpu.PrefetchScalarGridSpec(
        num_scalar_prefetch=1,
        grid=(batch, nqb // ATT_TILES),
        in_specs=[pl.BlockSpec((1, NSA_WIDTH, ATT_TILES * Q_BLOCK), lambda b, i, r: (b, 0, i)),
                  kv_spec(0), kv_spec(1), kv_spec(2), kv_spec(3),
                  pl.BlockSpec((1, NSA_GROUPS, ATT_TILES, ns, Q_BLOCK), lambda b, i, r: (b, 0, i, 0, 0)),
                  pl.BlockSpec((1, NSA_WIDTH, ATT_TILES * Q_BLOCK), lambda b, i, r: (b, 0, i)),
                  pl.BlockSpec((1, gt.shape[1], ATT_TILES * Q_BLOCK), lambda b, i, r: (b, 0, i)),
                  pl.BlockSpec(c_blk.shape, lambda b, i, r: (0, 0, 0)),
                  pl.BlockSpec(c_win.shape, lambda b, i, r: (0, 0, 0)),
                  pl.BlockSpec(srow.shape, lambda b, i, r: (0, 0, 0))],
        out_specs=pl.BlockSpec((ATT_TILES * Q_BLOCK, NSA_WIDTH), lambda b, i, r: (b * (nqb // ATT_TILES) + i, 0)),
    )
    assert nqb % ATT_TILES == 0
    return pl.pallas_call(
        functools.partial(_nsa_b_kernel, idx_words=1 + _slc_steps(ns) * SLC_WORDS),
        out_shape=jax.ShapeDtypeStruct((batch * t, NSA_WIDTH), BF16),
        grid_spec=grid_spec,
        compiler_params=_params(("parallel", "parallel")),
        name="nsa_attend",
    )(idx, qt, kvs, kvs, kvs, kvs, sel, oc, gt, c_blk, c_win, srow)


def _pad_cols(w, width):
    return jnp.pad(w, ((0, 0), (0, width - w.shape[1])))


def _compress_weights(w1, pe, w2):
    nsub = CMP_STRIDE
    w1r = w1.reshape(2, nsub, NSA_DH, CMP_HIDDEN)
    same_group = jnp.eye(NSA_GROUPS, dtype=F32)
    big = same_group[None, :, None, :, None, None] * w1r.transpose(1, 2, 0, 3)[:, None, :, None, :, :]
    big = big.reshape(nsub * KV_WIDTH, NSA_GROUPS * 2 * CMP_HIDDEN)
    per = pe.reshape(2, nsub, 1, NSA_DH)
    pe2 = jnp.broadcast_to(per, (2, nsub, NSA_GROUPS, NSA_DH)).reshape(2, 1, nsub * KV_WIDTH)
    pe2 = jnp.broadcast_to(pe2, (2, 8, nsub * KV_WIDTH)).reshape(16, nsub * KV_WIDTH)
    w2bd = same_group[:, None, :, None] * w2[None, :, None, :]
    return big.astype(BF16), pe2.astype(BF16), w2bd.reshape(NSA_GROUPS * CMP_HIDDEN, KV_WIDTH).astype(BF16)


def _selection_overlap_t(ncmp_pad, ns):
    cs = np.arange(ncmp_pad) * CMP_STRIDE
    ss = np.arange(ns) * SEL_BLOCK
    ov = np.minimum(cs[:, None] + CMP_BLOCK, ss[None, :] + SEL_BLOCK) - np.maximum(cs[:, None], ss[None, :])
    return (np.clip(ov, 0, None).astype(np.float32) / CMP_BLOCK).T


def _slc_steps(ns):
    return -(-ns // SLC_CHUNK)


def _active_blocks(cnt):
    ns = cnt.shape[-1]
    before_tile = jnp.arange(ns)[None, :] < (Q_BLOCK // SEL_BLOCK) * jnp.arange(cnt.shape[2])[:, None]
    flags = (cnt[:, :, :, 0, :] > 0.0) & before_tile
    blocks = jnp.arange(ns, dtype=jnp.int32)
    order = jnp.sort(jnp.where(flags, blocks, blocks + ns), axis=-1) % ns
    n_act = jnp.sum(flags, axis=-1, dtype=jnp.int32)
    lead = order.shape[:-1]
    steps = _slc_steps(ns)
    order = jnp.pad(order, [(0, 0)] * len(lead) + [(0, steps * SLC_CHUNK - ns)]).reshape(lead + (steps, SLC_CHUNK))
    order = jnp.pad(order, [(0, 0)] * (len(lead) + 1) + [(0, 4 * SLC_WORDS - SLC_CHUNK)])
    packed = jnp.sum(order.reshape(lead + (steps * SLC_WORDS, 4)) << (8 * jnp.arange(4, dtype=jnp.int32)), axis=-1)
    return jnp.concatenate([n_act[..., None], packed], axis=-1).reshape(-1)


def kernel(x, mem, g_mix, g_ffn, g_mem, w_mem_kv, w_up, conv_w, conv_b, w_down,
           a_w_in, a_w_alpha, a_b_alpha, a_g_head, a_w_out,
           g_kv, w_kv, pe_k, pe_v, w_ck1, w_ck2, w_cv1, w_cv2,
           b_w_in, b_w_out, g_final):
    batch, t, d = x.shape
    n = batch * t
    m = mem.shape[1]
    tm = min(1024, t)
    xf = x.reshape(n, d)
    memf = mem.reshape(batch * m, d)
    row = lambda v: v.reshape(1, -1)

    mkv0 = _rms_proj(memf, row(g_mem[0]), w_mem_kv[0].astype(BF16), m, BF16).reshape(batch, m, 2 * MEM_WIDTH)
    wa = a_w_in[0]
    c_alr = 2 * GLA_QK + 2 * GLA_V
    w_a = jnp.concatenate([wa[:, :c_alr], wa[:, c_alr + GLA_RANK:], _pad_cols(wa[:, c_alr:c_alr + GLA_RANK], LANE)],
                          axis=1).astype(BF16)
    w_alpha = jnp.pad(a_w_alpha[0], ((0, LANE - GLA_RANK), (0, 0))).astype(BF16)
    q, k, gl, v, r, mq = _inproj_a(xf, row(g_mix[0]), w_a, w_alpha, row(a_b_alpha[0]), tm)
    o = _gla(q, k, gl, v, r, row(a_g_head[0]), batch, tm)
    xf = _attn_out(o, mq, mkv0, a_w_out[0].astype(BF16), xf, batch, tm)
    xf = _ffn(xf, row(g_ffn[0]), w_up[0].astype(BF16), conv_w[0], row(conv_b[0]), w_down[0].astype(BF16),
              row(g_final), batch, tm, final_norm=False)

    wb = b_w_in[0]
    wqt = wb[:, :NSA_WIDTH].T.astype(BF16)
    n_gate = 3 * NSA_HEADS
    wgt = jnp.pad(wb[:, NSA_WIDTH:NSA_WIDTH + n_gate].T, ((0, 32 - n_gate), (0, 0))).astype(BF16)
    wmq = wb[:, NSA_WIDTH + n_gate:].astype(BF16)
    ckv_in, kvs, qt, gt, mq1 = _proj_b(xf, row(g_kv), row(g_mix[1]), w_kv.astype(BF16), wqt, wgt, wmq, batch, tm)
    nsub = t // CMP_STRIDE
    w1k, pek, w2k = _compress_weights(w_ck1, pe_k, w_ck2)
    w1v, pev, w2v = _compress_weights(w_cv1, pe_v, w_cv2)
    ckv = _compress(ckv_in.reshape(2, batch, nsub, CMP_STRIDE * KV_WIDTH),
                    jnp.stack([w1k, w1v]), jnp.stack([pek, pev]), jnp.stack([w2k, w2v]))

    ns = t // SEL_BLOCK
    c_cmp, d_cmp, c_blk, c_win, srow = _alibi_tiles(nsub)
    ovt = jnp.asarray(_selection_overlap_t(nsub, ns), BF16)
    oc, sel, cnt = _nsa_a(qt, ckv, jnp.asarray(c_cmp), jnp.asarray(d_cmp), jnp.asarray(srow), ovt)
    idx = _active_blocks(cnt)
    o1 = _nsa_b(idx, qt, kvs, sel, oc, gt, jnp.asarray(c_blk), jnp.asarray(c_win), jnp.asarray(srow))
    mkv1 = _rms_proj(memf, row(g_mem[1]), w_mem_kv[1].astype(BF16), m, BF16).reshape(batch, m, 2 * MEM_WIDTH)
    xf = _attn_out(o1, mq1, mkv1, b_w_out[0].astype(BF16), xf, batch, tm)
    xf = _ffn(xf, row(g_ffn[1]), w_up[1].astype(BF16), conv_w[1], row(conv_b[1]), w_down[1].astype(BF16),
              row(g_final), batch, tm, final_norm=True)
    return xf.reshape(batch, t, d)
```

```python
import functools

import numpy as np
import jax
import jax.numpy as jnp
from jax import lax
from jax.experimental import pallas as pl
from jax.experimental.pallas import tpu as pltpu

F32 = jnp.float32
BF16 = jnp.bfloat16
EPS = 1e-6
NEG_INF = float("-inf")
LOG2E = 1.4426950408889634

V7X_VMEM_BYTES = 64 * 1024 * 1024
VMEM_LIMIT = V7X_VMEM_BYTES - 8 * 1024 * 1024

MEM_HEADS = 4
MEM_DH = 128
MEM_WIDTH = MEM_HEADS * MEM_DH
GLA_HEADS = 4
GLA_DK = 64
GLA_DV = 128
GLA_QK = GLA_HEADS * GLA_DK
GLA_V = GLA_HEADS * GLA_DV
GLA_RANK = 16
GLA_TAU = 16.0
GLA_CHUNK = 64
GLA_LEVELS = 6
NSA_HEADS = 8
NSA_GROUPS = 2
NSA_HPG = NSA_HEADS // NSA_GROUPS
NSA_DH = 64
NSA_WIDTH = NSA_HEADS * NSA_DH
KV_WIDTH = NSA_GROUPS * NSA_DH
CMP_BLOCK = 32
CMP_STRIDE = 16
CMP_HIDDEN = 256
SEL_BLOCK = 64
SEL_TOPN = 16
WINDOW = 512
Q_BLOCK = 128
WIN_KEYS = WINDOW + Q_BLOCK
WIN_CHUNK = WIN_KEYS
CMP_CHUNK = 128
SLC_CHUNK = 6
SLC_WORDS = -(-SLC_CHUNK // 4)
RANK_FIRST_TILE = SEL_TOPN * SEL_BLOCK // Q_BLOCK
SEL_TILES = 4
ATT_TILES = 4
FFN_DIM = 2816
FFN_CHUNK = 256
LANE = 128


def _dot(a, b):
    return jnp.dot(a, b, preferred_element_type=F32)


def _dot_nt(a, b):
    return lax.dot_general(a, b, (((1,), (1,)), ((), ())), preferred_element_type=F32)


def _dot_tn(a, b):
    return lax.dot_general(a, b, (((0,), (0,)), ((), ())), preferred_element_type=F32)


def _params(sem):
    return pltpu.CompilerParams(dimension_semantics=sem, vmem_limit_bytes=VMEM_LIMIT)


def _const_spec(shape):
    n = len(shape)
    return pl.BlockSpec(shape, lambda *_: (0,) * n)


def _normalize(x):
    return x * lax.rsqrt(jnp.mean(x * x, axis=-1, keepdims=True) + EPS)


def _sigmoid(x):
    return 1.0 / (1.0 + jnp.exp(-x))


def _rms_proj_kernel(x_ref, g_ref, w_ref, o_ref):
    h = (_normalize(x_ref[...]) * g_ref[...]).astype(BF16)
    o_ref[...] = _dot(h, w_ref[...]).astype(o_ref.dtype)


def _rms_proj(x, g, w, tm, out_dtype):
    n, d = x.shape
    p = w.shape[1]
    return pl.pallas_call(
        _rms_proj_kernel,
        out_shape=jax.ShapeDtypeStruct((n, p), out_dtype),
        grid=(n // tm,),
        in_specs=[pl.BlockSpec((tm, d), lambda i: (i, 0)), _const_spec((1, d)), _const_spec((d, p))],
        out_specs=pl.BlockSpec((tm, p), lambda i: (i, 0)),
        compiler_params=_params(("parallel",)),
        name="rms_proj",
    )(x, g, w)


def _inproj_a_kernel(x_ref, g_ref, w_ref, wa_ref, ba_ref, q_ref, k_ref, gl_ref, v_ref, r_ref, mq_ref):
    h = (_normalize(x_ref[...]) * g_ref[...]).astype(BF16)
    c = 0
    q_ref[...] = _dot(h, w_ref[:, c:c + GLA_QK]) * (GLA_DK ** -0.5)
    c += GLA_QK
    k_ref[...] = _dot(h, w_ref[:, c:c + GLA_QK])
    c += GLA_QK
    v_ref[...] = _dot(h, w_ref[:, c:c + GLA_V]).astype(BF16)
    c += GLA_V
    r_ref[...] = _dot(h, w_ref[:, c:c + GLA_V])
    c += GLA_V
    mq_ref[...] = _dot(h, w_ref[:, c:c + MEM_WIDTH]).astype(BF16)
    c += MEM_WIDTH
    alr = _dot(h, w_ref[:, c:c + LANE]).astype(BF16)
    z = _dot(alr, wa_ref[...]) + ba_ref[...]
    log_sig = jnp.minimum(z, 0.0) - jnp.log1p(jnp.exp(-jnp.abs(z)))
    gl_ref[...] = log_sig * (1.0 / GLA_TAU)


def _inproj_a(x, g, w, wa, ba, tm):
    n, d = x.shape
    row = lambda i: (i, 0)
    outs = [
        jax.ShapeDtypeStruct((n, GLA_QK), F32), jax.ShapeDtypeStruct((n, GLA_QK), F32),
        jax.ShapeDtypeStruct((n, GLA_QK), F32), jax.ShapeDtypeStruct((n, GLA_V), BF16),
        jax.ShapeDtypeStruct((n, GLA_V), F32), jax.ShapeDtypeStruct((n, MEM_WIDTH), BF16),
    ]
    return pl.pallas_call(
        _inproj_a_kernel,
        out_shape=outs,
        grid=(n // tm,),
        in_specs=[pl.BlockSpec((tm, d), row), _const_spec((1, d)), _const_spec(w.shape),
                  _const_spec(wa.shape), _const_spec(ba.shape)],
        out_specs=[pl.BlockSpec((tm, s.shape[1]), row) for s in outs],
        compiler_params=_params(("parallel",)),
        name="inproj_a",
    )(x, g, w, wa, ba)


def _gla_constants():
    c = GLA_CHUNK
    w = np.zeros(((GLA_LEVELS + 2) * c, c), np.float32)
    masks = np.zeros((GLA_LEVELS + 1, c, c), np.float32)
    masks[0] = np.eye(c)
    for l in range(1, GLA_LEVELS + 1):
        blk, half = 2 ** l, 2 ** (l - 1)
        for i in range(c):
            mid = (i // blk) * blk + half - 1
            if i % blk >= half:
                w[(l - 1) * c + i, mid + 1:i + 1] = 1.0
            else:
                w[(l - 1) * c + i, i + 1:mid + 1] = 1.0
        for t in range(c):
            for s in range(c):
                if t // blk == s // blk and t % blk >= half and s % blk < half:
                    masks[l, t, s] = 1.0
    for i in range(c):
        w[GLA_LEVELS * c + i, :i + 1] = 1.0
        w[(GLA_LEVELS + 1) * c + i, i + 1:] = 1.0
    return np.concatenate([w, w], axis=1), np.tile(masks, (1, 1, GLA_HEADS))


def _gla_kernel(q_ref, k_ref, gl_ref, v_ref, r_ref, gh_ref, wcat_ref, msk_ref, o_ref, st_ref, *, n_chunks):
    c = GLA_CHUNK

    @pl.when(pl.program_id(1) == 0)
    def _():
        st_ref[...] = jnp.zeros_like(st_ref)

    lane_qk = lax.broadcasted_iota(jnp.int32, (1, GLA_QK), 1) // GLA_DK
    lane_v = lax.broadcasted_iota(jnp.int32, (1, GLA_V), 1) // GLA_DV
    row_qk = lax.broadcasted_iota(jnp.int32, (GLA_QK, 1), 0) // GLA_DK
    ones = jnp.ones((2 * c, LANE), BF16)
    wcat = wcat_ref[...]

    def stack_heads(x, lane_head):
        return jnp.concatenate([jnp.where(lane_head == h, x, jnp.zeros_like(x)) for h in range(GLA_HEADS)], axis=0)

    chunks = [slice(ci * c, (ci + 1) * c) for ci in range(n_chunks)]
    es, decs = [], []
    for rows in chunks:
        g = gl_ref[rows, :] * LOG2E
        g_hi = g.astype(BF16)
        g_split = jnp.concatenate([g_hi, (g - g_hi.astype(F32)).astype(BF16)], axis=0)
        es.append(jnp.exp2(_dot(wcat, g_split)))
        decs.append(jnp.exp2(_dot_tn(g_split, ones)))

    o_intras, q_ins, upds = [], [], []
    for rows, e in zip(chunks, es):
        q = q_ref[rows, :]
        k = k_ref[rows, :]
        v = v_ref[rows, :]
        attn = jnp.zeros((c, GLA_HEADS * c), F32)
        for l in range(GLA_LEVELS + 1):
            if l == 0:
                ql, kl = q, k
            else:
                el = e[(l - 1) * c:l * c]
                ql, kl = q * el, k * el
            attn = attn + _dot_nt(ql.astype(BF16), stack_heads(kl, lane_qk).astype(BF16)) * msk_ref[l]
        o_intras.append(_dot(attn.astype(BF16), stack_heads(v, lane_v)))
        q_ins.append((q * e[GLA_LEVELS * c:(GLA_LEVELS + 1) * c]).astype(BF16))
        kk = stack_heads(k * e[(GLA_LEVELS + 1) * c:(GLA_LEVELS + 2) * c], lane_qk).astype(BF16)
        v_rows = jnp.concatenate([v[:, h * GLA_DV:(h + 1) * GLA_DV] for h in range(GLA_HEADS)], axis=0)
        upds.append(_dot_tn(kk, v_rows))

    st = st_ref[...]
    outs = []
    for o_intra, q_in, dec, upd in zip(o_intras, q_ins, decs, upds):
        st_bd = jnp.concatenate([jnp.where(row_qk == h, st, 0.0).astype(BF16) for h in range(GLA_HEADS)], axis=1)
        outs.append(o_intra + _dot(q_in, st_bd))
        st = dec * st + upd
    st_ref[...] = st

    for rows, o in zip(chunks, outs):
        for h in range(GLA_HEADS):
            cs = slice(h * GLA_DV, (h + 1) * GLA_DV)
            on = _normalize(o[:, cs]) * gh_ref[...]
            r = r_ref[rows, cs]
            o_ref[rows, cs] = (on * (r * _sigmoid(r))).astype(BF16)


def _gla(q, k, gl, v, r, g_head, batch, ct):
    n = q.shape[0]
    t = n // batch
    nt = t // ct
    wcat, masks = _gla_constants()
    row = lambda b, i: (b * nt + i, 0)
    return pl.pallas_call(
        functools.partial(_gla_kernel, n_chunks=ct // GLA_CHUNK),
        out_shape=jax.ShapeDtypeStruct((n, GLA_V), BF16),
        grid=(batch, nt),
        in_specs=[pl.BlockSpec((ct, GLA_QK), row), pl.BlockSpec((ct, GLA_QK), row),
                  pl.BlockSpec((ct, GLA_QK), row), pl.BlockSpec((ct, GLA_V), row),
                  pl.BlockSpec((ct, GLA_V), row), _const_spec((1, GLA_DV)),
                  _const_spec(wcat.shape), _const_spec(masks.shape)],
        out_specs=pl.BlockSpec((ct, GLA_V), row),
        scratch_shapes=[pltpu.VMEM((GLA_QK, GLA_DV), F32)],
        compiler_params=_params(("parallel", "arbitrary")),
        name="gla",
    )(q, k, gl, v, r, g_head, jnp.asarray(wcat, BF16), jnp.asarray(masks, F32))


def _attn_out_kernel(o_ref, mq_ref, mk_ref, mv_ref, w_ref, x_ref, out_ref):
    heads = [slice(h * MEM_DH, (h + 1) * MEM_DH) for h in range(MEM_HEADS)]
    scores = [_dot_nt(mq_ref[:, cs], mk_ref[0, :, cs]) * (MEM_DH ** -0.5 * LOG2E) for cs in heads]
    parts = [o_ref[...]]
    for s, cs in zip(scores, heads):
        e = jnp.exp2(s - jnp.max(s, axis=-1, keepdims=True))
        p = e / jnp.sum(e, axis=-1, keepdims=True)
        parts.append(_dot(p.astype(BF16), mv_ref[0, :, cs]).astype(BF16))
    cat = jnp.concatenate(parts, axis=-1)
    out_ref[...] = x_ref[...] + _dot(cat, w_ref[...])


def _attn_out(o, mq, mkv, w_out, x, batch, tm):
    n, d = x.shape
    t = n // batch
    nt = t // tm
    m = mkv.shape[1]
    row = lambda b, i: (b * nt + i, 0)
    return pl.pallas_call(
        _attn_out_kernel,
        out_shape=jax.ShapeDtypeStruct((n, d), F32),
        grid=(batch, nt),
        in_specs=[pl.BlockSpec((tm, o.shape[1]), row), pl.BlockSpec((tm, MEM_WIDTH), row),
                  pl.BlockSpec((1, m, MEM_WIDTH), lambda b, i: (b, 0, 0)),
                  pl.BlockSpec((1, m, MEM_WIDTH), lambda b, i: (b, 0, 1)),
                  _const_spec(w_out.shape), pl.BlockSpec((tm, d), row)],
        out_specs=pl.BlockSpec((tm, d), row),
        compiler_params=_params(("parallel", "parallel")),
        name="attn_out",
    )(o, mq, mkv, mkv, w_out, x)


def _ffn_kernel(x_ref, g_ref, wup_ref, cw_ref, cb_ref, wd_ref, gf_ref, out_ref, act_ref, tail_ref, *,
                final_norm):
    @pl.when(pl.program_id(1) == 0)
    def _():
        tail_ref[...] = jnp.zeros_like(tail_ref)

    x = x_ref[...]
    tm = x.shape[0]
    h = (_normalize(x) * g_ref[...]).astype(BF16)
    rid = lax.broadcasted_iota(jnp.int32, (8, FFN_CHUNK), 0)
    for j in range(FFN_DIM // FFN_CHUNK):
        cs = slice(j * FFN_CHUNK, (j + 1) * FFN_CHUNK)
        a = _dot(h, wup_ref[:, cs])
        b = _dot(h, wup_ref[:, FFN_DIM + j * FFN_CHUNK:FFN_DIM + (j + 1) * FFN_CHUNK])
        tail = tail_ref[j]
        r1 = pltpu.roll(a, 1, 0)
        r2 = pltpu.roll(a, 2, 0)
        top1 = jnp.where(rid == 0, tail[7:8], r1[:8])
        top2 = jnp.where(rid == 0, tail[6:7], jnp.where(rid == 1, tail[7:8], r2[:8]))
        a1 = jnp.concatenate([top1, r1[8:]], axis=0)
        a2 = jnp.concatenate([top2, r2[8:]], axis=0)
        tail_ref[j] = a[tm - 8:]
        ac = a2 * cw_ref[0:1, cs] + a1 * cw_ref[1:2, cs] + a * cw_ref[2:3, cs] + cb_ref[:, cs]
        act_ref[:, cs] = (ac * _sigmoid(ac) * b).astype(BF16)
    y = x + _dot(act_ref[...], wd_ref[...])
    if final_norm:
        y = _normalize(y) * gf_ref[...]
    out_ref[...] = y


def _ffn(x, g, w_up, conv_w, conv_b, w_down, g_final, batch, tm, final_norm):
    n, d = x.shape
    nt = n // batch // tm
    row = lambda b, i: (b * nt + i, 0)
    once = pl.Buffered(1)
    return pl.pallas_call(
        functools.partial(_ffn_kernel, final_norm=final_norm),
        out_shape=jax.ShapeDtypeStruct((n, d), F32),
        grid=(batch, nt),
        in_specs=[pl.BlockSpec((tm, d), row), _const_spec((1, d)),
                  pl.BlockSpec(w_up.shape, lambda b, i: (0, 0), pipeline_mode=once),
                  _const_spec(conv_w.shape), _const_spec(conv_b.shape),
                  pl.BlockSpec(w_down.shape, lambda b, i: (0, 0), pipeline_mode=once),
                  _const_spec((1, d))],
        out_specs=pl.BlockSpec((tm, d), row),
        scratch_shapes=[pltpu.VMEM((tm, FFN_DIM), BF16),
                        pltpu.VMEM((FFN_DIM // FFN_CHUNK, 8, FFN_CHUNK), F32)],
        compiler_params=_params(("parallel", "arbitrary")),
        name="conv_ffn",
    )(x, g, w_up, conv_w, conv_b, w_down, g_final)


def _proj_b_kernel(x_ref, gkv_ref, gmix_ref, wkv_ref, wqt_ref, wgt_ref, wmq_ref,
                   cmp_ref, kvs_ref, qt_ref, gt_ref, mq_ref, regroup_ref):
    xn = _normalize(x_ref[...])
    hkv = (xn * gkv_ref[...]).astype(BF16)
    h1 = (xn * gmix_ref[...]).astype(BF16)
    kv_cmp = _dot(hkv, wkv_ref[:, 0:2 * KV_WIDTH])
    n_sub = x_ref.shape[0] // CMP_STRIDE
    for s in range(2):
        regroup_ref[s] = kv_cmp[:, s * KV_WIDTH:(s + 1) * KV_WIDTH]
        for p in range(CMP_STRIDE):
            cmp_ref[s, :, p * KV_WIDTH:(p + 1) * KV_WIDTH] = \
                regroup_ref[s, pl.ds(p, n_sub, stride=CMP_STRIDE), :].astype(BF16)
    kvs_ref[...] = _dot(hkv, wkv_ref[:, 2 * KV_WIDTH:]).astype(BF16)
    qt_ref[0] = (_dot_nt(wqt_ref[...], h1) * (NSA_DH ** -0.5 * LOG2E)).astype(BF16)
    gt_ref[0] = _sigmoid(_dot_nt(wgt_ref[...], h1))
    mq_ref[...] = _dot(h1, wmq_ref[...]).astype(BF16)


def _proj_b(x, g_kv, g_mix, w_kv, wqt, wgt, wmq, batch, tm):
    n, d = x.shape
    t = n // batch
    nt = t // tm
    row = lambda b, i: (b * nt + i, 0)
    outs = [
        jax.ShapeDtypeStruct((2, n // CMP_STRIDE, CMP_STRIDE * KV_WIDTH), BF16),
        jax.ShapeDtypeStruct((n, 4 * KV_WIDTH), BF16),
        jax.ShapeDtypeStruct((batch, NSA_WIDTH, t), BF16),
        jax.ShapeDtypeStruct((batch, wgt.shape[0], t), F32),
        jax.ShapeDtypeStruct((n, MEM_WIDTH), BF16),
    ]
    return pl.pallas_call(
        _proj_b_kernel,
        out_shape=outs,
        grid=(batch, nt),
        in_specs=[pl.BlockSpec((tm, d), row), _const_spec((1, d)), _const_spec((1, d)),
                  _const_spec(w_kv.shape), _const_spec(wqt.shape), _const_spec(wgt.shape),
                  _const_spec(wmq.shape)],
        out_specs=[pl.BlockSpec((2, tm // CMP_STRIDE, CMP_STRIDE * KV_WIDTH), lambda b, i: (0, b * nt + i, 0)),
                   pl.BlockSpec((tm, 4 * KV_WIDTH), row),
                   pl.BlockSpec((1, NSA_WIDTH, tm), lambda b, i: (b, 0, i)),
                   pl.BlockSpec((1, wgt.shape[0], tm), lambda b, i: (b, 0, i)),
                   pl.BlockSpec((tm, MEM_WIDTH), row)],
        scratch_shapes=[pltpu.VMEM((2, tm, KV_WIDTH), F32)],
        compiler_params=_params(("parallel", "parallel")),
        name="proj_b",
    )(x, g_kv, g_mix, w_kv, wqt, wgt, wmq)


def _compress_kernel(a_ref, w1_ref, pe_ref, w2_ref, o_ref):
    a = a_ref[0, 0]
    w1 = w1_ref[0]
    u = _dot(a, w1)
    cpe = _dot(pe_ref[0], w1)
    nrow = a.shape[0]
    hid = []
    for g in range(NSA_GROUPS):
        c0 = slice((2 * g) * CMP_HIDDEN, (2 * g + 1) * CMP_HIDDEN)
        c1 = slice((2 * g + 1) * CMP_HIDDEN, (2 * g + 2) * CMP_HIDDEN)
        nxt = pltpu.roll(u[:, c1], nrow - 1, 0)
        hid.append(u[:, c0] + nxt + cpe[0:1, c0] + cpe[8:9, c1])
    hcat = jax.nn.gelu(jnp.concatenate(hid, axis=-1), approximate=True).astype(BF16)
    o_ref[0, 0] = _dot(hcat, w2_ref[0]).astype(BF16)


def _compress(a, w1, pe, w2):
    _, batch, nsub, width = a.shape
    return pl.pallas_call(
        _compress_kernel,
        out_shape=jax.ShapeDtypeStruct((2, batch, nsub, KV_WIDTH), BF16),
        grid=(2, batch),
        in_specs=[pl.BlockSpec((1, 1, nsub, width), lambda s, b: (s, b, 0, 0)),
                  pl.BlockSpec((1,) + w1.shape[1:], lambda s, b: (s, 0, 0)),
                  pl.BlockSpec((1,) + pe.shape[1:], lambda s, b: (s, 0, 0)),
                  pl.BlockSpec((1,) + w2.shape[1:], lambda s, b: (s, 0, 0))],
        out_specs=pl.BlockSpec((1, 1, nsub, KV_WIDTH), lambda s, b: (s, b, 0, 0)),
        compiler_params=_params(("parallel", "parallel")),
        name="compress",
    )(a, w1, pe, w2)


def _alibi_tiles(ncmp_pad):
    slopes = LOG2E * 2.0 ** (-8.0 * np.arange(1, NSA_HEADS + 1) / NSA_HEADS)
    tl = np.tile(np.arange(Q_BLOCK), NSA_HPG)[None, :]
    cmp_end = (np.arange(ncmp_pad) * CMP_STRIDE + CMP_BLOCK - 1)[:, None]
    key = np.arange(SEL_BLOCK)[:, None]
    dist_win = tl + WINDOW - np.arange(WIN_KEYS)[:, None]
    c_cmp = np.zeros((NSA_GROUPS, ncmp_pad, NSA_HPG * Q_BLOCK), np.float32)
    c_blk = np.zeros((NSA_GROUPS, SEL_BLOCK, NSA_HPG * Q_BLOCK), np.float32)
    c_win = np.zeros((NSA_GROUPS, WIN_KEYS, NSA_HPG * Q_BLOCK), np.float32)
    srow = np.zeros((NSA_GROUPS, 1, NSA_HPG * Q_BLOCK), np.float32)
    for g in range(NSA_GROUPS):
        s = np.repeat(slopes[g * NSA_HPG:(g + 1) * NSA_HPG], Q_BLOCK)[None, :]
        c_cmp[g] = -s * (tl - cmp_end)
        c_blk[g] = -s * (tl - key)
        c_win[g] = np.where((dist_win >= 0) & (dist_win < WINDOW), -s * dist_win, -np.inf)
        srow[g] = s
    d_cmp = (cmp_end - tl).astype(np.float32)
    return c_cmp, d_cmp, c_blk, c_win, srow


def _group_queries(qt_ref, g, tile=0):
    zeros = jnp.zeros((NSA_DH, Q_BLOCK), BF16)
    cols = []
    for hh in range(NSA_HPG):
        h = g * NSA_HPG + hh
        piece = qt_ref[0, h * NSA_DH:(h + 1) * NSA_DH, tile * Q_BLOCK:(tile + 1) * Q_BLOCK]
        cols.append(jnp.concatenate([piece, zeros] if g == 0 else [zeros, piece], axis=0))
    return jnp.concatenate(cols, axis=1)


def _nsa_a_kernel(qt_ref, ckv_ref, ccmp_ref, dcmp_ref, srow_ref, ovt_ref, oc_ref, sel_ref, cnt_ref, imp_ref):
    first_tile = pl.program_id(1) * SEL_TILES
    nblk = sel_ref.shape[3]
    jidx = lax.broadcasted_iota(jnp.int32, (nblk, Q_BLOCK), 0)
    tl = lax.broadcasted_iota(jnp.int32, (nblk, Q_BLOCK), 1)
    units = [(k, g) for k in range(SEL_TILES) for g in range(NSA_GROUPS)]
    qbs = [first_tile + k for k in range(SEL_TILES)]
    bases = [(qb * Q_BLOCK).astype(F32) for qb in qbs]
    curs = [2 * qb + jnp.where(tl >= SEL_BLOCK, 1, 0) for qb in qbs]
    n_chunks = (qbs[-1] * (Q_BLOCK // CMP_STRIDE) + (Q_BLOCK - CMP_BLOCK) // CMP_STRIDE) // CMP_CHUNK + 1
    qts = {(k, g): _group_queries(qt_ref, g, k) for k, g in units}
    shifts = {(k, g): srow_ref[g] * bases[k] for k, g in units}

    def cmp_branch(n_rows):
        kc = ckv_ref[0, 0, 0:n_rows, :]
        vc = ckv_ref[1, 0, 0:n_rows, :]
        ov = ovt_ref[:, 0:n_rows]
        lane_grp = lax.broadcasted_iota(jnp.int32, vc.shape, 1) // NSA_DH
        raws = {u: _dot(kc, qts[u]) for u in units}
        for k, g in units:
            visible = dcmp_ref[0:n_rows, :] <= bases[k]
            s = raws[k, g] + jnp.where(visible, ccmp_ref[g, 0:n_rows, :] - shifts[k, g], NEG_INF)
            m = jnp.max(s, axis=0, keepdims=True)
            e = jnp.exp2((s - jnp.where(m == NEG_INF, 0.0, m)).astype(BF16))
            pv = _dot_tn(jnp.where(lane_grp == g, vc, jnp.ones_like(vc)), e)
            inv = 1.0 / jnp.maximum(pv[(1 - g) * NSA_DH:(1 - g) * NSA_DH + 1], 1e-30)
            oc = pv[g * NSA_DH:(g + 1) * NSA_DH] * inv
            imp = _dot(ov, e) * inv
            for hh in range(NSA_HPG):
                h = g * NSA_HPG + hh
                oc_ref[0, h * NSA_DH:(h + 1) * NSA_DH, k * Q_BLOCK:(k + 1) * Q_BLOCK] = \
                    oc[:, hh * Q_BLOCK:(hh + 1) * Q_BLOCK]
            imp_g = imp[:, 0:Q_BLOCK]
            for hh in range(1, NSA_HPG):
                imp_g = imp_g + imp[:, hh * Q_BLOCK:(hh + 1) * Q_BLOCK]
            imp_ref[k, g] = imp_g

    for nc in range(1, ckv_ref.shape[2] // CMP_CHUNK + 1):
        pl.when(n_chunks == nc)(functools.partial(cmp_branch, nc * CMP_CHUNK))

    unit_row = lax.broadcasted_iota(jnp.int32, (len(units), nblk), 0)

    def emit(sels):
        counts = jnp.zeros((len(units), nblk), F32)
        for n, (k, g) in enumerate(units):
            sel_ref[0, g, k] = sels[k, g]
            takers = _dot_nt(jnp.ones((len(units), Q_BLOCK), BF16), sels[k, g].astype(BF16))
            counts = jnp.where(unit_row == n, takers, counts)
        cnt_ref[0, 0] = counts

    @pl.when(first_tile < RANK_FIRST_TILE)
    def _():
        emit({(k, g): jnp.where(jidx <= curs[k], 1.0, 0.0) for k, g in units})

    def rank_branch(rows):
        jr = lax.broadcasted_iota(jnp.int32, (rows, Q_BLOCK), 0)
        jf = jr.astype(F32)
        cur_r = [2 * qb + jnp.where(lax.broadcasted_iota(jnp.int32, (rows, Q_BLOCK), 1) >= SEL_BLOCK, 1, 0)
                 for qb in qbs]
        v0 = {(k, g): jnp.where((jr >= 1) & (jr <= cur_r[k] - 2), imp_ref[k, g, 0:rows, :], NEG_INF)
              for k, g in units}
        v = dict(v0)
        for _ in range(SEL_TOPN - 3):
            for u in units:
                mx = jnp.max(v[u], axis=0, keepdims=True)
                first = jnp.min(jnp.where(v[u] == mx, jf, float(nblk)), axis=0, keepdims=True)
                v[u] = jnp.where(jf == first, NEG_INF, v[u])
        sels = {}
        for k, g in units:
            cur = cur_r[k]
            taken = (v[k, g] != v0[k, g]) | (jr == 0) | (jr == cur) | (jr == cur - 1)
            sel = jnp.where(taken, 1.0, 0.0)
            if rows < nblk:
                sel = jnp.concatenate([sel, jnp.zeros((nblk - rows, Q_BLOCK), F32)], axis=0)
            sels[k, g] = sel
        emit(sels)

    @pl.when(first_tile >= RANK_FIRST_TILE)
    def _():
        blocks_per_variant = CMP_CHUNK * CMP_STRIDE // SEL_BLOCK
        for nc in range(1, ckv_ref.shape[2] // CMP_CHUNK + 1):
            pl.when(n_chunks == nc)(functools.partial(rank_branch, min(nblk, nc * blocks_per_variant)))


def _nsa_a(qt, ckv, c_cmp, d_cmp, srow, ovt):
    batch, _, t = qt.shape
    nqb = t // Q_BLOCK
    ns = t // SEL_BLOCK
    ncmp = ckv.shape[2]
    outs = [jax.ShapeDtypeStruct((batch, NSA_WIDTH, t), F32),
            jax.ShapeDtypeStruct((batch, NSA_GROUPS, nqb, ns, Q_BLOCK), F32),
            jax.ShapeDtypeStruct((batch, nqb // SEL_TILES, SEL_TILES * NSA_GROUPS, ns), F32)]
    assert nqb % SEL_TILES == 0 and RANK_FIRST_TILE % SEL_TILES == 0
    assert (SEL_TILES * NSA_GROUPS) % 8 == 0
    return pl.pallas_call(
        _nsa_a_kernel,
        out_shape=outs,
        grid=(batch, nqb // SEL_TILES),
        in_specs=[pl.BlockSpec((1, NSA_WIDTH, SEL_TILES * Q_BLOCK), lambda b, i: (b, 0, i)),
                  pl.BlockSpec((2, 1, ncmp, KV_WIDTH), lambda b, i: (0, b, 0, 0)),
                  _const_spec(c_cmp.shape), _const_spec(d_cmp.shape), _const_spec(srow.shape),
                  _const_spec(ovt.shape)],
        out_specs=[pl.BlockSpec((1, NSA_WIDTH, SEL_TILES * Q_BLOCK), lambda b, i: (b, 0, i)),
                   pl.BlockSpec((1, NSA_GROUPS, SEL_TILES, ns, Q_BLOCK), lambda b, i: (b, 0, i, 0, 0)),
                   pl.BlockSpec((1, 1, SEL_TILES * NSA_GROUPS, ns), lambda b, i: (b, i, 0, 0))],
        scratch_shapes=[pltpu.VMEM((SEL_TILES, NSA_GROUPS, ns, Q_BLOCK), F32)],
        compiler_params=_params(("parallel", "parallel")),
        name="nsa_select",
    )(qt, ckv, c_cmp, d_cmp, srow, ovt)


def _block_softmax_step(carry, raw, consts, shifts, v_rows, g):
    m, l, acc = carry
    width = raw.shape[1]
    fold = lambda x: x.reshape(SEL_BLOCK // 8, 8, width)
    us = [raw[i * SEL_BLOCK:(i + 1) * SEL_BLOCK] + c for i, c in enumerate(consts)]
    top8 = None
    for u, sh in zip(us, shifts):
        t = jnp.max(fold(u), axis=0) + sh
        top8 = t if top8 is None else jnp.maximum(top8, t)
    m_new = jnp.maximum(m, jnp.max(top8, axis=0, keepdims=True))
    m_safe = jnp.where(m_new == NEG_INF, 0.0, m_new)
    alpha = jnp.where(m == NEG_INF, 0.0, jnp.exp2(m - m_safe))
    ps = [jnp.exp2((u - (m_safe - sh)).astype(BF16)) for u, sh in zip(us, shifts)]
    lane_grp = lax.broadcasted_iota(jnp.int32, v_rows.shape, 1) // NSA_DH
    v_aug = jnp.where(lane_grp == g, v_rows, jnp.ones_like(v_rows))
    pv = _dot_tn(v_aug, jnp.concatenate(ps, axis=0))
    og = 1 - g
    return m_new, alpha * l + pv[og * NSA_DH:og * NSA_DH + 1], alpha * acc + pv[g * NSA_DH:(g + 1) * NSA_DH]


def _nsa_b_kernel(idx_ref, qt_ref, ks_ref, vs_ref, kw_ref, vw_ref, sel_ref, oc_ref, gt_ref, cblk_ref, cwin_ref,
                  srow_ref, o_ref, *, idx_words):
    b = pl.program_id(0)
    first_tile = pl.program_id(1) * ATT_TILES
    width = NSA_HPG * Q_BLOCK
    units = [(k, g) for k in range(ATT_TILES) for g in range(NSA_GROUPS)]
    qbs = [first_tile + k for k in range(ATT_TILES)]
    bases = [qb * Q_BLOCK for qb in qbs]
    qts = {(k, g): _group_queries(qt_ref, g, k) for k, g in units}
    srows = [srow_ref[g] for g in range(NSA_GROUPS)]
    n_tiles = pl.num_programs(1) * ATT_TILES
    entries = {(k, g): ((b * NSA_GROUPS + g) * n_tiles + qbs[k]) * idx_words for k, g in units}
    n_act = {u: idx_ref[entries[u]] for u in units}
    init = (jnp.full((1, width), NEG_INF, F32), jnp.zeros((1, width), F32), jnp.zeros((NSA_DH, width), F32))
    zero_row = jnp.zeros((1, width), F32)

    def gate_row(k, g, j, live):
        picked = sel_ref[0, g, k, pl.ds(j, 1), :]
        gate = jnp.where((picked > 0.0) & live, 0.0, NEG_INF)
        return jnp.concatenate([gate] * NSA_HPG, axis=1)

    def stage(it):
        out = {}
        for k, g in units:
            words = [idx_ref[entries[k, g] + 1 + it * SLC_WORDS + w] for w in range(SLC_WORDS)]
            kts, vts, shifts = [], [], []
            for i in range(SLC_CHUNK):
                live = it * SLC_CHUNK + i < n_act[k, g]
                j = jnp.where(live, lax.shift_right_logical(words[i // 4], 8 * (i % 4)) & 0xFF, 0)
                k0 = pl.multiple_of(j * SEL_BLOCK, SEL_BLOCK)
                kts.append(ks_ref[pl.ds(k0, SEL_BLOCK), :])
                vts.append(vs_ref[pl.ds(k0, SEL_BLOCK), :])
                shifts.append(gate_row(k, g, j, live) - srows[g] * (bases[k] - k0).astype(F32))
            out[k, g] = (_dot(jnp.concatenate(kts, axis=0), qts[k, g]), shifts, jnp.concatenate(vts, axis=0))
        return out

    def body(it, states):
        staged = stage(it)
        return tuple(_block_softmax_step(state, staged[u][0], [cblk_ref[u[1]]] * SLC_CHUNK, staged[u][1],
                                         staged[u][2], u[1]) for u, state in zip(units, states))

    longest = functools.reduce(jnp.maximum, [n_act[u] for u in units])
    far = lax.fori_loop(0, (longest + SLC_CHUNK - 1) // SLC_CHUNK, body, (init,) * len(units))
    slc = dict(zip(units, far))

    diag = range(WINDOW // SEL_BLOCK, WIN_KEYS // SEL_BLOCK)
    win = {u: init for u in units}
    steps = []
    for k, g in units:
        r0 = pl.multiple_of(bases[k], Q_BLOCK)
        shifts = [gate_row(k, g, 2 * qbs[k] + i, True) for i in range(len(diag))]
        steps.append((slc, (k, g), ks_ref[pl.ds(r0, Q_BLOCK), :], vs_ref[pl.ds(r0, Q_BLOCK), :], diag, shifts))
    for k in range(ATT_TILES):
        for c in range(WIN_KEYS // WIN_CHUNK):
            blocks = range(c * (WIN_CHUNK // SEL_BLOCK), (c + 1) * (WIN_CHUNK // SEL_BLOCK))
            kws, vws, offs = [], [], []
            for r in blocks:
                j = 2 * qbs[k] - WINDOW // SEL_BLOCK + r
                k0 = pl.multiple_of(jnp.maximum(j, 0) * SEL_BLOCK, SEL_BLOCK)
                kws.append(kw_ref[pl.ds(k0, SEL_BLOCK), :])
                vws.append(vw_ref[pl.ds(k0, SEL_BLOCK), :])
                offs.append(zero_row + jnp.where(j >= 0, 0.0, NEG_INF))
            kw = jnp.concatenate(kws, axis=0)
            vw = jnp.concatenate(vws, axis=0)
            for g in range(NSA_GROUPS):
                steps.append((win, (k, g), kw, vw, blocks, offs))
    raws = [_dot(keys, qts[u]) for _, u, keys, _, _, _ in steps]
    for raw, (states, u, _, values, blocks, shifts) in zip(raws, steps):
        consts = [cwin_ref[u[1], r * SEL_BLOCK:(r + 1) * SEL_BLOCK, :] for r in blocks]
        states[u] = _block_softmax_step(states[u], raw, consts, shifts, values, u[1])

    for k in range(ATT_TILES):
        cols = slice(k * Q_BLOCK, (k + 1) * Q_BLOCK)
        heads_out = []
        for g in range(NSA_GROUPS):
            o_s = slc[k, g][2] * (1.0 / jnp.maximum(slc[k, g][1], 1e-30))
            o_w = win[k, g][2] * (1.0 / jnp.maximum(win[k, g][1], 1e-30))
            for hh in range(NSA_HPG):
                h = g * NSA_HPG + hh
                cs = slice(hh * Q_BLOCK, (hh + 1) * Q_BLOCK)
                gates = gt_ref[0, 3 * h:3 * h + 3, cols]
                heads_out.append(gates[0:1] * oc_ref[0, h * NSA_DH:(h + 1) * NSA_DH, cols]
                                 + gates[1:2] * o_s[:, cs] + gates[2:3] * o_w[:, cs])
        o_ref[cols, :] = jnp.concatenate(heads_out, axis=0).astype(BF16).T


def _nsa_b(idx, qt, kvs, sel, oc, gt, c_blk, c_win, srow):
    batch, _, t = qt.shape
    nqb = t // Q_BLOCK
    ns = t // SEL_BLOCK
    kv_spec = lambda c: pl.BlockSpec((t, KV_WIDTH), lambda b, i, idx_ref: (b, c))
    grid_spec = pltpu.PrefetchScalarGridSpec(
        num_scalar_prefetch=1,
        grid=(batch, nqb // ATT_TILES),
        in_specs=[pl.BlockSpec((1, NSA_WIDTH, ATT_TILES * Q_BLOCK), lambda b, i, r: (b, 0, i)),
                  kv_spec(0), kv_spec(1), kv_spec(2), kv_spec(3),
                  pl.BlockSpec((1, NSA_GROUPS, ATT_TILES, ns, Q_BLOCK), lambda b, i, r: (b, 0, i, 0, 0)),
                  pl.BlockSpec((1, NSA_WIDTH, ATT_TILES * Q_BLOCK), lambda b, i, r: (b, 0, i)),
                  pl.BlockSpec((1, gt.shape[1], ATT_TILES * Q_BLOCK), lambda b, i, r: (b, 0, i)),
                  pl.BlockSpec(c_blk.shape, lambda b, i, r: (0, 0, 0)),
                  pl.BlockSpec(c_win.shape, lambda b, i, r: (0, 0, 0)),
                  pl.BlockSpec(srow.shape, lambda b, i, r: (0, 0, 0))],
        out_specs=pl.BlockSpec((ATT_TILES * Q_BLOCK, NSA_WIDTH), lambda b, i, r: (b * (nqb // ATT_TILES) + i, 0)),
    )
    assert nqb % ATT_TILES == 0
    return pl.pallas_call(
        functools.partial(_nsa_b_kernel, idx_words=1 + _slc_steps(ns) * SLC_WORDS),
        out_shape=jax.ShapeDtypeStruct((batch * t, NSA_WIDTH), BF16),
        grid_spec=grid_spec,
        compiler_params=_params(("parallel", "parallel")),
        name="nsa_attend",
    )(idx, qt, kvs, kvs, kvs, kvs, sel, oc, gt, c_blk, c_win, srow)


def _pad_cols(w, width):
    return jnp.pad(w, ((0, 0), (0, width - w.shape[1])))


def _compress_weights(w1, pe, w2):
    nsub = CMP_STRIDE
    w1r = w1.reshape(2, nsub, NSA_DH, CMP_HIDDEN)
    same_group = jnp.eye(NSA_GROUPS, dtype=F32)
    big = same_group[None, :, None, :, None, None] * w1r.transpose(1, 2, 0, 3)[:, None, :, None, :, :]
    big = big.reshape(nsub * KV_WIDTH, NSA_GROUPS * 2 * CMP_HIDDEN)
    per = pe.reshape(2, nsub, 1, NSA_DH)
    pe2 = jnp.broadcast_to(per, (2, nsub, NSA_GROUPS, NSA_DH)).reshape(2, 1, nsub * KV_WIDTH)
    pe2 = jnp.broadcast_to(pe2, (2, 8, nsub * KV_WIDTH)).reshape(16, nsub * KV_WIDTH)
    w2bd = same_group[:, None, :, None] * w2[None, :, None, :]
    return big.astype(BF16), pe2.astype(BF16), w2bd.reshape(NSA_GROUPS * CMP_HIDDEN, KV_WIDTH).astype(BF16)


def _selection_overlap_t(ncmp_pad, ns):
    cs = np.arange(ncmp_pad) * CMP_STRIDE
    ss = np.arange(ns) * SEL_BLOCK
    ov = np.minimum(cs[:, None] + CMP_BLOCK, ss[None, :] + SEL_BLOCK) - np.maximum(cs[:, None], ss[None, :])
    return (np.clip(ov, 0, None).astype(np.float32) / CMP_BLOCK).T


def _slc_steps(ns):
    return -(-ns // SLC_CHUNK)


def _active_blocks(cnt):
    batch, n_steps, _, ns = cnt.shape
    cnt = cnt.reshape(batch, n_steps, SEL_TILES, NSA_GROUPS, ns).transpose(0, 3, 1, 2, 4)
    cnt = cnt.reshape(batch, NSA_GROUPS, n_steps * SEL_TILES, ns)
    before_tile = jnp.arange(ns)[None, :] < (Q_BLOCK // SEL_BLOCK) * jnp.arange(cnt.shape[2])[:, None]
    flags = (cnt > 0.0) & before_tile
    blocks = jnp.arange(ns, dtype=jnp.int32)
    order = jnp.sort(jnp.where(flags, blocks, blocks + ns), axis=-1) % ns
    n_act = jnp.sum(flags, axis=-1, dtype=jnp.int32)
    lead = order.shape[:-1]
    steps = _slc_steps(ns)
    order = jnp.pad(order, [(0, 0)] * len(lead) + [(0, steps * SLC_CHUNK - ns)]).reshape(lead + (steps, SLC_CHUNK))
    order = jnp.pad(order, [(0, 0)] * (len(lead) + 1) + [(0, 4 * SLC_WORDS - SLC_CHUNK)])
    packed = jnp.sum(order.reshape(lead + (steps * SLC_WORDS, 4)) << (8 * jnp.arange(4, dtype=jnp.int32)), axis=-1)
    return jnp.concatenate([n_act[..., None], packed], axis=-1).reshape(-1)


def kernel(x, mem, g_mix, g_ffn, g_mem, w_mem_kv, w_up, conv_w, conv_b, w_down,
           a_w_in, a_w_alpha, a_b_alpha, a_g_head, a_w_out,
           g_kv, w_kv, pe_k, pe_v, w_ck1, w_ck2, w_cv1, w_cv2,
           b_w_in, b_w_out, g_final):
    batch, t, d = x.shape
    n = batch * t
    m = mem.shape[1]
    tm = min(1024, t)
    xf = x.reshape(n, d)
    memf = mem.reshape(batch * m, d)
    row = lambda v: v.reshape(1, -1)

    mkv0 = _rms_proj(memf, row(g_mem[0]), w_mem_kv[0].astype(BF16), m, BF16).reshape(batch, m, 2 * MEM_WIDTH)
    wa = a_w_in[0]
    c_alr = 2 * GLA_QK + 2 * GLA_V
    w_a = jnp.concatenate([wa[:, :c_alr], wa[:, c_alr + GLA_RANK:], _pad_cols(wa[:, c_alr:c_alr + GLA_RANK], LANE)],
                          axis=1).astype(BF16)
    w_alpha = jnp.pad(a_w_alpha[0], ((0, LANE - GLA_RANK), (0, 0))).astype(BF16)
    q, k, gl, v, r, mq = _inproj_a(xf, row(g_mix[0]), w_a, w_alpha, row(a_b_alpha[0]), tm)
    o = _gla(q, k, gl, v, r, row(a_g_head[0]), batch, tm)
    xf = _attn_out(o, mq, mkv0, a_w_out[0].astype(BF16), xf, batch, tm)
    xf = _ffn(xf, row(g_ffn[0]), w_up[0].astype(BF16), conv_w[0], row(conv_b[0]), w_down[0].astype(BF16),
              row(g_final), batch, tm, final_norm=False)

    wb = b_w_in[0]
    wqt = wb[:, :NSA_WIDTH].T.astype(BF16)
    n_gate = 3 * NSA_HEADS
    wgt = jnp.pad(wb[:, NSA_WIDTH:NSA_WIDTH + n_gate].T, ((0, 32 - n_gate), (0, 0))).astype(BF16)
    wmq = wb[:, NSA_WIDTH + n_gate:].astype(BF16)
    ckv_in, kvs, qt, gt, mq1 = _proj_b(xf, row(g_kv), row(g_mix[1]), w_kv.astype(BF16), wqt, wgt, wmq, batch, tm)
    nsub = t // CMP_STRIDE
    w1k, pek, w2k = _compress_weights(w_ck1, pe_k, w_ck2)
    w1v, pev, w2v = _compress_weights(w_cv1, pe_v, w_cv2)
    ckv = _compress(ckv_in.reshape(2, batch, nsub, CMP_STRIDE * KV_WIDTH),
                    jnp.stack([w1k, w1v]), jnp.stack([pek, pev]), jnp.stack([w2k, w2v]))

    ns = t // SEL_BLOCK
    c_cmp, d_cmp, c_blk, c_win, srow = _alibi_tiles(nsub)
    ovt = jnp.asarray(_selection_overlap_t(nsub, ns), BF16)
    oc, sel, cnt = _nsa_a(qt, ckv, jnp.asarray(c_cmp), jnp.asarray(d_cmp), jnp.asarray(srow), ovt)
    idx = _active_blocks(cnt)
    o1 = _nsa_b(idx, qt, kvs, sel, oc, gt, jnp.asarray(c_blk), jnp.asarray(c_win), jnp.asarray(srow))
    mkv1 = _rms_proj(memf, row(g_mem[1]), w_mem_kv[1].astype(BF16), m, BF16).reshape(batch, m, 2 * MEM_WIDTH)
    xf = _attn_out(o1, mq1, mkv1, b_w_out[0].astype(BF16), xf, batch, tm)
    xf = _ffn(xf, row(g_ffn[1]), w_up[1].astype(BF16), conv_w[1], row(conv_b[1]), w_down[1].astype(BF16),
              row(g_final), batch, tm, final_norm=True)
    return xf.reshape(batch, t, d)
```

```python
import functools

import numpy as np
import jax
import jax.numpy as jnp
from jax import lax
from jax.experimental import pallas as pl
from jax.experimental.pallas import tpu as pltpu

F32 = jnp.float32
BF16 = jnp.bfloat16
EPS = 1e-6
NEG_INF = float("-inf")
LOG2E = 1.4426950408889634

V7X_VMEM_BYTES = 64 * 1024 * 1024
VMEM_LIMIT = V7X_VMEM_BYTES - 8 * 1024 * 1024

MEM_HEADS = 4
MEM_DH = 128
MEM_WIDTH = MEM_HEADS * MEM_DH
GLA_HEADS = 4
GLA_DK = 64
GLA_DV = 128
GLA_QK = GLA_HEADS * GLA_DK
GLA_V = GLA_HEADS * GLA_DV
GLA_RANK = 16
GLA_TAU = 16.0
GLA_CHUNK = 64
GLA_LEVELS = 6
NSA_HEADS = 8
NSA_GROUPS = 2
NSA_HPG = NSA_HEADS // NSA_GROUPS
NSA_DH = 64
NSA_WIDTH = NSA_HEADS * NSA_DH
KV_WIDTH = NSA_GROUPS * NSA_DH
CMP_BLOCK = 32
CMP_STRIDE = 16
CMP_HIDDEN = 256
SEL_BLOCK = 64
SEL_TOPN = 16
WINDOW = 512
Q_BLOCK = 128
WIN_KEYS = WINDOW + Q_BLOCK
WIN_CHUNK = WIN_KEYS
CMP_CHUNK = 128
SLC_CHUNK = 6
SLC_WORDS = -(-SLC_CHUNK // 4)
RANK_FIRST_TILE = SEL_TOPN * SEL_BLOCK // Q_BLOCK
SEL_TILES = 4
ATT_TILES = 4
FFN_DIM = 2816
FFN_CHUNK = 256
LANE = 128


def _dot(a, b):
    return jnp.dot(a, b, preferred_element_type=F32)


def _dot_nt(a, b):
    return lax.dot_general(a, b, (((1,), (1,)), ((), ())), preferred_element_type=F32)


def _dot_tn(a, b):
    return lax.dot_general(a, b, (((0,), (0,)), ((), ())), preferred_element_type=F32)


def _params(sem):
    return pltpu.CompilerParams(dimension_semantics=sem, vmem_limit_bytes=VMEM_LIMIT)


def _const_spec(shape):
    n = len(shape)
    return pl.BlockSpec(shape, lambda *_: (0,) * n)


def _normalize(x):
    return x * lax.rsqrt(jnp.mean(x * x, axis=-1, keepdims=True) + EPS)


def _sigmoid(x):
    return 1.0 / (1.0 + jnp.exp(-x))


def _inproj_a_kernel(x_ref, g_ref, w_ref, wa_ref, ba_ref, q_ref, k_ref, gl_ref, v_ref, r_ref, mq_ref):
    h = (_normalize(x_ref[...]) * g_ref[...]).astype(BF16)
    c = 0
    q_ref[...] = _dot(h, w_ref[:, c:c + GLA_QK]) * (GLA_DK ** -0.5)
    c += GLA_QK
    k_ref[...] = _dot(h, w_ref[:, c:c + GLA_QK])
    c += GLA_QK
    v_ref[...] = _dot(h, w_ref[:, c:c + GLA_V]).astype(BF16)
    c += GLA_V
    r_ref[...] = _dot(h, w_ref[:, c:c + GLA_V])
    c += GLA_V
    mq_ref[...] = _dot(h, w_ref[:, c:c + MEM_WIDTH]).astype(BF16)
    c += MEM_WIDTH
    alr = _dot(h, w_ref[:, c:c + LANE]).astype(BF16)
    z = _dot(alr, wa_ref[...]) + ba_ref[...]
    log_sig = jnp.minimum(z, 0.0) - jnp.log1p(jnp.exp(-jnp.abs(z)))
    gl_ref[...] = log_sig * (1.0 / GLA_TAU)


def _inproj_a(x, g, w, wa, ba, tm):
    n, d = x.shape
    row = lambda i: (i, 0)
    outs = [
        jax.ShapeDtypeStruct((n, GLA_QK), F32), jax.ShapeDtypeStruct((n, GLA_QK), F32),
        jax.ShapeDtypeStruct((n, GLA_QK), F32), jax.ShapeDtypeStruct((n, GLA_V), BF16),
        jax.ShapeDtypeStruct((n, GLA_V), F32), jax.ShapeDtypeStruct((n, MEM_WIDTH), BF16),
    ]
    return pl.pallas_call(
        _inproj_a_kernel,
        out_shape=outs,
        grid=(n // tm,),
        in_specs=[pl.BlockSpec((tm, d), row), _const_spec((1, d)), _const_spec(w.shape),
                  _const_spec(wa.shape), _const_spec(ba.shape)],
        out_specs=[pl.BlockSpec((tm, s.shape[1]), row) for s in outs],
        compiler_params=_params(("parallel",)),
        name="inproj_a",
    )(x, g, w, wa, ba)


def _gla_constants():
    c = GLA_CHUNK
    w = np.zeros(((GLA_LEVELS + 2) * c, c), np.float32)
    masks = np.zeros((GLA_LEVELS + 1, c, c), np.float32)
    masks[0] = np.eye(c)
    for l in range(1, GLA_LEVELS + 1):
        blk, half = 2 ** l, 2 ** (l - 1)
        for i in range(c):
            mid = (i // blk) * blk + half - 1
            if i % blk >= half:
                w[(l - 1) * c + i, mid + 1:i + 1] = 1.0
            else:
                w[(l - 1) * c + i, i + 1:mid + 1] = 1.0
        for t in range(c):
            for s in range(c):
                if t // blk == s // blk and t % blk >= half and s % blk < half:
                    masks[l, t, s] = 1.0
    for i in range(c):
        w[GLA_LEVELS * c + i, :i + 1] = 1.0
        w[(GLA_LEVELS + 1) * c + i, i + 1:] = 1.0
    return np.concatenate([w, w], axis=1), np.tile(masks, (1, 1, GLA_HEADS))


def _gla_kernel(q_ref, k_ref, gl_ref, v_ref, r_ref, gh_ref, wcat_ref, msk_ref, o_ref, st_ref, *, n_chunks):
    c = GLA_CHUNK

    @pl.when(pl.program_id(1) == 0)
    def _():
        st_ref[...] = jnp.zeros_like(st_ref)

    lane_qk = lax.broadcasted_iota(jnp.int32, (1, GLA_QK), 1) // GLA_DK
    lane_v = lax.broadcasted_iota(jnp.int32, (1, GLA_V), 1) // GLA_DV
    row_qk = lax.broadcasted_iota(jnp.int32, (GLA_QK, 1), 0) // GLA_DK
    ones = jnp.ones((2 * c, LANE), BF16)
    wcat = wcat_ref[...]

    def stack_heads(x, lane_head):
        return jnp.concatenate([jnp.where(lane_head == h, x, jnp.zeros_like(x)) for h in range(GLA_HEADS)], axis=0)

    chunks = [slice(ci * c, (ci + 1) * c) for ci in range(n_chunks)]
    es, decs = [], []
    for rows in chunks:
        g = gl_ref[rows, :] * LOG2E
        g_hi = g.astype(BF16)
        g_split = jnp.concatenate([g_hi, (g - g_hi.astype(F32)).astype(BF16)], axis=0)
        es.append(jnp.exp2(_dot(wcat, g_split)))
        decs.append(jnp.exp2(_dot_tn(g_split, ones)))

    o_intras, q_ins, upds = [], [], []
    for rows, e in zip(chunks, es):
        q = q_ref[rows, :]
        k = k_ref[rows, :]
        v = v_ref[rows, :]
        attn = jnp.zeros((c, GLA_HEADS * c), F32)
        for l in range(GLA_LEVELS + 1):
            if l == 0:
                ql, kl = q, k
            else:
                el = e[(l - 1) * c:l * c]
                ql, kl = q * el, k * el
            attn = attn + _dot_nt(ql.astype(BF16), stack_heads(kl, lane_qk).astype(BF16)) * msk_ref[l]
        o_intras.append(_dot(attn.astype(BF16), stack_heads(v, lane_v)))
        q_ins.append((q * e[GLA_LEVELS * c:(GLA_LEVELS + 1) * c]).astype(BF16))
        kk = stack_heads(k * e[(GLA_LEVELS + 1) * c:(GLA_LEVELS + 2) * c], lane_qk).astype(BF16)
        v_rows = jnp.concatenate([v[:, h * GLA_DV:(h + 1) * GLA_DV] for h in range(GLA_HEADS)], axis=0)
        upds.append(_dot_tn(kk, v_rows))

    st = st_ref[...]
    outs = []
    for o_intra, q_in, dec, upd in zip(o_intras, q_ins, decs, upds):
        st_bd = jnp.concatenate([jnp.where(row_qk == h, st, 0.0).astype(BF16) for h in range(GLA_HEADS)], axis=1)
        outs.append(o_intra + _dot(q_in, st_bd))
        st = dec * st + upd
    st_ref[...] = st

    for rows, o in zip(chunks, outs):
        for h in range(GLA_HEADS):
            cs = slice(h * GLA_DV, (h + 1) * GLA_DV)
            on = _normalize(o[:, cs]) * gh_ref[...]
            r = r_ref[rows, cs]
            o_ref[rows, cs] = (on * (r * _sigmoid(r))).astype(BF16)


def _gla(q, k, gl, v, r, g_head, batch, ct):
    n = q.shape[0]
    t = n // batch
    nt = t // ct
    wcat, masks = _gla_constants()
    row = lambda b, i: (b * nt + i, 0)
    return pl.pallas_call(
        functools.partial(_gla_kernel, n_chunks=ct // GLA_CHUNK),
        out_shape=jax.ShapeDtypeStruct((n, GLA_V), BF16),
        grid=(batch, nt),
        in_specs=[pl.BlockSpec((ct, GLA_QK), row), pl.BlockSpec((ct, GLA_QK), row),
                  pl.BlockSpec((ct, GLA_QK), row), pl.BlockSpec((ct, GLA_V), row),
                  pl.BlockSpec((ct, GLA_V), row), _const_spec((1, GLA_DV)),
                  _const_spec(wcat.shape), _const_spec(masks.shape)],
        out_specs=pl.BlockSpec((ct, GLA_V), row),
        scratch_shapes=[pltpu.VMEM((GLA_QK, GLA_DV), F32)],
        compiler_params=_params(("parallel", "arbitrary")),
        name="gla",
    )(q, k, gl, v, r, g_head, jnp.asarray(wcat, BF16), jnp.asarray(masks, F32))


def _attn_out_kernel(o_ref, mq_ref, mem_ref, gmem_ref, wmem_ref, w_ref, x_ref, out_ref, mkv_ref):
    @pl.when(pl.program_id(1) == 0)
    def _():
        hm = (_normalize(mem_ref[0]) * gmem_ref[...]).astype(BF16)
        mkv_ref[...] = _dot(hm, wmem_ref[...]).astype(BF16)

    heads = [slice(h * MEM_DH, (h + 1) * MEM_DH) for h in range(MEM_HEADS)]
    scores = [_dot_nt(mq_ref[:, cs], mkv_ref[:, cs]) * (MEM_DH ** -0.5 * LOG2E) for cs in heads]
    parts = [o_ref[...]]
    for s, cs in zip(scores, heads):
        e = jnp.exp2(s - jnp.max(s, axis=-1, keepdims=True))
        p = e / jnp.sum(e, axis=-1, keepdims=True)
        values = mkv_ref[:, MEM_WIDTH + cs.start:MEM_WIDTH + cs.stop]
        parts.append(_dot(p.astype(BF16), values).astype(BF16))
    cat = jnp.concatenate(parts, axis=-1)
    out_ref[...] = x_ref[...] + _dot(cat, w_ref[...])


def _attn_out(o, mq, mem, g_mem, w_mem, w_out, x, tm):
    n, d = x.shape
    batch, m, _ = mem.shape
    nt = n // batch // tm
    row = lambda b, i: (b * nt + i, 0)
    return pl.pallas_call(
        _attn_out_kernel,
        out_shape=jax.ShapeDtypeStruct((n, d), F32),
        grid=(batch, nt),
        in_specs=[pl.BlockSpec((tm, o.shape[1]), row), pl.BlockSpec((tm, MEM_WIDTH), row),
                  pl.BlockSpec((1, m, d), lambda b, i: (b, 0, 0)), _const_spec((1, d)), _const_spec(w_mem.shape),
                  _const_spec(w_out.shape), pl.BlockSpec((tm, d), row)],
        out_specs=pl.BlockSpec((tm, d), row),
        scratch_shapes=[pltpu.VMEM((m, 2 * MEM_WIDTH), BF16)],
        compiler_params=_params(("parallel", "arbitrary")),
        name="attn_out",
    )(o, mq, mem, g_mem, w_mem, w_out, x)


def _ffn_kernel(x_ref, g_ref, wup_ref, cw_ref, cb_ref, wd_ref, gf_ref, out_ref, act_ref, tail_ref, *,
                final_norm):
    @pl.when(pl.program_id(1) == 0)
    def _():
        tail_ref[...] = jnp.zeros_like(tail_ref)

    x = x_ref[...]
    tm = x.shape[0]
    h = (_normalize(x) * g_ref[...]).astype(BF16)
    rid = lax.broadcasted_iota(jnp.int32, (8, FFN_CHUNK), 0)
    for j in range(FFN_DIM // FFN_CHUNK):
        cs = slice(j * FFN_CHUNK, (j + 1) * FFN_CHUNK)
        a = _dot(h, wup_ref[:, cs])
        b = _dot(h, wup_ref[:, FFN_DIM + j * FFN_CHUNK:FFN_DIM + (j + 1) * FFN_CHUNK])
        tail = tail_ref[j]
        r1 = pltpu.roll(a, 1, 0)
        r2 = pltpu.roll(a, 2, 0)
        top1 = jnp.where(rid == 0, tail[7:8], r1[:8])
        top2 = jnp.where(rid == 0, tail[6:7], jnp.where(rid == 1, tail[7:8], r2[:8]))
        a1 = jnp.concatenate([top1, r1[8:]], axis=0)
        a2 = jnp.concatenate([top2, r2[8:]], axis=0)
        tail_ref[j] = a[tm - 8:]
        ac = a2 * cw_ref[0:1, cs] + a1 * cw_ref[1:2, cs] + a * cw_ref[2:3, cs] + cb_ref[:, cs]
        act_ref[:, cs] = (ac * _sigmoid(ac) * b).astype(BF16)
    y = x + _dot(act_ref[...], wd_ref[...])
    if final_norm:
        y = _normalize(y) * gf_ref[...]
    out_ref[...] = y


def _ffn(x, g, w_up, conv_w, conv_b, w_down, g_final, batch, tm, final_norm):
    n, d = x.shape
    nt = n // batch // tm
    row = lambda b, i: (b * nt + i, 0)
    once = pl.Buffered(1)
    return pl.pallas_call(
        functools.partial(_ffn_kernel, final_norm=final_norm),
        out_shape=jax.ShapeDtypeStruct((n, d), F32),
        grid=(batch, nt),
        in_specs=[pl.BlockSpec((tm, d), row), _const_spec((1, d)),
                  pl.BlockSpec(w_up.shape, lambda b, i: (0, 0), pipeline_mode=once),
                  _const_spec(conv_w.shape), _const_spec(conv_b.shape),
                  pl.BlockSpec(w_down.shape, lambda b, i: (0, 0), pipeline_mode=once),
                  _const_spec((1, d))],
        out_specs=pl.BlockSpec((tm, d), row),
        scratch_shapes=[pltpu.VMEM((tm, FFN_DIM), BF16),
                        pltpu.VMEM((FFN_DIM // FFN_CHUNK, 8, FFN_CHUNK), F32)],
        compiler_params=_params(("parallel", "arbitrary")),
        name="conv_ffn",
    )(x, g, w_up, conv_w, conv_b, w_down, g_final)


def _proj_b_kernel(x_ref, gkv_ref, gmix_ref, wkv_ref, wqt_ref, wgt_ref, wmq_ref,
                   cmp_ref, kvs_ref, qt_ref, gt_ref, mq_ref, regroup_ref):
    xn = _normalize(x_ref[...])
    hkv = (xn * gkv_ref[...]).astype(BF16)
    h1 = (xn * gmix_ref[...]).astype(BF16)
    kv_cmp = _dot(hkv, wkv_ref[:, 0:2 * KV_WIDTH])
    n_sub = x_ref.shape[0] // CMP_STRIDE
    for s in range(2):
        regroup_ref[s] = kv_cmp[:, s * KV_WIDTH:(s + 1) * KV_WIDTH]
        for p in range(CMP_STRIDE):
            cmp_ref[s, :, p * KV_WIDTH:(p + 1) * KV_WIDTH] = \
                regroup_ref[s, pl.ds(p, n_sub, stride=CMP_STRIDE), :].astype(BF16)
    kvs_ref[...] = _dot(hkv, wkv_ref[:, 2 * KV_WIDTH:]).astype(BF16)
    qt_ref[0] = (_dot_nt(wqt_ref[...], h1) * (NSA_DH ** -0.5 * LOG2E)).astype(BF16)
    gt_ref[0] = _sigmoid(_dot_nt(wgt_ref[...], h1))
    mq_ref[...] = _dot(h1, wmq_ref[...]).astype(BF16)


def _proj_b(x, g_kv, g_mix, w_kv, wqt, wgt, wmq, batch, tm):
    n, d = x.shape
    t = n // batch
    nt = t // tm
    row = lambda b, i: (b * nt + i, 0)
    outs = [
        jax.ShapeDtypeStruct((2, n // CMP_STRIDE, CMP_STRIDE * KV_WIDTH), BF16),
        jax.ShapeDtypeStruct((n, 4 * KV_WIDTH), BF16),
        jax.ShapeDtypeStruct((batch, NSA_WIDTH, t), BF16),
        jax.ShapeDtypeStruct((batch, wgt.shape[0], t), F32),
        jax.ShapeDtypeStruct((n, MEM_WIDTH), BF16),
    ]
    return pl.pallas_call(
        _proj_b_kernel,
        out_shape=outs,
        grid=(batch, nt),
        in_specs=[pl.BlockSpec((tm, d), row), _const_spec((1, d)), _const_spec((1, d)),
                  _const_spec(w_kv.shape), _const_spec(wqt.shape), _const_spec(wgt.shape),
                  _const_spec(wmq.shape)],
        out_specs=[pl.BlockSpec((2, tm // CMP_STRIDE, CMP_STRIDE * KV_WIDTH), lambda b, i: (0, b * nt + i, 0)),
                   pl.BlockSpec((tm, 4 * KV_WIDTH), row),
                   pl.BlockSpec((1, NSA_WIDTH, tm), lambda b, i: (b, 0, i)),
                   pl.BlockSpec((1, wgt.shape[0], tm), lambda b, i: (b, 0, i)),
                   pl.BlockSpec((tm, MEM_WIDTH), row)],
        scratch_shapes=[pltpu.VMEM((2, tm, KV_WIDTH), F32)],
        compiler_params=_params(("parallel", "parallel")),
        name="proj_b",
    )(x, g_kv, g_mix, w_kv, wqt, wgt, wmq)


def _compress_kernel(a_ref, w1_ref, pe_ref, w2_ref, o_ref):
    a = a_ref[0, 0]
    w1 = w1_ref[0]
    u = _dot(a, w1)
    cpe = _dot(pe_ref[0], w1)
    nrow = a.shape[0]
    hid = []
    for g in range(NSA_GROUPS):
        c0 = slice((2 * g) * CMP_HIDDEN, (2 * g + 1) * CMP_HIDDEN)
        c1 = slice((2 * g + 1) * CMP_HIDDEN, (2 * g + 2) * CMP_HIDDEN)
        nxt = pltpu.roll(u[:, c1], nrow - 1, 0)
        hid.append(u[:, c0] + nxt + cpe[0:1, c0] + cpe[8:9, c1])
    hcat = jax.nn.gelu(jnp.concatenate(hid, axis=-1), approximate=True).astype(BF16)
    o_ref[0, 0] = _dot(hcat, w2_ref[0]).astype(BF16)


def _compress(a, w1, pe, w2):
    _, batch, nsub, width = a.shape
    return pl.pallas_call(
        _compress_kernel,
        out_shape=jax.ShapeDtypeStruct((2, batch, nsub, KV_WIDTH), BF16),
        grid=(2, batch),
        in_specs=[pl.BlockSpec((1, 1, nsub, width), lambda s, b: (s, b, 0, 0)),
                  pl.BlockSpec((1,) + w1.shape[1:], lambda s, b: (s, 0, 0)),
                  pl.BlockSpec((1,) + pe.shape[1:], lambda s, b: (s, 0, 0)),
                  pl.BlockSpec((1,) + w2.shape[1:], lambda s, b: (s, 0, 0))],
        out_specs=pl.BlockSpec((1, 1, nsub, KV_WIDTH), lambda s, b: (s, b, 0, 0)),
        compiler_params=_params(("parallel", "parallel")),
        name="compress",
    )(a, w1, pe, w2)


def _alibi_tiles(ncmp_pad):
    slopes = LOG2E * 2.0 ** (-8.0 * np.arange(1, NSA_HEADS + 1) / NSA_HEADS)
    tl = np.tile(np.arange(Q_BLOCK), NSA_HPG)[None, :]
    cmp_end = (np.arange(ncmp_pad) * CMP_STRIDE + CMP_BLOCK - 1)[:, None]
    key = np.arange(SEL_BLOCK)[:, None]
    dist_win = tl + WINDOW - np.arange(WIN_KEYS)[:, None]
    c_cmp = np.zeros((NSA_GROUPS, ncmp_pad, NSA_HPG * Q_BLOCK), np.float32)
    c_blk = np.zeros((NSA_GROUPS, SEL_BLOCK, NSA_HPG * Q_BLOCK), np.float32)
    c_win = np.zeros((NSA_GROUPS, WIN_KEYS, NSA_HPG * Q_BLOCK), np.float32)
    srow = np.zeros((NSA_GROUPS, 1, NSA_HPG * Q_BLOCK), np.float32)
    for g in range(NSA_GROUPS):
        s = np.repeat(slopes[g * NSA_HPG:(g + 1) * NSA_HPG], Q_BLOCK)[None, :]
        c_cmp[g] = -s * (tl - cmp_end)
        c_blk[g] = -s * (tl - key)
        c_win[g] = np.where((dist_win >= 0) & (dist_win < WINDOW), -s * dist_win, -np.inf)
        srow[g] = s
    d_cmp = (cmp_end - tl).astype(np.float32)
    return c_cmp, d_cmp, c_blk, c_win, srow


def _group_queries(qt_ref, g, tile=0):
    zeros = jnp.zeros((NSA_DH, Q_BLOCK), BF16)
    cols = []
    for hh in range(NSA_HPG):
        h = g * NSA_HPG + hh
        piece = qt_ref[0, h * NSA_DH:(h + 1) * NSA_DH, tile * Q_BLOCK:(tile + 1) * Q_BLOCK]
        cols.append(jnp.concatenate([piece, zeros] if g == 0 else [zeros, piece], axis=0))
    return jnp.concatenate(cols, axis=1)


def _nsa_a_kernel(qt_ref, ckv_ref, ccmp_ref, dcmp_ref, srow_ref, ovt_ref, oc_ref, sel_ref, cnt_ref, imp_ref):
    first_tile = pl.program_id(1) * SEL_TILES
    nblk = sel_ref.shape[3]
    jidx = lax.broadcasted_iota(jnp.int32, (nblk, Q_BLOCK), 0)
    tl = lax.broadcasted_iota(jnp.int32, (nblk, Q_BLOCK), 1)
    units = [(k, g) for k in range(SEL_TILES) for g in range(NSA_GROUPS)]
    qbs = [first_tile + k for k in range(SEL_TILES)]
    bases = [(qb * Q_BLOCK).astype(F32) for qb in qbs]
    curs = [2 * qb + jnp.where(tl >= SEL_BLOCK, 1, 0) for qb in qbs]
    n_chunks = (qbs[-1] * (Q_BLOCK // CMP_STRIDE) + (Q_BLOCK - CMP_BLOCK) // CMP_STRIDE) // CMP_CHUNK + 1
    qts = {(k, g): _group_queries(qt_ref, g, k) for k, g in units}
    shifts = {(k, g): srow_ref[g] * bases[k] for k, g in units}

    def cmp_branch(n_rows):
        kc = ckv_ref[0, 0, 0:n_rows, :]
        vc = ckv_ref[1, 0, 0:n_rows, :]
        ov = ovt_ref[:, 0:n_rows]
        lane_grp = lax.broadcasted_iota(jnp.int32, vc.shape, 1) // NSA_DH
        raws = {u: _dot(kc, qts[u]) for u in units}
        for k, g in units:
            visible = dcmp_ref[0:n_rows, :] <= bases[k]
            s = raws[k, g] + jnp.where(visible, ccmp_ref[g, 0:n_rows, :] - shifts[k, g], NEG_INF)
            m = jnp.max(s, axis=0, keepdims=True)
            e = jnp.exp2((s - jnp.where(m == NEG_INF, 0.0, m)).astype(BF16))
            pv = _dot_tn(jnp.where(lane_grp == g, vc, jnp.ones_like(vc)), e)
            inv = 1.0 / jnp.maximum(pv[(1 - g) * NSA_DH:(1 - g) * NSA_DH + 1], 1e-30)
            oc = pv[g * NSA_DH:(g + 1) * NSA_DH] * inv
            imp = _dot(ov, e) * inv
            for hh in range(NSA_HPG):
                h = g * NSA_HPG + hh
                oc_ref[0, h * NSA_DH:(h + 1) * NSA_DH, k * Q_BLOCK:(k + 1) * Q_BLOCK] = \
                    oc[:, hh * Q_BLOCK:(hh + 1) * Q_BLOCK]
            imp_g = imp[:, 0:Q_BLOCK]
            for hh in range(1, NSA_HPG):
                imp_g = imp_g + imp[:, hh * Q_BLOCK:(hh + 1) * Q_BLOCK]
            imp_ref[k, g] = imp_g

    for nc in range(1, ckv_ref.shape[2] // CMP_CHUNK + 1):
        pl.when(n_chunks == nc)(functools.partial(cmp_branch, nc * CMP_CHUNK))

    def emit(k, g, sel):
        sel_ref[0, g, k] = sel
        cnt_ref[0, g, k] = _dot_nt(jnp.ones((8, Q_BLOCK), BF16), sel.astype(BF16))

    @pl.when(first_tile < RANK_FIRST_TILE)
    def _():
        for k, g in units:
            emit(k, g, jnp.where(jidx <= curs[k], 1.0, 0.0))

    def rank_branch(rows):
        jr = lax.broadcasted_iota(jnp.int32, (rows, Q_BLOCK), 0)
        jf = jr.astype(F32)
        cur_r = [2 * qb + jnp.where(lax.broadcasted_iota(jnp.int32, (rows, Q_BLOCK), 1) >= SEL_BLOCK, 1, 0)
                 for qb in qbs]
        v0 = {(k, g): jnp.where((jr >= 1) & (jr <= cur_r[k] - 2), imp_ref[k, g, 0:rows, :], NEG_INF)
              for k, g in units}
        v = dict(v0)
        for _ in range(SEL_TOPN - 3):
            for u in units:
                mx = jnp.max(v[u], axis=0, keepdims=True)
                first = jnp.min(jnp.where(v[u] == mx, jf, float(nblk)), axis=0, keepdims=True)
                v[u] = jnp.where(jf == first, NEG_INF, v[u])
        for k, g in units:
            cur = cur_r[k]
            taken = (v[k, g] != v0[k, g]) | (jr == 0) | (jr == cur) | (jr == cur - 1)
            sel = jnp.where(taken, 1.0, 0.0)
            if rows < nblk:
                sel = jnp.concatenate([sel, jnp.zeros((nblk - rows, Q_BLOCK), F32)], axis=0)
            emit(k, g, sel)

    @pl.when(first_tile >= RANK_FIRST_TILE)
    def _():
        blocks_per_variant = CMP_CHUNK * CMP_STRIDE // SEL_BLOCK
        for nc in range(1, ckv_ref.shape[2] // CMP_CHUNK + 1):
            pl.when(n_chunks == nc)(functools.partial(rank_branch, min(nblk, nc * blocks_per_variant)))


def _nsa_a(qt, ckv, c_cmp, d_cmp, srow, ovt):
    batch, _, t = qt.shape
    nqb = t // Q_BLOCK
    ns = t // SEL_BLOCK
    ncmp = ckv.shape[2]
    outs = [jax.ShapeDtypeStruct((batch, NSA_WIDTH, t), F32),
            jax.ShapeDtypeStruct((batch, NSA_GROUPS, nqb, ns, Q_BLOCK), F32),
            jax.ShapeDtypeStruct((batch, NSA_GROUPS, nqb, 8, ns), F32)]
    assert nqb % SEL_TILES == 0 and RANK_FIRST_TILE % SEL_TILES == 0
    return pl.pallas_call(
        _nsa_a_kernel,
        out_shape=outs,
        grid=(batch, nqb // SEL_TILES),
        in_specs=[pl.BlockSpec((1, NSA_WIDTH, SEL_TILES * Q_BLOCK), lambda b, i: (b, 0, i)),
                  pl.BlockSpec((2, 1, ncmp, KV_WIDTH), lambda b, i: (0, b, 0, 0)),
                  _const_spec(c_cmp.shape), _const_spec(d_cmp.shape), _const_spec(srow.shape),
                  _const_spec(ovt.shape)],
        out_specs=[pl.BlockSpec((1, NSA_WIDTH, SEL_TILES * Q_BLOCK), lambda b, i: (b, 0, i)),
                   pl.BlockSpec((1, NSA_GROUPS, SEL_TILES, ns, Q_BLOCK), lambda b, i: (b, 0, i, 0, 0)),
                   pl.BlockSpec((1, NSA_GROUPS, SEL_TILES, 8, ns), lambda b, i: (b, 0, i, 0, 0))],
        scratch_shapes=[pltpu.VMEM((SEL_TILES, NSA_GROUPS, ns, Q_BLOCK), F32)],
        compiler_params=_params(("parallel", "parallel")),
        name="nsa_select",
    )(qt, ckv, c_cmp, d_cmp, srow, ovt)


def _block_softmax_step(carry, raw, consts, shifts, v_rows, g):
    m, l, acc = carry
    width = raw.shape[1]
    fold = lambda x: x.reshape(SEL_BLOCK // 8, 8, width)
    us = [raw[i * SEL_BLOCK:(i + 1) * SEL_BLOCK] + c for i, c in enumerate(consts)]
    top8 = None
    for u, sh in zip(us, shifts):
        t = jnp.max(fold(u), axis=0) + sh
        top8 = t if top8 is None else jnp.maximum(top8, t)
    m_new = jnp.maximum(m, jnp.max(top8, axis=0, keepdims=True))
    m_safe = jnp.where(m_new == NEG_INF, 0.0, m_new)
    alpha = jnp.where(m == NEG_INF, 0.0, jnp.exp2(m - m_safe))
    ps = [jnp.exp2((u - (m_safe - sh)).astype(BF16)) for u, sh in zip(us, shifts)]
    lane_grp = lax.broadcasted_iota(jnp.int32, v_rows.shape, 1) // NSA_DH
    v_aug = jnp.where(lane_grp == g, v_rows, jnp.ones_like(v_rows))
    pv = _dot_tn(v_aug, jnp.concatenate(ps, axis=0))
    og = 1 - g
    return m_new, alpha * l + pv[og * NSA_DH:og * NSA_DH + 1], alpha * acc + pv[g * NSA_DH:(g + 1) * NSA_DH]


def _nsa_b_kernel(idx_ref, qt_ref, ks_ref, vs_ref, kw_ref, vw_ref, sel_ref, oc_ref, gt_ref, cblk_ref, cwin_ref,
                  srow_ref, o_ref, *, idx_words):
    b = pl.program_id(0)
    first_tile = pl.program_id(1) * ATT_TILES
    width = NSA_HPG * Q_BLOCK
    units = [(k, g) for k in range(ATT_TILES) for g in range(NSA_GROUPS)]
    qbs = [first_tile + k for k in range(ATT_TILES)]
    bases = [qb * Q_BLOCK for qb in qbs]
    qts = {(k, g): _group_queries(qt_ref, g, k) for k, g in units}
    srows = [srow_ref[g] for g in range(NSA_GROUPS)]
    n_tiles = pl.num_programs(1) * ATT_TILES
    entries = {(k, g): ((b * NSA_GROUPS + g) * n_tiles + qbs[k]) * idx_words for k, g in units}
    n_act = {u: idx_ref[entries[u]] for u in units}
    init = (jnp.full((1, width), NEG_INF, F32), jnp.zeros((1, width), F32), jnp.zeros((NSA_DH, width), F32))
    zero_row = jnp.zeros((1, width), F32)

    def gate_row(k, g, j, live):
        picked = sel_ref[0, g, k, pl.ds(j, 1), :]
        gate = jnp.where((picked > 0.0) & live, 0.0, NEG_INF)
        return jnp.concatenate([gate] * NSA_HPG, axis=1)

    def stage(it):
        out = {}
        for k, g in units:
            words = [idx_ref[entries[k, g] + 1 + it * SLC_WORDS + w] for w in range(SLC_WORDS)]
            kts, vts, shifts = [], [], []
            for i in range(SLC_CHUNK):
                live = it * SLC_CHUNK + i < n_act[k, g]
                j = jnp.where(live, lax.shift_right_logical(words[i // 4], 8 * (i % 4)) & 0xFF, 0)
                k0 = pl.multiple_of(j * SEL_BLOCK, SEL_BLOCK)
                kts.append(ks_ref[pl.ds(k0, SEL_BLOCK), :])
                vts.append(vs_ref[pl.ds(k0, SEL_BLOCK), :])
                shifts.append(gate_row(k, g, j, live) - srows[g] * (bases[k] - k0).astype(F32))
            out[k, g] = (_dot(jnp.concatenate(kts, axis=0), qts[k, g]), shifts, jnp.concatenate(vts, axis=0))
        return out

    def body(it, states):
        staged = stage(it)
        return tuple(_block_softmax_step(state, staged[u][0], [cblk_ref[u[1]]] * SLC_CHUNK, staged[u][1],
                                         staged[u][2], u[1]) for u, state in zip(units, states))

    longest = functools.reduce(jnp.maximum, [n_act[u] for u in units])
    far = lax.fori_loop(0, (longest + SLC_CHUNK - 1) // SLC_CHUNK, body, (init,) * len(units))
    slc = dict(zip(units, far))

    diag = range(WINDOW // SEL_BLOCK, WIN_KEYS // SEL_BLOCK)
    win = {u: init for u in units}
    steps = []
    for k, g in units:
        r0 = pl.multiple_of(bases[k], Q_BLOCK)
        shifts = [gate_row(k, g, 2 * qbs[k] + i, True) for i in range(len(diag))]
        steps.append((slc, (k, g), ks_ref[pl.ds(r0, Q_BLOCK), :], vs_ref[pl.ds(r0, Q_BLOCK), :], diag, shifts))
    for k in range(ATT_TILES):
        for c in range(WIN_KEYS // WIN_CHUNK):
            blocks = range(c * (WIN_CHUNK // SEL_BLOCK), (c + 1) * (WIN_CHUNK // SEL_BLOCK))
            kws, vws, offs = [], [], []
            for r in blocks:
                j = 2 * qbs[k] - WINDOW // SEL_BLOCK + r
                k0 = pl.multiple_of(jnp.maximum(j, 0) * SEL_BLOCK, SEL_BLOCK)
                kws.append(kw_ref[pl.ds(k0, SEL_BLOCK), :])
                vws.append(vw_ref[pl.ds(k0, SEL_BLOCK), :])
                offs.append(zero_row + jnp.where(j >= 0, 0.0, NEG_INF))
            kw = jnp.concatenate(kws, axis=0)
            vw = jnp.concatenate(vws, axis=0)
            for g in range(NSA_GROUPS):
                steps.append((win, (k, g), kw, vw, blocks, offs))
    raws = [_dot(keys, qts[u]) for _, u, keys, _, _, _ in steps]
    for raw, (states, u, _, values, blocks, shifts) in zip(raws, steps):
        consts = [cwin_ref[u[1], r * SEL_BLOCK:(r + 1) * SEL_BLOCK, :] for r in blocks]
        states[u] = _block_softmax_step(states[u], raw, consts, shifts, values, u[1])

    for k in range(ATT_TILES):
        cols = slice(k * Q_BLOCK, (k + 1) * Q_BLOCK)
        heads_out = []
        for g in range(NSA_GROUPS):
            o_s = slc[k, g][2] * (1.0 / jnp.maximum(slc[k, g][1], 1e-30))
            o_w = win[k, g][2] * (1.0 / jnp.maximum(win[k, g][1], 1e-30))
            for hh in range(NSA_HPG):
                h = g * NSA_HPG + hh
                cs = slice(hh * Q_BLOCK, (hh + 1) * Q_BLOCK)
                gates = gt_ref[0, 3 * h:3 * h + 3, cols]
                heads_out.append(gates[0:1] * oc_ref[0, h * NSA_DH:(h + 1) * NSA_DH, cols]
                                 + gates[1:2] * o_s[:, cs] + gates[2:3] * o_w[:, cs])
        o_ref[cols, :] = jnp.concatenate(heads_out, axis=0).astype(BF16).T


def _nsa_b(idx, qt, kvs, sel, oc, gt, c_blk, c_win, srow):
    batch, _, t = qt.shape
    nqb = t // Q_BLOCK
    ns = t // SEL_BLOCK
    kv_spec = lambda c: pl.BlockSpec((t, KV_WIDTH), lambda b, i, idx_ref: (b, c))
    grid_spec = pltpu.PrefetchScalarGridSpec(
        num_scalar_prefetch=1,
        grid=(batch, nqb // ATT_TILES),
        in_specs=[pl.BlockSpec((1, NSA_WIDTH, ATT_TILES * Q_BLOCK), lambda b, i, r: (b, 0, i)),
                  kv_spec(0), kv_spec(1), kv_spec(2), kv_spec(3),
                  pl.BlockSpec((1, NSA_GROUPS, ATT_TILES, ns, Q_BLOCK), lambda b, i, r: (b, 0, i, 0, 0)),
                  pl.BlockSpec((1, NSA_WIDTH, ATT_TILES * Q_BLOCK), lambda b, i, r: (b, 0, i)),
                  pl.BlockSpec((1, gt.shape[1], ATT_TILES * Q_BLOCK), lambda b, i, r: (b, 0, i)),
                  pl.BlockSpec(c_blk.shape, lambda b, i, r: (0, 0, 0)),
                  pl.BlockSpec(c_win.shape, lambda b, i, r: (0, 0, 0)),
                  pl.BlockSpec(srow.shape, lambda b, i, r: (0, 0, 0))],
        out_specs=pl.BlockSpec((ATT_TILES * Q_BLOCK, NSA_WIDTH), lambda b, i, r: (b * (nqb // ATT_TILES) + i, 0)),
    )
    assert nqb % ATT_TILES == 0
    return pl.pallas_call(
        functools.partial(_nsa_b_kernel, idx_words=1 + _slc_steps(ns) * SLC_WORDS),
        out_shape=jax.ShapeDtypeStruct((batch * t, NSA_WIDTH), BF16),
        grid_spec=grid_spec,
        compiler_params=_params(("parallel", "parallel")),
        name="nsa_attend",
    )(idx, qt, kvs, kvs, kvs, kvs, sel, oc, gt, c_blk, c_win, srow)


def _pad_cols(w, width):
    return jnp.pad(w, ((0, 0), (0, width - w.shape[1])))


def _compress_weights(w1, pe, w2):
    nsub = CMP_STRIDE
    w1r = w1.reshape(2, nsub, NSA_DH, CMP_HIDDEN)
    same_group = jnp.eye(NSA_GROUPS, dtype=F32)
    big = same_group[None, :, None, :, None, None] * w1r.transpose(1, 2, 0, 3)[:, None, :, None, :, :]
    big = big.reshape(nsub * KV_WIDTH, NSA_GROUPS * 2 * CMP_HIDDEN)
    per = pe.reshape(2, nsub, 1, NSA_DH)
    pe2 = jnp.broadcast_to(per, (2, nsub, NSA_GROUPS, NSA_DH)).reshape(2, 1, nsub * KV_WIDTH)
    pe2 = jnp.broadcast_to(pe2, (2, 8, nsub * KV_WIDTH)).reshape(16, nsub * KV_WIDTH)
    w2bd = same_group[:, None, :, None] * w2[None, :, None, :]
    return big.astype(BF16), pe2.astype(BF16), w2bd.reshape(NSA_GROUPS * CMP_HIDDEN, KV_WIDTH).astype(BF16)


def _selection_overlap_t(ncmp_pad, ns):
    cs = np.arange(ncmp_pad) * CMP_STRIDE
    ss = np.arange(ns) * SEL_BLOCK
    ov = np.minimum(cs[:, None] + CMP_BLOCK, ss[None, :] + SEL_BLOCK) - np.maximum(cs[:, None], ss[None, :])
    return (np.clip(ov, 0, None).astype(np.float32) / CMP_BLOCK).T


def _slc_steps(ns):
    return -(-ns // SLC_CHUNK)


def _active_blocks(cnt):
    ns = cnt.shape[-1]
    before_tile = jnp.arange(ns)[None, :] < (Q_BLOCK // SEL_BLOCK) * jnp.arange(cnt.shape[2])[:, None]
    flags = (cnt[:, :, :, 0, :] > 0.0) & before_tile
    blocks = jnp.arange(ns, dtype=jnp.int32)
    order = jnp.sort(jnp.where(flags, blocks, blocks + ns), axis=-1) % ns
    n_act = jnp.sum(flags, axis=-1, dtype=jnp.int32)
    lead = order.shape[:-1]
    steps = _slc_steps(ns)
    order = jnp.pad(order, [(0, 0)] * len(lead) + [(0, steps * SLC_CHUNK - ns)]).reshape(lead + (steps, SLC_CHUNK))
    order = jnp.pad(order, [(0, 0)] * (len(lead) + 1) + [(0, 4 * SLC_WORDS - SLC_CHUNK)])
    packed = jnp.sum(order.reshape(lead + (steps * SLC_WORDS, 4)) << (8 * jnp.arange(4, dtype=jnp.int32)), axis=-1)
    return jnp.concatenate([n_act[..., None], packed], axis=-1).reshape(-1)


def kernel(x, mem, g_mix, g_ffn, g_mem, w_mem_kv, w_up, conv_w, conv_b, w_down,
           a_w_in, a_w_alpha, a_b_alpha, a_g_head, a_w_out,
           g_kv, w_kv, pe_k, pe_v, w_ck1, w_ck2, w_cv1, w_cv2,
           b_w_in, b_w_out, g_final):
    batch, t, d = x.shape
    n = batch * t
    tm = min(1024, t)
    xf = x.reshape(n, d)
    row = lambda v: v.reshape(1, -1)

    wa = a_w_in[0]
    c_alr = 2 * GLA_QK + 2 * GLA_V
    w_a = jnp.concatenate([wa[:, :c_alr], wa[:, c_alr + GLA_RANK:], _pad_cols(wa[:, c_alr:c_alr + GLA_RANK], LANE)],
                          axis=1).astype(BF16)
    w_alpha = jnp.pad(a_w_alpha[0], ((0, LANE - GLA_RANK), (0, 0))).astype(BF16)
    q, k, gl, v, r, mq = _inproj_a(xf, row(g_mix[0]), w_a, w_alpha, row(a_b_alpha[0]), tm)
    o = _gla(q, k, gl, v, r, row(a_g_head[0]), batch, tm)
    xf = _attn_out(o, mq, mem, row(g_mem[0]), w_mem_kv[0].astype(BF16), a_w_out[0].astype(BF16), xf, tm)
    xf = _ffn(xf, row(g_ffn[0]), w_up[0].astype(BF16), conv_w[0], row(conv_b[0]), w_down[0].astype(BF16),
              row(g_final), batch, tm, final_norm=False)

    wb = b_w_in[0]
    wqt = wb[:, :NSA_WIDTH].T.astype(BF16)
    n_gate = 3 * NSA_HEADS
    wgt = jnp.pad(wb[:, NSA_WIDTH:NSA_WIDTH + n_gate].T, ((0, 32 - n_gate), (0, 0))).astype(BF16)
    wmq = wb[:, NSA_WIDTH + n_gate:].astype(BF16)
    ckv_in, kvs, qt, gt, mq1 = _proj_b(xf, row(g_kv), row(g_mix[1]), w_kv.astype(BF16), wqt, wgt, wmq, batch, tm)
    nsub = t // CMP_STRIDE
    w1k, pek, w2k = _compress_weights(w_ck1, pe_k, w_ck2)
    w1v, pev, w2v = _compress_weights(w_cv1, pe_v, w_cv2)
    ckv = _compress(ckv_in.reshape(2, batch, nsub, CMP_STRIDE * KV_WIDTH),
                    jnp.stack([w1k, w1v]), jnp.stack([pek, pev]), jnp.stack([w2k, w2v]))

    ns = t // SEL_BLOCK
    c_cmp, d_cmp, c_blk, c_win, srow = _alibi_tiles(nsub)
    ovt = jnp.asarray(_selection_overlap_t(nsub, ns), BF16)
    oc, sel, cnt = _nsa_a(qt, ckv, jnp.asarray(c_cmp), jnp.asarray(d_cmp), jnp.asarray(srow), ovt)
    idx = _active_blocks(cnt)
    o1 = _nsa_b(idx, qt, kvs, sel, oc, gt, jnp.asarray(c_blk), jnp.asarray(c_win), jnp.asarray(srow))
    xf = _attn_out(o1, mq1, mem, row(g_mem[1]), w_mem_kv[1].astype(BF16), b_w_out[0].astype(BF16), xf, tm)
    xf = _ffn(xf, row(g_ffn[1]), w_up[1].astype(BF16), conv_w[1], row(conv_b[1]), w_down[1].astype(BF16),
              row(g_final), batch, tm, final_norm=True)
    return xf.reshape(batch, t, d)
```
